```python
import jax, jax.numpy as jnp
from jax import lax
import numpy as np

D_MODEL = 1024
BATCH = 8
SEQ = 2048
DEPTH = 2

D_MIX = D_MODEL
POOL_WINDOWS = (2, 4, 8, 16)
POOL_GROUPS = len(POOL_WINDOWS)
POOL_WIDTH = D_MIX // 4
POOL_CH = POOL_WIDTH // POOL_GROUPS
SG_HEADS = 4
SG_WIDTH = D_MIX // 4
SG_DIM = SG_WIDTH // SG_HEADS
SG_CHUNK = 128
MLA_HEADS = 8
MLA_WIDTH = D_MIX - POOL_WIDTH - SG_WIDTH
V_DIM = MLA_WIDTH // MLA_HEADS
QK_NOPE = 64
QK_ROPE = 32
Q_LORA = D_MODEL // 4
KV_LORA = D_MODEL // 8
ROPE_THETA = 10000.0
Q_BLOCK = 128
ATTN_SCALE = (QK_NOPE + QK_ROPE) ** -0.5
IN_SPLITS = (POOL_WIDTH,
             POOL_WIDTH + SG_WIDTH,
             POOL_WIDTH + 2 * SG_WIDTH,
             POOL_WIDTH + 2 * SG_WIDTH + Q_LORA,
             POOL_WIDTH + 2 * SG_WIDTH + Q_LORA + KV_LORA)
N_IN = POOL_WIDTH + 2 * SG_WIDTH + Q_LORA + KV_LORA + QK_ROPE
N_EXPERTS = 64
TOP_K = 8
N_EXPERT_GROUPS = 8
TOPK_GROUPS = 4
EXPERT_FF = D_MODEL // 4
SHARED_FF = EXPERT_FF
ROUTED_SCALE = 2.5
EXPERT_BLOCK = 128
EPS = 1e-6
MAX_POS_OFFSET = 1024

kernel_name = 'hymba_pool_sgu_mla_moe_adaln'


def _rms_norm(x, gain=None):
    xf = x.astype(jnp.float32)
    y = xf * lax.rsqrt(jnp.mean(xf * xf, axis=-1, keepdims=True) + EPS)
    if gain is not None:
        y = y * gain.astype(jnp.float32)
    return y.astype(x.dtype)


def _ada_norm(x, shift, scale):
    return _rms_norm(x) * (1 + scale) + shift


def _pool_mixer(a, w_pool, pool_scale):
    b, s, _ = a.shape
    af = a.astype(jnp.float32)
    csum = jnp.cumsum(af, axis=1)
    t = jnp.arange(1, s + 1, dtype=jnp.float32)[:, None]
    means = []
    for g, w in enumerate(POOL_WINDOWS):
        cg = csum[..., g * POOL_CH:(g + 1) * POOL_CH]
        lag = jnp.pad(cg, ((0, 0), (w, 0), (0, 0)))[:, :s]
        means.append((cg - lag) / jnp.minimum(t, float(w)))
    d = (jnp.stack(means, axis=2) - af.reshape(b, s, POOL_GROUPS, POOL_CH)).astype(a.dtype)
    y = jnp.einsum('bsgc,gcd->bsgd', d, w_pool).reshape(b, s, POOL_WIDTH)
    return y * pool_scale


def _spatial_gating(u, v, w_spatial, b_spatial):
    b, s, _ = u.shape
    u = jax.nn.gelu(u)
    vf = jax.nn.gelu(v).astype(jnp.float32)
    mu = jnp.mean(vf, axis=-1, keepdims=True)
    var = jnp.mean(jnp.square(vf - mu), axis=-1, keepdims=True)
    vn = ((vf - mu) * lax.rsqrt(var + EPS)).astype(v.dtype)
    vn = vn.reshape(b, s // SG_CHUNK, SG_CHUNK, SG_HEADS, SG_DIM)
    causal = jnp.tril(jnp.ones((SG_CHUNK, SG_CHUNK), w_spatial.dtype))
    mixed = jnp.einsum('htk,bnkhd->bnthd', w_spatial * causal, vn) + b_spatial.T[:, :, None]
    return u * mixed.reshape(b, s, SG_WIDTH)


def _rope(x, cos, sin):
    half = x.shape[-1] // 2
    xf = x.astype(jnp.float32)
    x1, x2 = xf[..., :half], xf[..., half:]
    return jnp.concatenate([x1 * cos - x2 * sin, x2 * cos + x1 * sin], axis=-1).astype(x.dtype)


def _latent_attention(cq, ckv, k_pe, positions, g_q, w_uq, g_kv, w_ukv):
    b, s, _ = cq.shape
    q = (_rms_norm(cq, g_q) @ w_uq).reshape(b, s, MLA_HEADS, QK_NOPE + QK_ROPE)
    kv = (_rms_norm(ckv, g_kv) @ w_ukv).reshape(b, s, MLA_HEADS, QK_NOPE + V_DIM)
    q_nope, q_pe = q[..., :QK_NOPE], q[..., QK_NOPE:]
    k_nope, v = kv[..., :QK_NOPE], kv[..., QK_NOPE:]
    inv_freq = ROPE_THETA ** (-jnp.arange(0, QK_ROPE, 2, dtype=jnp.float32) / QK_ROPE)
    ang = positions.astype(jnp.float32)[..., None] * inv_freq
    cos, sin = jnp.cos(ang), jnp.sin(ang)
    q_pe = _rope(q_pe, cos[:, :, None], sin[:, :, None])
    k_pe = _rope(k_pe, cos, sin)
    outs = []
    for i in range(s // Q_BLOCK):
        q0, k_end = i * Q_BLOCK, (i + 1) * Q_BLOCK
        scores = (jnp.einsum('bqhd,bkhd->bhqk', q_nope[:, q0:k_end], k_nope[:, :k_end])
                  + jnp.einsum('bqhr,bkr->bhqk', q_pe[:, q0:k_end], k_pe[:, :k_end])
                  ).astype(jnp.float32) * ATTN_SCALE
        causal = jnp.arange(k_end)[None, :] <= (q0 + jnp.arange(Q_BLOCK))[:, None]
        probs = jax.nn.softmax(jnp.where(causal, scores, -jnp.inf), axis=-1).astype(v.dtype)
        outs.append(jnp.einsum('bhqk,bkhd->bqhd', probs, v[:, :k_end]))
    return jnp.concatenate(outs, axis=1).reshape(b, s, MLA_WIDTH)


def _token_mixer(h, positions, w_in, w_pool, pool_scale, w_spatial, b_spatial,
                 g_q, w_uq, g_kv, w_ukv, w_out):
    a, u, v, cq, ckv, k_pe = jnp.split(h @ w_in, IN_SPLITS, axis=-1)
    y = jnp.concatenate([
        _pool_mixer(a, w_pool, pool_scale),
        _spatial_gating(u, v, w_spatial, b_spatial),
        _latent_attention(cq, ckv, k_pe, positions, g_q, w_uq, g_kv, w_ukv),
    ], axis=-1)
    return y @ w_out


def _swiglu(x, w_gate, w_up, w_down):
    return (jax.nn.silu(x @ w_gate) * (x @ w_up)) @ w_down


def _route(xf, w_router, router_bias):
    t = xf.shape[0]
    per_group = N_EXPERTS // N_EXPERT_GROUPS
    scores = jax.nn.sigmoid((xf @ w_router).astype(jnp.float32))
    sel = scores + router_bias.astype(jnp.float32)
    group_score = lax.top_k(sel.reshape(t, N_EXPERT_GROUPS, per_group), 2)[0].sum(-1)
    top_groups = lax.top_k(group_score, TOPK_GROUPS)[1]
    group_mask = jax.nn.one_hot(top_groups, N_EXPERT_GROUPS).sum(1) > 0
    sel = jnp.where(jnp.repeat(group_mask, per_group, axis=1), sel, -jnp.inf)
    top_idx = lax.top_k(sel, TOP_K)[1]
    top_w = jnp.take_along_axis(scores, top_idx, axis=-1)
    top_w = top_w / jnp.sum(top_w, axis=-1, keepdims=True) * ROUTED_SCALE
    return top_idx, top_w


def _routed_experts(xf, top_idx, top_w, w_gate, w_up, w_down):
    t, d = xf.shape
    n_assign = t * TOP_K
    flat_e = top_idx.reshape(-1)
    order = jnp.argsort(flat_e)
    e_sorted = flat_e[order]
    tok_sorted = (order // TOP_K).astype(jnp.int32)
    w_sorted = top_w.reshape(-1)[order].astype(xf.dtype)
    counts = jnp.bincount(flat_e, length=N_EXPERTS)
    padded = (counts + EXPERT_BLOCK - 1) // EXPERT_BLOCK * EXPERT_BLOCK
    pad_end = jnp.cumsum(padded)
    pad_start = pad_end - padded
    start = jnp.cumsum(counts) - counts
    dest = pad_start[e_sorted] + jnp.arange(n_assign) - start[e_sorted]
    n_blocks = -(-(n_assign + N_EXPERTS * (EXPERT_BLOCK - 1)) // EXPERT_BLOCK)
    n_rows = n_blocks * EXPERT_BLOCK
    row_tok = jnp.full((n_rows,), t, jnp.int32).at[dest].set(tok_sorted)
    row_w = jnp.zeros((n_rows,), xf.dtype).at[dest].set(w_sorted)
    block_e = jnp.minimum(
        jnp.searchsorted(pad_end, jnp.arange(n_blocks) * EXPERT_BLOCK, side='right'),
        N_EXPERTS - 1)
    x_rows = jnp.concatenate([xf, jnp.zeros((1, d), xf.dtype)], axis=0)[row_tok]
    x_rows = x_rows.reshape(n_blocks, EXPERT_BLOCK, d)

    def expert_block(args):
        xb, e = args
        return _swiglu(xb, w_gate[e], w_up[e], w_down[e])

    y_rows = lax.map(expert_block, (x_rows, block_e)).reshape(n_rows, d)
    return jax.ops.segment_sum(y_rows * row_w[:, None], row_tok, num_segments=t + 1)[:t]


def _moe(h, w_router, router_bias, w_gate, w_up, w_down, ws_gate, ws_up, ws_down):
    b, s, d = h.shape
    xf = h.reshape(b * s, d)
    top_idx, top_w = _route(xf, w_router, router_bias)
    y = _routed_experts(xf, top_idx, top_w, w_gate, w_up, w_down) + _swiglu(xf, ws_gate, ws_up, ws_down)
    return y.reshape(b, s, d)


def setup_inputs(seed: int = 0) -> dict:
    key = jax.random.key(seed)
    ks = jax.random.split(key, 26)
    L = DEPTH
    f32 = jnp.float32

    def nrm(k, shape, fan_in, mult=1.0):
        return jax.random.normal(k, shape, f32) * (mult * fan_in ** -0.5)

    def gain(k, shape):
        return 1.0 + 0.02 * jax.random.normal(k, shape, f32)

    offsets = jax.random.randint(ks[2], (BATCH, 1), 0, MAX_POS_OFFSET, dtype=jnp.int32)
    positions = offsets + jnp.arange(SEQ, dtype=jnp.int32)[None, :]
    return {
        'x': jax.random.normal(ks[0], (BATCH, SEQ, D_MODEL), f32),
        'c': jax.random.normal(ks[1], (BATCH, D_MODEL), f32),
        'positions': positions,
        'w_ada': nrm(ks[3], (L, D_MODEL, 6 * D_MODEL), D_MODEL, 0.5),
        'b_ada': 0.02 * jax.random.normal(ks[4], (L, 6 * D_MODEL), f32),
        'w_in': nrm(ks[5], (L, D_MODEL, N_IN), D_MODEL),
        'w_pool': nrm(ks[6], (L, POOL_GROUPS, POOL_CH, POOL_CH), POOL_CH),
        'pool_scale': gain(ks[7], (L, POOL_WIDTH)),
        'w_spatial': nrm(ks[8], (L, SG_HEADS, SG_CHUNK, SG_CHUNK), SG_CHUNK),
        'b_spatial': gain(ks[9], (L, SG_HEADS, SG_CHUNK)),
        'g_q': gain(ks[10], (L, Q_LORA)),
        'w_uq': nrm(ks[11], (L, Q_LORA, MLA_HEADS * (QK_NOPE + QK_ROPE)), Q_LORA),
        'g_kv': gain(ks[12], (L, KV_LORA)),
        'w_ukv': nrm(ks[13], (L, KV_LORA, MLA_HEADS * (QK_NOPE + V_DIM)), KV_LORA),
        'w_out': nrm(ks[14], (L, D_MIX, D_MODEL), D_MIX),
        'w_router': nrm(ks[15], (L, D_MODEL, N_EXPERTS), D_MODEL),
        'router_bias': 0.01 * jax.random.normal(ks[16], (L, N_EXPERTS), f32),
        'w_gate': nrm(ks[17], (L, N_EXPERTS, D_MODEL, EXPERT_FF), D_MODEL),
        'w_up': nrm(ks[18], (L, N_EXPERTS, D_MODEL, EXPERT_FF), D_MODEL),
        'w_down': nrm(ks[19], (L, N_EXPERTS, EXPERT_FF, D_MODEL), EXPERT_FF),
        'ws_gate': nrm(ks[20], (L, D_MODEL, SHARED_FF), D_MODEL),
        'ws_up': nrm(ks[21], (L, D_MODEL, SHARED_FF), D_MODEL),
        'ws_down': nrm(ks[22], (L, SHARED_FF, D_MODEL), SHARED_FF),
        'final_gain': gain(ks[23], (D_MODEL,)),
    }


def reference(x, c, positions, w_ada, b_ada, w_in, w_pool, pool_scale, w_spatial, b_spatial,
              g_q, w_uq, g_kv, w_ukv, w_out, w_router, router_bias, w_gate, w_up, w_down,
              ws_gate, ws_up, ws_down, final_gain):
    cond = jax.nn.silu(c)
    for l in range(DEPTH):
        mod = (cond @ w_ada[l] + b_ada[l])[:, None, :]
        shift1, scale1, gate1, shift2, scale2, gate2 = jnp.split(mod, 6, axis=-1)
        h = _ada_norm(x, shift1, scale1)
        x = x + gate1 * _token_mixer(h, positions, w_in[l], w_pool[l], pool_scale[l],
                                     w_spatial[l], b_spatial[l], g_q[l], w_uq[l],
                                     g_kv[l], w_ukv[l], w_out[l])
        h = _ada_norm(x, shift2, scale2)
        x = x + gate2 * _moe(h, w_router[l], router_bias[l], w_gate[l], w_up[l], w_down[l],
                             ws_gate[l], ws_up[l], ws_down[l])
    return _rms_norm(x, final_gain)
```

```python
import functools

import jax
import jax.numpy as jnp
from jax import lax
from jax.experimental import pallas as pl
from jax.experimental.pallas import tpu as pltpu

F32 = jnp.float32
BF16 = jnp.bfloat16

EPS = 1e-6
LANES = 128
POOL_WINDOWS = (2, 4, 8, 16)
POOL_WIDTH = 256
POOL_CH = 64
MAX_WINDOW = 16
SG_HEADS = 4
SG_WIDTH = 256
SG_DIM = 64
SG_CHUNK = 128
MLA_HEADS = 8
V_DIM = 64
QK_NOPE = 64
QK_ROPE = 32
HALF_ROPE = QK_ROPE // 2
Q_LORA = 256
KV_LORA = 128
ROPE_THETA = 10000.0
ATTN_SCALE = (QK_NOPE + QK_ROPE) ** -0.5
HEAD_PAD = 128
N_EXPERTS = 64
TOP_K = 8
N_EXPERT_GROUPS = 8
GROUP_SIZE = N_EXPERTS // N_EXPERT_GROUPS
TOPK_GROUPS = 4
ROUTED_SCALE = 2.5
VMEM_LIMIT = 52 * 1024 * 1024


def _dot(a, b):
    return jnp.dot(a, b, preferred_element_type=F32)


def _dot_nt(a, b):
    return lax.dot_general(a, b, (((1,), (1,)), ((), ())), preferred_element_type=F32)


def _rms(x):
    return x * lax.rsqrt(jnp.mean(x * x, axis=-1, keepdims=True) + EPS)


def _gelu_tanh(x):
    c = (2.0 / jnp.pi) ** 0.5
    return x * (0.5 * (1.0 + jnp.tanh(c * (x + 0.044715 * (x * x * x)))))


def _silu(x):
    return x * jax.nn.sigmoid(x)


def _ada_kernel(c_ref, w_ref, b_ref, o_ref):
    cond = _silu(c_ref[...])
    o_ref[0] = _dot(cond.astype(BF16), w_ref[0].astype(BF16)) + b_ref[0]


def _ada_mod(c, w_ada, b_ada):
    depth, d, n = w_ada.shape
    b = c.shape[0]
    nt = 1536
    return pl.pallas_call(
        _ada_kernel,
        grid=(depth, n // nt),
        in_specs=[pl.BlockSpec((b, d), lambda l, j: (0, 0)),
                  pl.BlockSpec((1, d, nt), lambda l, j: (l, 0, j)),
                  pl.BlockSpec((1, 1, nt), lambda l, j: (l, 0, j))],
        out_specs=pl.BlockSpec((1, b, nt), lambda l, j: (l, 0, j)),
        out_shape=jax.ShapeDtypeStruct((depth, b, n), F32),
        compiler_params=pltpu.CompilerParams(vmem_limit_bytes=VMEM_LIMIT),
        name="ada_mod",
    )(c, w_ada, b_ada.reshape(depth, 1, n))


def _rope_kernel(pos_ref, invf_ref, sign_ref, cos_ref, sin_ref):
    ang = pos_ref[...].astype(F32) * invf_ref[...]
    cos_ref[...] = jnp.cos(ang)
    sin_ref[...] = jnp.sin(ang) * sign_ref[...]


def _rope_tables(positions):
    t = positions.size
    tm = 2048
    inv_freq = ROPE_THETA ** (-jnp.arange(0, QK_ROPE, 2, dtype=F32) / QK_ROPE)
    invf = jnp.zeros((1, HEAD_PAD), F32)
    invf = invf.at[0, QK_NOPE:QK_NOPE + HALF_ROPE].set(inv_freq)
    invf = invf.at[0, QK_NOPE + HALF_ROPE:QK_NOPE + QK_ROPE].set(inv_freq)
    sign = jnp.zeros((1, HEAD_PAD), F32)
    sign = sign.at[0, QK_NOPE:QK_NOPE + HALF_ROPE].set(-1.0)
    sign = sign.at[0, QK_NOPE + HALF_ROPE:QK_NOPE + QK_ROPE].set(1.0)
    return pl.pallas_call(
        _rope_kernel,
        grid=(t // tm,),
        in_specs=[pl.BlockSpec((tm, 1), lambda i: (i, 0)),
                  pl.BlockSpec((1, HEAD_PAD), lambda i: (0, 0)),
                  pl.BlockSpec((1, HEAD_PAD), lambda i: (0, 0))],
        out_specs=[pl.BlockSpec((tm, HEAD_PAD), lambda i: (i, 0)),
                   pl.BlockSpec((tm, HEAD_PAD), lambda i: (i, 0))],
        out_shape=[jax.ShapeDtypeStruct((t, HEAD_PAD), F32)] * 2,
        name="rope_tables",
    )(positions.reshape(t, 1), invf, sign)


def _mixer_in_kernel(x_ref, shift_ref, scale_ref, win_ref, wpool_ref, pscale_ref, wsp_ref, bsp_ref,
                     gq_ref, wuq_ref, wuqs_ref, gkv_ref, wk_ref, wv_ref, cos_ref, sin_ref,
                     mix_ref, q_ref, k_ref, v_ref, carry_ref, *, tiles_per_seq):
    tm = x_ref.shape[0]
    ti = pl.program_id(0) % tiles_per_seq
    h = _rms(x_ref[...]) * (1.0 + scale_ref[0]) + shift_ref[0]
    z = _dot(h.astype(BF16), win_ref[...])

    a = z[:, 0:POOL_WIDTH]

    @pl.when(ti == 0)
    def _():
        carry_ref[...] = jnp.zeros_like(carry_ref)

    ext = jnp.concatenate([carry_ref[...], a], axis=0)
    carry_ref[...] = a[tm - MAX_WINDOW:, :]
    p1 = ext + pltpu.roll(ext, 1, 0)
    p2 = p1 + pltpu.roll(p1, 2, 0)
    p3 = p2 + pltpu.roll(p2, 4, 0)
    p4 = p3 + pltpu.roll(p3, 8, 0)
    lane = lax.broadcasted_iota(jnp.int32, (tm, POOL_WIDTH), 1)
    row = lax.broadcasted_iota(jnp.int32, (tm, POOL_WIDTH), 0) + (ti * tm + 1)
    g0, g1, g2 = lane < POOL_CH, lane < 2 * POOL_CH, lane < 3 * POOL_CH
    wsum = jnp.where(g0, p1[MAX_WINDOW:], jnp.where(g1, p2[MAX_WINDOW:],
                     jnp.where(g2, p3[MAX_WINDOW:], p4[MAX_WINDOW:])))
    width = jnp.where(g0, POOL_WINDOWS[0], jnp.where(g1, POOL_WINDOWS[1],
                      jnp.where(g2, POOL_WINDOWS[2], POOL_WINDOWS[3])))
    cnt = jnp.minimum(row, width).astype(F32)
    dlt = wsum / cnt - a
    y_pool = _dot(dlt.astype(BF16), wpool_ref[...]) * pscale_ref[...]
    mix_ref[:, 0:POOL_WIDTH] = y_pool.astype(mix_ref.dtype)

    ug = _gelu_tanh(z[:, POOL_WIDTH:POOL_WIDTH + SG_WIDTH])
    vg = _gelu_tanh(z[:, POOL_WIDTH + SG_WIDTH:POOL_WIDTH + 2 * SG_WIDTH])
    mu = jnp.mean(vg, axis=-1, keepdims=True)
    vc = vg - mu
    var = jnp.mean(vc * vc, axis=-1, keepdims=True)
    vn = (vc * lax.rsqrt(var + EPS)).astype(BF16)
    r_i = lax.broadcasted_iota(jnp.int32, (SG_CHUNK, SG_CHUNK), 0)
    c_i = lax.broadcasted_iota(jnp.int32, (SG_CHUNK, SG_CHUNK), 1)
    wms = [jnp.where(c_i <= r_i, wsp_ref[hh], 0.0).astype(BF16) for hh in range(SG_HEADS)]
    lane_head = lax.broadcasted_iota(jnp.int32, (SG_CHUNK, SG_WIDTH), 1) // SG_DIM
    for cidx in range(tm // SG_CHUNK):
        rows = slice(cidx * SG_CHUNK, (cidx + 1) * SG_CHUNK)
        vchunk = vn[rows]
        mixed = bsp_ref[...]
        for hh in range(SG_HEADS):
            mixed = mixed + jnp.where(lane_head == hh, _dot(wms[hh], vchunk), 0.0)
        mix_ref[rows, POOL_WIDTH:POOL_WIDTH + SG_WIDTH] = (ug[rows] * mixed).astype(mix_ref.dtype)

    o_cq = POOL_WIDTH + 2 * SG_WIDTH
    o_ckv = o_cq + Q_LORA
    o_kpe = o_ckv + KV_LORA
    cos = cos_ref[...]
    sin = sin_ref[...]
    cqn = (_rms(z[:, o_cq:o_ckv]) * gq_ref[...]).astype(BF16)
    q = _dot(cqn, wuq_ref[...])
    qs = _dot(cqn, wuqs_ref[...])
    ckvn = (_rms(z[:, o_ckv:o_kpe]) * gkv_ref[...]).astype(BF16)
    kn = _dot(ckvn, wk_ref[...])
    kpe = z[:, o_kpe:o_kpe + HEAD_PAD] * cos + z[:, o_kpe + HEAD_PAD:o_kpe + 2 * HEAD_PAD] * sin
    for hh in range(MLA_HEADS):
        blk = slice(hh * HEAD_PAD, (hh + 1) * HEAD_PAD)
        q_ref[:, blk] = (q[:, blk] * cos + qs[:, blk] * sin).astype(q_ref.dtype)
        k_ref[:, blk] = (kn[:, blk] + kpe).astype(k_ref.dtype)
    v_ref[...] = _dot(ckvn, wv_ref[...]).astype(v_ref.dtype)


def _mixer_in(x2d, shift, scale, cos_t, sin_t, p, seq):
    t, d = x2d.shape
    tm = 512
    tiles_per_seq = seq // tm
    nz = p["w_in"].shape[1]
    hq = MLA_HEADS * HEAD_PAD

    def full(shape):
        return pl.BlockSpec(shape, lambda i: (0,) * len(shape))

    def mod():
        return pl.BlockSpec((1, 1, d), lambda i: (i // tiles_per_seq, 0, 0))

    return pl.pallas_call(
        functools.partial(_mixer_in_kernel, tiles_per_seq=tiles_per_seq),
        grid=(t // tm,),
        in_specs=[pl.BlockSpec((tm, d), lambda i: (i, 0)), mod(), mod(),
                  full((d, nz)), full((POOL_WIDTH, POOL_WIDTH)), full((1, POOL_WIDTH)),
                  full((SG_HEADS, SG_CHUNK, SG_CHUNK)), full((SG_CHUNK, SG_WIDTH)),
                  full((1, Q_LORA)), full((Q_LORA, hq)), full((Q_LORA, hq)),
                  full((1, KV_LORA)), full((KV_LORA, hq)), full((KV_LORA, MLA_HEADS * V_DIM)),
                  pl.BlockSpec((tm, HEAD_PAD), lambda i: (i, 0)),
                  pl.BlockSpec((tm, HEAD_PAD), lambda i: (i, 0))],
        out_specs=[pl.BlockSpec((tm, POOL_WIDTH + SG_WIDTH), lambda i: (i, 0)),
                   pl.BlockSpec((tm, hq), lambda i: (i, 0)),
                   pl.BlockSpec((tm, hq), lambda i: (i, 0)),
                   pl.BlockSpec((tm, MLA_HEADS * V_DIM), lambda i: (i, 0))],
        out_shape=[jax.ShapeDtypeStruct((t, POOL_WIDTH + SG_WIDTH), BF16),
                   jax.ShapeDtypeStruct((t, hq), BF16),
                   jax.ShapeDtypeStruct((t, hq), BF16),
                   jax.ShapeDtypeStruct((t, MLA_HEADS * V_DIM), BF16)],
        scratch_shapes=[pltpu.VMEM((MAX_WINDOW, POOL_WIDTH), F32)],
        compiler_params=pltpu.CompilerParams(dimension_semantics=("arbitrary",),
                                             vmem_limit_bytes=VMEM_LIMIT),
        name="mixer_in",
    )(x2d, shift, scale, p["w_in"], p["w_pool"], p["pool_scale"], p["w_spatial"], p["b_spatial"],
      p["g_q"], p["w_uq"], p["w_uq_sw"], p["g_kv"], p["w_k"], p["w_v"], cos_t, sin_t)


def _attn_kernel(q_ref, k_ref, v_ref, o_ref, m_ref, l_ref, acc_ref):
    tq = q_ref.shape[0]
    i = pl.program_id(2)
    r_i = lax.broadcasted_iota(jnp.int32, (tq, tq), 0)
    c_i = lax.broadcasted_iota(jnp.int32, (tq, tq), 1)
    outs = []
    for hh in range(2):
        blk = slice(hh * HEAD_PAD, (hh + 1) * HEAD_PAD)
        q = q_ref[:, blk]

        kd = k_ref[pl.ds(i * tq, tq), blk]
        s = jnp.where(c_i <= r_i, _dot_nt(q, kd) * ATTN_SCALE, -jnp.inf)
        m0 = jnp.max(s, axis=-1, keepdims=True)
        p0 = jnp.exp(s - m0)
        m_ref[...] = m0
        l_ref[...] = jnp.sum(p0, axis=-1, keepdims=True)
        acc_ref[...] = _dot(p0.astype(BF16), v_ref[pl.ds(i * tq, tq), :])

        def body(j, carry):
            kj = k_ref[pl.ds(j * tq, tq), blk]
            sj = _dot_nt(q, kj) * ATTN_SCALE
            m_old = m_ref[...]
            m_new = jnp.maximum(m_old, jnp.max(sj, axis=-1, keepdims=True))
            alpha = jnp.exp(m_old - m_new)
            pj = jnp.exp(sj - m_new)
            l_ref[...] = alpha * l_ref[...] + jnp.sum(pj, axis=-1, keepdims=True)
            acc_ref[...] = alpha * acc_ref[...] + _dot(pj.astype(BF16), v_ref[pl.ds(j * tq, tq), :])
            m_ref[...] = m_new
            return carry

        lax.fori_loop(0, i, body, 0)
        outs.append(acc_ref[...] / l_ref[...])
    lane = lax.broadcasted_iota(jnp.int32, (tq, 2 * V_DIM), 1)
    o_ref[...] = jnp.where(lane < V_DIM, outs[0], outs[1]).astype(o_ref.dtype)


def _attention(q, k, v, batch, seq):
    t = q.shape[0]
    tq = 256
    nq = seq // tq
    return pl.pallas_call(
        _attn_kernel,
        grid=(batch, MLA_HEADS // 2, nq),
        in_specs=[pl.BlockSpec((tq, 2 * HEAD_PAD), lambda b, hp, i: (b * nq + i, hp)),
                  pl.BlockSpec((seq, 2 * HEAD_PAD), lambda b, hp, i: (b, hp)),
                  pl.BlockSpec((seq, 2 * V_DIM), lambda b, hp, i: (b, hp))],
        out_specs=pl.BlockSpec((tq, 2 * V_DIM), lambda b, hp, i: (b * nq + i, hp)),
        out_shape=jax.ShapeDtypeStruct((t, MLA_HEADS * V_DIM), BF16),
        scratch_shapes=[pltpu.VMEM((tq, 1), F32), pltpu.VMEM((tq, 1), F32),
                        pltpu.VMEM((tq, 2 * V_DIM), F32)],
        compiler_params=pltpu.CompilerParams(
            dimension_semantics=("arbitrary", "arbitrary", "arbitrary"),
            vmem_limit_bytes=VMEM_LIMIT),
        name="attention",
    )(q, k, v)


def _first_max_index(cur, idx, sentinel):
    m = jnp.max(cur, axis=0, keepdims=True)
    first = jnp.min(jnp.where(cur == m, idx, sentinel), axis=0, keepdims=True)
    return m, first


def _route(logits_t, bias_t):
    n_tok = logits_t.shape[1]
    scores = jax.nn.sigmoid(logits_t)
    sel = scores + bias_t
    neg = -jnp.inf
    sub = lax.broadcasted_iota(jnp.int32, (GROUP_SIZE, n_tok), 0).astype(F32)
    gid = lax.broadcasted_iota(jnp.int32, (N_EXPERT_GROUPS, n_tok), 0).astype(F32)
    gscore = jnp.zeros((N_EXPERT_GROUPS, n_tok), F32)
    for g in range(N_EXPERT_GROUPS):
        s = sel[g * GROUP_SIZE:(g + 1) * GROUP_SIZE]
        m1, i1 = _first_max_index(s, sub, float(GROUP_SIZE))
        m2 = jnp.max(jnp.where(sub == i1, neg, s), axis=0, keepdims=True)
        gscore = jnp.where(gid == float(g), m1 + m2, gscore)
    eid = lax.broadcasted_iota(jnp.int32, (N_EXPERTS, n_tok), 0).astype(F32)
    egroup = jnp.floor(eid * (1.0 / GROUP_SIZE))
    allowed = jnp.zeros((N_EXPERTS, n_tok), F32)
    cur = gscore
    for _ in range(TOPK_GROUPS):
        _, gi = _first_max_index(cur, gid, float(N_EXPERT_GROUPS))
        cur = jnp.where(gid == gi, neg, cur)
        allowed = jnp.where(egroup == gi, 1.0, allowed)
    cur = jnp.where(allowed > 0.0, sel, neg)
    chosen = jnp.zeros((N_EXPERTS, n_tok), F32)
    for _ in range(TOP_K):
        _, ei = _first_max_index(cur, eid, float(N_EXPERTS))
        hit = eid == ei
        cur = jnp.where(hit, neg, cur)
        chosen = jnp.where(hit, 1.0, chosen)
    w = jnp.where(chosen > 0.0, scores, 0.0)
    return w / jnp.sum(w, axis=0, keepdims=True) * ROUTED_SCALE


def _split_bf16(x):
    hi = x.astype(BF16)
    lo = (x - hi.astype(F32)).astype(BF16)
    return hi, lo


def _mixer_out_kernel(mix_ref, att_ref, x_ref, gate_ref, shift_ref, scale_ref, wo_ref, wr_ref, rb_ref,
                      x2_ref, h2_ref, cw_ref):
    half = mix_ref.shape[1]
    y = _dot(mix_ref[...], wo_ref[0:half, :]) + _dot(att_ref[...], wo_ref[half:, :])
    x2 = x_ref[...] + gate_ref[0] * y
    x2_ref[...] = x2
    h2 = _rms(x2) * (1.0 + scale_ref[0]) + shift_ref[0]
    h2_ref[...] = h2.astype(h2_ref.dtype)
    h_hi, h_lo = _split_bf16(h2)
    w_hi, w_lo = _split_bf16(wr_ref[...])
    logits_t = _dot_nt(w_hi, h_hi) + (_dot_nt(w_hi, h_lo) + _dot_nt(w_lo, h_hi))
    cw_t = _route(logits_t, rb_ref[...])
    pad = jnp.zeros((LANES - N_EXPERTS, cw_t.shape[1]), F32)
    cw_ref[...] = jnp.concatenate([cw_t, pad], axis=0).T


def _mixer_out(mix, att, x2d, gate, shift, scale, p, seq):
    t, d = x2d.shape
    tm = 512
    tiles_per_seq = seq // tm

    def full(shape):
        return pl.BlockSpec(shape, lambda i: (0,) * len(shape))

    def mod():
        return pl.BlockSpec((1, 1, d), lambda i: (i // tiles_per_seq, 0, 0))

    return pl.pallas_call(
        _mixer_out_kernel,
        grid=(t // tm,),
        in_specs=[pl.BlockSpec((tm, mix.shape[1]), lambda i: (i, 0)),
                  pl.BlockSpec((tm, att.shape[1]), lambda i: (i, 0)),
                  pl.BlockSpec((tm, d), lambda i: (i, 0)), mod(), mod(), mod(),
                  full((d, d)), full((N_EXPERTS, d)), full((N_EXPERTS, 1))],
        out_specs=[pl.BlockSpec((tm, d), lambda i: (i, 0)),
                   pl.BlockSpec((tm, d), lambda i: (i, 0)),
                   pl.BlockSpec((tm, LANES), lambda i: (i, 0))],
        out_shape=[jax.ShapeDtypeStruct((t, d), F32),
                   jax.ShapeDtypeStruct((t, d), BF16),
                   jax.ShapeDtypeStruct((t, LANES), F32)],
        compiler_params=pltpu.CompilerParams(dimension_semantics=("arbitrary",),
                                             vmem_limit_bytes=VMEM_LIMIT),
        name="mixer_out",
    )(mix, att, x2d, gate, shift, scale, p["w_out"], p["w_router_t"], p["router_bias"])


def _moe_kernel(h_ref, cw_ref, x2_ref, gate_ref, wg_ref, wu_ref, wd_ref, sg_ref, su_ref, sd_ref, fg_ref,
                o_ref, acc_ref, *, final_norm):
    e = pl.program_id(1)
    h = h_ref[...]

    @pl.when(e == 0)
    def _():
        act = _silu(_dot(h, sg_ref[...])) * _dot(h, su_ref[...])
        acc_ref[...] = _dot(act.astype(BF16), sd_ref[...])

    act = _silu(_dot(h, wg_ref[0, 0].astype(BF16))) * _dot(h, wu_ref[0, 0].astype(BF16))
    y = _dot(act.astype(BF16), wd_ref[0, 0].astype(BF16))
    lane = lax.broadcasted_iota(jnp.int32, cw_ref.shape, 1)
    col = jnp.sum(jnp.where(lane == e, cw_ref[...], 0.0), axis=-1, keepdims=True)
    acc_ref[...] += col * y

    @pl.when(e == pl.num_programs(1) - 1)
    def _():
        x3 = x2_ref[...] + gate_ref[0] * acc_ref[...]
        if final_norm:
            x3 = _rms(x3) * fg_ref[...]
        o_ref[...] = x3


def _moe(h2, cw, x2, gate, p, w_gate, w_up, w_down, layer, final_gain, seq, final_norm):
    t, d = x2.shape
    tm = 1024
    tiles_per_seq = seq // tm
    ff = w_gate.shape[3]
    return pl.pallas_call(
        functools.partial(_moe_kernel, final_norm=final_norm),
        grid=(t // tm, N_EXPERTS),
        in_specs=[pl.BlockSpec((tm, d), lambda i, e: (i, 0)),
                  pl.BlockSpec((tm, LANES), lambda i, e: (i, 0)),
                  pl.BlockSpec((tm, d), lambda i, e: (i, 0)),
                  pl.BlockSpec((1, 1, d), lambda i, e: (i // tiles_per_seq, 0, 0)),
                  pl.BlockSpec((1, 1, d, ff), lambda i, e: (layer, e, 0, 0)),
                  pl.BlockSpec((1, 1, d, ff), lambda i, e: (layer, e, 0, 0)),
                  pl.BlockSpec((1, 1, ff, d), lambda i, e: (layer, e, 0, 0)),
                  pl.BlockSpec((d, ff), lambda i, e: (0, 0)),
                  pl.BlockSpec((d, ff), lambda i, e: (0, 0)),
                  pl.BlockSpec((ff, d), lambda i, e: (0, 0)),
                  pl.BlockSpec((1, d), lambda i, e: (0, 0))],
        out_specs=pl.BlockSpec((tm, d), lambda i, e: (i, 0)),
        out_shape=jax.ShapeDtypeStruct((t, d), F32),
        scratch_shapes=[pltpu.VMEM((tm, d), F32)],
        compiler_params=pltpu.CompilerParams(dimension_semantics=("arbitrary", "arbitrary"),
                                             vmem_limit_bytes=VMEM_LIMIT),
        name="moe",
    )(h2, cw, x2, gate, w_gate, w_up, w_down, p["ws_gate"], p["ws_up"], p["ws_down"], final_gain)


def _prep_layer(w_in, w_pool, pool_scale, w_spatial, b_spatial, g_q, w_uq, g_kv, w_ukv, w_out,
                w_router, router_bias, ws_gate, ws_up, ws_down):
    d = w_in.shape[0]
    o_kpe = POOL_WIDTH + 2 * SG_WIDTH + Q_LORA + KV_LORA
    x1 = w_in[:, o_kpe:o_kpe + HALF_ROPE]
    x2 = w_in[:, o_kpe + HALF_ROPE:o_kpe + QK_ROPE]
    zl = jnp.zeros((d, QK_NOPE), F32)
    zr = jnp.zeros((d, HEAD_PAD - QK_NOPE - QK_ROPE), F32)
    w_in_pad = jnp.concatenate([w_in[:, :o_kpe], zl, x1, x2, zr, zl, x2, x1, zr], axis=1)

    uq = w_uq.reshape(Q_LORA, MLA_HEADS, QK_NOPE + QK_ROPE)
    q1 = uq[..., QK_NOPE:QK_NOPE + HALF_ROPE]
    q2 = uq[..., QK_NOPE + HALF_ROPE:]
    zq = jnp.zeros((Q_LORA, MLA_HEADS, HEAD_PAD - QK_NOPE - QK_ROPE), F32)
    w_uq_pad = jnp.concatenate([uq, zq], axis=-1).reshape(Q_LORA, MLA_HEADS * HEAD_PAD)
    w_uq_sw = jnp.concatenate([jnp.zeros_like(uq[..., :QK_NOPE]), q2, q1, zq], axis=-1)
    w_uq_sw = w_uq_sw.reshape(Q_LORA, MLA_HEADS * HEAD_PAD)

    ukv = w_ukv.reshape(KV_LORA, MLA_HEADS, QK_NOPE + V_DIM)
    zk = jnp.zeros((KV_LORA, MLA_HEADS, HEAD_PAD - QK_NOPE), F32)
    w_k = jnp.concatenate([ukv[..., :QK_NOPE], zk], axis=-1).reshape(KV_LORA, MLA_HEADS * HEAD_PAD)
    w_v = ukv[..., QK_NOPE:].reshape(KV_LORA, MLA_HEADS * V_DIM)

    w_pool_bd = jax.scipy.linalg.block_diag(*[w_pool[g] for g in range(len(POOL_WINDOWS))])
    b_sp = jnp.repeat(b_spatial.T, SG_DIM, axis=1)
    return {
        "w_in": w_in_pad.astype(BF16), "w_pool": w_pool_bd.astype(BF16),
        "pool_scale": pool_scale.reshape(1, -1), "w_spatial": w_spatial, "b_spatial": b_sp,
        "g_q": g_q.reshape(1, -1), "w_uq": w_uq_pad.astype(BF16), "w_uq_sw": w_uq_sw.astype(BF16),
        "g_kv": g_kv.reshape(1, -1), "w_k": w_k.astype(BF16), "w_v": w_v.astype(BF16),
        "w_out": w_out.astype(BF16), "w_router_t": w_router.T, "router_bias": router_bias.reshape(-1, 1),
        "ws_gate": ws_gate.astype(BF16), "ws_up": ws_up.astype(BF16), "ws_down": ws_down.astype(BF16),
    }


def kernel(x, c, positions, w_ada, b_ada, w_in, w_pool, pool_scale, w_spatial, b_spatial, g_q, w_uq, g_kv, w_ukv, w_out, w_router, router_bias, w_gate, w_up, w_down, ws_gate, ws_up, ws_down, final_gain):
    batch, seq, d = x.shape
    depth = w_ada.shape[0]
    mod = _ada_mod(c, w_ada, b_ada)
    cos_t, sin_t = _rope_tables(positions)
    xt = x.reshape(batch * seq, d)
    fg = final_gain.reshape(1, d)
    for l in range(depth):
        p = _prep_layer(w_in[l], w_pool[l], pool_scale[l], w_spatial[l], b_spatial[l], g_q[l], w_uq[l],
                        g_kv[l], w_ukv[l], w_out[l], w_router[l], router_bias[l], ws_gate[l], ws_up[l],
                        ws_down[l])
        shift1, scale1, gate1, shift2, scale2, gate2 = [
            mod[l, :, k * d:(k + 1) * d].reshape(batch, 1, d) for k in range(6)]
        mix, q, k, v = _mixer_in(xt, shift1, scale1, cos_t, sin_t, p, seq)
        att = _attention(q, k, v, batch, seq)
        x2, h2, cw = _mixer_out(mix, att, xt, gate1, shift2, scale2, p, seq)
        xt = _moe(h2, cw, x2, gate2, p, w_gate, w_up, w_down, l, fg, seq, final_norm=(l == depth - 1))
    return xt.reshape(batch, seq, d)
```

```python
import functools

import jax
import jax.numpy as jnp
from jax import lax
from jax.experimental import pallas as pl
from jax.experimental.pallas import tpu as pltpu

F32 = jnp.float32
BF16 = jnp.bfloat16

EPS = 1e-6
LANES = 128
POOL_WINDOWS = (2, 4, 8, 16)
POOL_WIDTH = 256
POOL_CH = 64
MAX_WINDOW = 16
SG_HEADS = 4
SG_WIDTH = 256
SG_DIM = 64
SG_CHUNK = 128
MLA_HEADS = 8
V_DIM = 64
QK_NOPE = 64
QK_ROPE = 32
HALF_ROPE = QK_ROPE // 2
Q_LORA = 256
KV_LORA = 128
ROPE_THETA = 10000.0
ATTN_SCALE = (QK_NOPE + QK_ROPE) ** -0.5
HEAD_PAD = 128
N_EXPERTS = 64
TOP_K = 8
N_EXPERT_GROUPS = 8
GROUP_SIZE = N_EXPERTS // N_EXPERT_GROUPS
TOPK_GROUPS = 4
ROUTED_SCALE = 2.5
VMEM_LIMIT = 52 * 1024 * 1024


def _dot(a, b):
    return jnp.dot(a, b, preferred_element_type=F32)


def _dot_nt(a, b):
    return lax.dot_general(a, b, (((1,), (1,)), ((), ())), preferred_element_type=F32)


def _rms(x):
    return x * lax.rsqrt(jnp.mean(x * x, axis=-1, keepdims=True) + EPS)


def _gelu_tanh(x):
    c = (2.0 / jnp.pi) ** 0.5
    return x * (0.5 * (1.0 + jnp.tanh(c * (x + 0.044715 * (x * x * x)))))


def _silu(x):
    return x * jax.nn.sigmoid(x)


def _ada_kernel(c_ref, w_ref, b_ref, o_ref):
    cond = _silu(c_ref[...])
    o_ref[0] = _dot(cond.astype(BF16), w_ref[0].astype(BF16)) + b_ref[0]


def _ada_mod(c, w_ada, b_ada):
    depth, d, n = w_ada.shape
    b = c.shape[0]
    nt = 1536
    return pl.pallas_call(
        _ada_kernel,
        grid=(depth, n // nt),
        in_specs=[pl.BlockSpec((b, d), lambda l, j: (0, 0)),
                  pl.BlockSpec((1, d, nt), lambda l, j: (l, 0, j)),
                  pl.BlockSpec((1, 1, nt), lambda l, j: (l, 0, j))],
        out_specs=pl.BlockSpec((1, b, nt), lambda l, j: (l, 0, j)),
        out_shape=jax.ShapeDtypeStruct((depth, b, n), F32),
        compiler_params=pltpu.CompilerParams(vmem_limit_bytes=VMEM_LIMIT),
        name="ada_mod",
    )(c, w_ada, b_ada.reshape(depth, 1, n))


def _rope_kernel(pos_ref, invf_ref, sign_ref, cos_ref, sin_ref):
    ang = pos_ref[...].astype(F32) * invf_ref[...]
    cos_ref[...] = jnp.cos(ang)
    sin_ref[...] = jnp.sin(ang) * sign_ref[...]


def _rope_tables(positions):
    t = positions.size
    tm = 2048
    inv_freq = ROPE_THETA ** (-jnp.arange(0, QK_ROPE, 2, dtype=F32) / QK_ROPE)
    invf = jnp.zeros((1, HEAD_PAD), F32)
    invf = invf.at[0, QK_NOPE:QK_NOPE + HALF_ROPE].set(inv_freq)
    invf = invf.at[0, QK_NOPE + HALF_ROPE:QK_NOPE + QK_ROPE].set(inv_freq)
    sign = jnp.zeros((1, HEAD_PAD), F32)
    sign = sign.at[0, QK_NOPE:QK_NOPE + HALF_ROPE].set(-1.0)
    sign = sign.at[0, QK_NOPE + HALF_ROPE:QK_NOPE + QK_ROPE].set(1.0)
    return pl.pallas_call(
        _rope_kernel,
        grid=(t // tm,),
        in_specs=[pl.BlockSpec((tm, 1), lambda i: (i, 0)),
                  pl.BlockSpec((1, HEAD_PAD), lambda i: (0, 0)),
                  pl.BlockSpec((1, HEAD_PAD), lambda i: (0, 0))],
        out_specs=[pl.BlockSpec((tm, HEAD_PAD), lambda i: (i, 0)),
                   pl.BlockSpec((tm, HEAD_PAD), lambda i: (i, 0))],
        out_shape=[jax.ShapeDtypeStruct((t, HEAD_PAD), F32)] * 2,
        name="rope_tables",
    )(positions.reshape(t, 1), invf, sign)


def _mixer_in_kernel(x_ref, shift_ref, scale_ref, win_ref, wpool_ref, pscale_ref, wsp_ref, bsp_ref,
                     gq_ref, wuq_ref, wuqs_ref, gkv_ref, wk_ref, wv_ref, cos_ref, sin_ref,
                     mix_ref, q_ref, k_ref, v_ref, carry_ref, *, tiles_per_seq):
    tm = x_ref.shape[0]
    ti = pl.program_id(0) % tiles_per_seq
    h = _rms(x_ref[...]) * (1.0 + scale_ref[0]) + shift_ref[0]
    z = _dot(h.astype(BF16), win_ref[...])

    a = z[:, 0:POOL_WIDTH]

    @pl.when(ti == 0)
    def _():
        carry_ref[...] = jnp.zeros_like(carry_ref)

    ext = jnp.concatenate([carry_ref[...], a], axis=0)
    carry_ref[...] = a[tm - MAX_WINDOW:, :]
    p1 = ext + pltpu.roll(ext, 1, 0)
    p2 = p1 + pltpu.roll(p1, 2, 0)
    p3 = p2 + pltpu.roll(p2, 4, 0)
    p4 = p3 + pltpu.roll(p3, 8, 0)
    lane = lax.broadcasted_iota(jnp.int32, (tm, POOL_WIDTH), 1)
    row = lax.broadcasted_iota(jnp.int32, (tm, POOL_WIDTH), 0) + (ti * tm + 1)
    g0, g1, g2 = lane < POOL_CH, lane < 2 * POOL_CH, lane < 3 * POOL_CH
    wsum = jnp.where(g0, p1[MAX_WINDOW:], jnp.where(g1, p2[MAX_WINDOW:],
                     jnp.where(g2, p3[MAX_WINDOW:], p4[MAX_WINDOW:])))
    width = jnp.where(g0, POOL_WINDOWS[0], jnp.where(g1, POOL_WINDOWS[1],
                      jnp.where(g2, POOL_WINDOWS[2], POOL_WINDOWS[3])))
    cnt = jnp.minimum(row, width).astype(F32)
    dlt = wsum / cnt - a
    y_pool = _dot(dlt.astype(BF16), wpool_ref[...]) * pscale_ref[...]
    mix_ref[:, 0:POOL_WIDTH] = y_pool.astype(mix_ref.dtype)

    ug = _gelu_tanh(z[:, POOL_WIDTH:POOL_WIDTH + SG_WIDTH])
    vg = _gelu_tanh(z[:, POOL_WIDTH + SG_WIDTH:POOL_WIDTH + 2 * SG_WIDTH])
    mu = jnp.mean(vg, axis=-1, keepdims=True)
    vc = vg - mu
    var = jnp.mean(vc * vc, axis=-1, keepdims=True)
    vn = (vc * lax.rsqrt(var + EPS)).astype(BF16)
    r_i = lax.broadcasted_iota(jnp.int32, (SG_CHUNK, SG_CHUNK), 0)
    c_i = lax.broadcasted_iota(jnp.int32, (SG_CHUNK, SG_CHUNK), 1)
    wms = [jnp.where(c_i <= r_i, wsp_ref[hh], 0.0).astype(BF16) for hh in range(SG_HEADS)]
    lane_head = lax.broadcasted_iota(jnp.int32, (SG_CHUNK, SG_WIDTH), 1) // SG_DIM
    for cidx in range(tm // SG_CHUNK):
        rows = slice(cidx * SG_CHUNK, (cidx + 1) * SG_CHUNK)
        vchunk = vn[rows]
        mixed = bsp_ref[...]
        for hh in range(SG_HEADS):
            mixed = mixed + jnp.where(lane_head == hh, _dot(wms[hh], vchunk), 0.0)
        mix_ref[rows, POOL_WIDTH:POOL_WIDTH + SG_WIDTH] = (ug[rows] * mixed).astype(mix_ref.dtype)

    o_cq = POOL_WIDTH + 2 * SG_WIDTH
    o_ckv = o_cq + Q_LORA
    o_kpe = o_ckv + KV_LORA
    cos = cos_ref[...]
    sin = sin_ref[...]
    cqn = (_rms(z[:, o_cq:o_ckv]) * gq_ref[...]).astype(BF16)
    q = _dot(cqn, wuq_ref[...])
    qs = _dot(cqn, wuqs_ref[...])
    ckvn = (_rms(z[:, o_ckv:o_kpe]) * gkv_ref[...]).astype(BF16)
    kn = _dot(ckvn, wk_ref[...])
    kpe = z[:, o_kpe:o_kpe + HEAD_PAD] * cos + z[:, o_kpe + HEAD_PAD:o_kpe + 2 * HEAD_PAD] * sin
    for hh in range(MLA_HEADS):
        blk = slice(hh * HEAD_PAD, (hh + 1) * HEAD_PAD)
        q_ref[:, blk] = ((q[:, blk] * cos + qs[:, blk] * sin) * ATTN_SCALE).astype(q_ref.dtype)
        k_ref[:, blk] = (kn[:, blk] + kpe).astype(k_ref.dtype)
    v_ref[...] = _dot(ckvn, wv_ref[...]).astype(v_ref.dtype)


def _mixer_in(x2d, shift, scale, cos_t, sin_t, p, seq):
    t, d = x2d.shape
    tm = 512
    tiles_per_seq = seq // tm
    nz = p["w_in"].shape[1]
    hq = MLA_HEADS * HEAD_PAD

    def full(shape):
        return pl.BlockSpec(shape, lambda i: (0,) * len(shape))

    def mod():
        return pl.BlockSpec((1, 1, d), lambda i: (i // tiles_per_seq, 0, 0))

    return pl.pallas_call(
        functools.partial(_mixer_in_kernel, tiles_per_seq=tiles_per_seq),
        grid=(t // tm,),
        in_specs=[pl.BlockSpec((tm, d), lambda i: (i, 0)), mod(), mod(),
                  full((d, nz)), full((POOL_WIDTH, POOL_WIDTH)), full((1, POOL_WIDTH)),
                  full((SG_HEADS, SG_CHUNK, SG_CHUNK)), full((SG_CHUNK, SG_WIDTH)),
                  full((1, Q_LORA)), full((Q_LORA, hq)), full((Q_LORA, hq)),
                  full((1, KV_LORA)), full((KV_LORA, hq)), full((KV_LORA, MLA_HEADS * V_DIM)),
                  pl.BlockSpec((tm, HEAD_PAD), lambda i: (i, 0)),
                  pl.BlockSpec((tm, HEAD_PAD), lambda i: (i, 0))],
        out_specs=[pl.BlockSpec((tm, POOL_WIDTH + SG_WIDTH), lambda i: (i, 0)),
                   pl.BlockSpec((tm, hq), lambda i: (i, 0)),
                   pl.BlockSpec((tm, hq), lambda i: (i, 0)),
                   pl.BlockSpec((tm, MLA_HEADS * V_DIM), lambda i: (i, 0))],
        out_shape=[jax.ShapeDtypeStruct((t, POOL_WIDTH + SG_WIDTH), BF16),
                   jax.ShapeDtypeStruct((t, hq), BF16),
                   jax.ShapeDtypeStruct((t, hq), BF16),
                   jax.ShapeDtypeStruct((t, MLA_HEADS * V_DIM), BF16)],
        scratch_shapes=[pltpu.VMEM((MAX_WINDOW, POOL_WIDTH), F32)],
        compiler_params=pltpu.CompilerParams(dimension_semantics=("arbitrary",),
                                             vmem_limit_bytes=VMEM_LIMIT),
        name="mixer_in",
    )(x2d, shift, scale, p["w_in"], p["w_pool"], p["pool_scale"], p["w_spatial"], p["b_spatial"],
      p["g_q"], p["w_uq"], p["w_uq_sw"], p["g_kv"], p["w_k"], p["w_v"], cos_t, sin_t)


ATTN_TQ = 256


def _attn_kernel(q_ref, k_ref, v_ref, o_ref):
    seq = q_ref.shape[0]
    tq = ATTN_TQ
    r_i = lax.broadcasted_iota(jnp.int32, (tq, tq), 0)
    c_i = lax.broadcasted_iota(jnp.int32, (tq, tq), 1)
    lane = lax.broadcasted_iota(jnp.int32, (tq, 2 * V_DIM), 1)
    for i in range(seq // tq):
        lo, hi = i * tq, (i + 1) * tq
        outs = []
        for hh in range(2):
            blk = slice(hh * HEAD_PAD, (hh + 1) * HEAD_PAD)
            q = q_ref[lo:hi, blk]
            s_d = jnp.where(c_i <= r_i, _dot_nt(q, k_ref[lo:hi, blk]), -jnp.inf)
            m = jnp.max(s_d, axis=-1, keepdims=True)
            if i > 0:
                s_o = _dot_nt(q, k_ref[0:lo, blk])
                m = jnp.maximum(m, jnp.max(s_o, axis=-1, keepdims=True))
                p_o = jnp.exp(s_o - m)
            p_d = jnp.exp(s_d - m)
            l = jnp.sum(p_d, axis=-1, keepdims=True)
            acc = _dot(p_d.astype(BF16), v_ref[lo:hi, :])
            if i > 0:
                l = l + jnp.sum(p_o, axis=-1, keepdims=True)
                acc = acc + _dot(p_o.astype(BF16), v_ref[0:lo, :])
            outs.append(acc * (1.0 / l))
        o_ref[lo:hi, :] = jnp.where(lane < V_DIM, outs[0], outs[1]).astype(o_ref.dtype)


def _attention(q, k, v, batch, seq):
    t = q.shape[0]
    return pl.pallas_call(
        _attn_kernel,
        grid=(batch, MLA_HEADS // 2),
        in_specs=[pl.BlockSpec((seq, 2 * HEAD_PAD), lambda b, hp: (b, hp)),
                  pl.BlockSpec((seq, 2 * HEAD_PAD), lambda b, hp: (b, hp)),
                  pl.BlockSpec((seq, 2 * V_DIM), lambda b, hp: (b, hp))],
        out_specs=pl.BlockSpec((seq, 2 * V_DIM), lambda b, hp: (b, hp)),
        out_shape=jax.ShapeDtypeStruct((t, MLA_HEADS * V_DIM), BF16),
        compiler_params=pltpu.CompilerParams(
            dimension_semantics=("arbitrary", "arbitrary"), vmem_limit_bytes=VMEM_LIMIT),
        name="attention",
    )(q, k, v)


def _first_max_index(cur, idx, sentinel):
    m = jnp.max(cur, axis=0, keepdims=True)
    first = jnp.min(jnp.where(cur == m, idx, sentinel), axis=0, keepdims=True)
    return m, first


def _route(logits_t, bias_t):
    n_tok = logits_t.shape[1]
    scores = jax.nn.sigmoid(logits_t)
    sel = scores + bias_t
    neg = -jnp.inf
    sub = lax.broadcasted_iota(jnp.int32, (GROUP_SIZE, n_tok), 0).astype(F32)
    gid = lax.broadcasted_iota(jnp.int32, (N_EXPERT_GROUPS, n_tok), 0).astype(F32)
    gscore = jnp.zeros((N_EXPERT_GROUPS, n_tok), F32)
    for g in range(N_EXPERT_GROUPS):
        s = sel[g * GROUP_SIZE:(g + 1) * GROUP_SIZE]
        m1, i1 = _first_max_index(s, sub, float(GROUP_SIZE))
        m2 = jnp.max(jnp.where(sub == i1, neg, s), axis=0, keepdims=True)
        gscore = jnp.where(gid == float(g), m1 + m2, gscore)
    eid = lax.broadcasted_iota(jnp.int32, (N_EXPERTS, n_tok), 0).astype(F32)
    egroup = jnp.floor(eid * (1.0 / GROUP_SIZE))
    allowed = jnp.zeros((N_EXPERTS, n_tok), F32)
    cur = gscore
    for _ in range(TOPK_GROUPS):
        _, gi = _first_max_index(cur, gid, float(N_EXPERT_GROUPS))
        cur = jnp.where(gid == gi, neg, cur)
        allowed = jnp.where(egroup == gi, 1.0, allowed)
    cur = jnp.where(allowed > 0.0, sel, neg)
    chosen = jnp.zeros((N_EXPERTS, n_tok), F32)
    for _ in range(TOP_K):
        _, ei = _first_max_index(cur, eid, float(N_EXPERTS))
        hit = eid == ei
        cur = jnp.where(hit, neg, cur)
        chosen = jnp.where(hit, 1.0, chosen)
    w = jnp.where(chosen > 0.0, scores, 0.0)
    return w / jnp.sum(w, axis=0, keepdims=True) * ROUTED_SCALE


def _split_bf16(x):
    hi = x.astype(BF16)
    lo = (x - hi.astype(F32)).astype(BF16)
    return hi, lo


def _mixer_out_kernel(mix_ref, att_ref, x_ref, gate_ref, shift_ref, scale_ref, wo_ref, wr_ref, rb_ref,
                      x2_ref, h2_ref, cw_ref):
    half = mix_ref.shape[1]
    y = _dot(mix_ref[...], wo_ref[0:half, :]) + _dot(att_ref[...], wo_ref[half:, :])
    x2 = x_ref[...] + gate_ref[0] * y
    x2_ref[...] = x2
    h2 = _rms(x2) * (1.0 + scale_ref[0]) + shift_ref[0]
    h2_ref[...] = h2.astype(h2_ref.dtype)
    h_hi, h_lo = _split_bf16(h2)
    w_hi, w_lo = _split_bf16(wr_ref[...])
    logits_t = _dot_nt(w_hi, h_hi) + (_dot_nt(w_hi, h_lo) + _dot_nt(w_lo, h_hi))
    cw_t = _route(logits_t, rb_ref[...])
    pad = jnp.zeros((LANES - N_EXPERTS, cw_t.shape[1]), F32)
    cw_ref[...] = jnp.concatenate([cw_t, pad], axis=0).T


def _mixer_out(mix, att, x2d, gate, shift, scale, p, seq):
    t, d = x2d.shape
    tm = 512
    tiles_per_seq = seq // tm

    def full(shape):
        return pl.BlockSpec(shape, lambda i: (0,) * len(shape))

    def mod():
        return pl.BlockSpec((1, 1, d), lambda i: (i // tiles_per_seq, 0, 0))

    return pl.pallas_call(
        _mixer_out_kernel,
        grid=(t // tm,),
        in_specs=[pl.BlockSpec((tm, mix.shape[1]), lambda i: (i, 0)),
                  pl.BlockSpec((tm, att.shape[1]), lambda i: (i, 0)),
                  pl.BlockSpec((tm, d), lambda i: (i, 0)), mod(), mod(), mod(),
                  full((d, d)), full((N_EXPERTS, d)), full((N_EXPERTS, 1))],
        out_specs=[pl.BlockSpec((tm, d), lambda i: (i, 0)),
                   pl.BlockSpec((tm, d), lambda i: (i, 0)),
                   pl.BlockSpec((tm, LANES), lambda i: (i, 0))],
        out_shape=[jax.ShapeDtypeStruct((t, d), F32),
                   jax.ShapeDtypeStruct((t, d), BF16),
                   jax.ShapeDtypeStruct((t, LANES), F32)],
        compiler_params=pltpu.CompilerParams(dimension_semantics=("arbitrary",),
                                             vmem_limit_bytes=VMEM_LIMIT),
        name="mixer_out",
    )(mix, att, x2d, gate, shift, scale, p["w_out"], p["w_router_t"], p["router_bias"])


def _moe_kernel(h_ref, cw_ref, x2_ref, gate_ref, wg_ref, wu_ref, wd_ref, sg_ref, su_ref, sd_ref, fg_ref,
                o_ref, acc_ref, *, final_norm):
    e = pl.program_id(1)
    h = h_ref[...]

    @pl.when(e == 0)
    def _():
        act = _silu(_dot(h, sg_ref[...])) * _dot(h, su_ref[...])
        acc_ref[...] = _dot(act.astype(BF16), sd_ref[...])

    act = _silu(_dot(h, wg_ref[0, 0].astype(BF16))) * _dot(h, wu_ref[0, 0].astype(BF16))
    y = _dot(act.astype(BF16), wd_ref[0, 0].astype(BF16))
    lane = lax.broadcasted_iota(jnp.int32, cw_ref.shape, 1)
    col = jnp.sum(jnp.where(lane == e, cw_ref[...], 0.0), axis=-1, keepdims=True)
    acc_ref[...] += col * y

    @pl.when(e == pl.num_programs(1) - 1)
    def _():
        x3 = x2_ref[...] + gate_ref[0] * acc_ref[...]
        if final_norm:
            x3 = _rms(x3) * fg_ref[...]
        o_ref[...] = x3


def _moe(h2, cw, x2, gate, p, w_gate, w_up, w_down, layer, final_gain, seq, final_norm):
    t, d = x2.shape
    tm = 1024
    tiles_per_seq = seq // tm
    ff = w_gate.shape[3]
    return pl.pallas_call(
        functools.partial(_moe_kernel, final_norm=final_norm),
        grid=(t // tm, N_EXPERTS),
        in_specs=[pl.BlockSpec((tm, d), lambda i, e: (i, 0)),
                  pl.BlockSpec((tm, LANES), lambda i, e: (i, 0)),
                  pl.BlockSpec((tm, d), lambda i, e: (i, 0)),
                  pl.BlockSpec((1, 1, d), lambda i, e: (i // tiles_per_seq, 0, 0)),
                  pl.BlockSpec((1, 1, d, ff), lambda i, e: (layer, e, 0, 0)),
                  pl.BlockSpec((1, 1, d, ff), lambda i, e: (layer, e, 0, 0)),
                  pl.BlockSpec((1, 1, ff, d), lambda i, e: (layer, e, 0, 0)),
                  pl.BlockSpec((d, ff), lambda i, e: (0, 0)),
                  pl.BlockSpec((d, ff), lambda i, e: (0, 0)),
                  pl.BlockSpec((ff, d), lambda i, e: (0, 0)),
                  pl.BlockSpec((1, d), lambda i, e: (0, 0))],
        out_specs=pl.BlockSpec((tm, d), lambda i, e: (i, 0)),
        out_shape=jax.ShapeDtypeStruct((t, d), F32),
        scratch_shapes=[pltpu.VMEM((tm, d), F32)],
        compiler_params=pltpu.CompilerParams(dimension_semantics=("arbitrary", "arbitrary"),
                                             vmem_limit_bytes=VMEM_LIMIT),
        name="moe",
    )(h2, cw, x2, gate, w_gate, w_up, w_down, p["ws_gate"], p["ws_up"], p["ws_down"], final_gain)


def _prep_layer(w_in, w_pool, pool_scale, w_spatial, b_spatial, g_q, w_uq, g_kv, w_ukv, w_out,
                w_router, router_bias, ws_gate, ws_up, ws_down):
    d = w_in.shape[0]
    o_kpe = POOL_WIDTH + 2 * SG_WIDTH + Q_LORA + KV_LORA
    x1 = w_in[:, o_kpe:o_kpe + HALF_ROPE]
    x2 = w_in[:, o_kpe + HALF_ROPE:o_kpe + QK_ROPE]
    zl = jnp.zeros((d, QK_NOPE), F32)
    zr = jnp.zeros((d, HEAD_PAD - QK_NOPE - QK_ROPE), F32)
    w_in_pad = jnp.concatenate([w_in[:, :o_kpe], zl, x1, x2, zr, zl, x2, x1, zr], axis=1)

    uq = w_uq.reshape(Q_LORA, MLA_HEADS, QK_NOPE + QK_ROPE)
    q1 = uq[..., QK_NOPE:QK_NOPE + HALF_ROPE]
    q2 = uq[..., QK_NOPE + HALF_ROPE:]
    zq = jnp.zeros((Q_LORA, MLA_HEADS, HEAD_PAD - QK_NOPE - QK_ROPE), F32)
    w_uq_pad = jnp.concatenate([uq, zq], axis=-1).reshape(Q_LORA, MLA_HEADS * HEAD_PAD)
    w_uq_sw = jnp.concatenate([jnp.zeros_like(uq[..., :QK_NOPE]), q2, q1, zq], axis=-1)
    w_uq_sw = w_uq_sw.reshape(Q_LORA, MLA_HEADS * HEAD_PAD)

    ukv = w_ukv.reshape(KV_LORA, MLA_HEADS, QK_NOPE + V_DIM)
    zk = jnp.zeros((KV_LORA, MLA_HEADS, HEAD_PAD - QK_NOPE), F32)
    w_k = jnp.concatenate([ukv[..., :QK_NOPE], zk], axis=-1).reshape(KV_LORA, MLA_HEADS * HEAD_PAD)
    w_v = ukv[..., QK_NOPE:].reshape(KV_LORA, MLA_HEADS * V_DIM)

    w_pool_bd = jax.scipy.linalg.block_diag(*[w_pool[g] for g in range(len(POOL_WINDOWS))])
    b_sp = jnp.repeat(b_spatial.T, SG_DIM, axis=1)
    return {
        "w_in": w_in_pad.astype(BF16), "w_pool": w_pool_bd.astype(BF16),
        "pool_scale": pool_scale.reshape(1, -1), "w_spatial": w_spatial, "b_spatial": b_sp,
        "g_q": g_q.reshape(1, -1), "w_uq": w_uq_pad.astype(BF16), "w_uq_sw": w_uq_sw.astype(BF16),
        "g_kv": g_kv.reshape(1, -1), "w_k": w_k.astype(BF16), "w_v": w_v.astype(BF16),
        "w_out": w_out.astype(BF16), "w_router_t": w_router.T, "router_bias": router_bias.reshape(-1, 1),
        "ws_gate": ws_gate.astype(BF16), "ws_up": ws_up.astype(BF16), "ws_down": ws_down.astype(BF16),
    }


def kernel(x, c, positions, w_ada, b_ada, w_in, w_pool, pool_scale, w_spatial, b_spatial, g_q, w_uq, g_kv, w_ukv, w_out, w_router, router_bias, w_gate, w_up, w_down, ws_gate, ws_up, ws_down, final_gain):
    batch, seq, d = x.shape
    depth = w_ada.shape[0]
    mod = _ada_mod(c, w_ada, b_ada)
    cos_t, sin_t = _rope_tables(positions)
    xt = x.reshape(batch * seq, d)
    fg = final_gain.reshape(1, d)
    for l in range(depth):
        p = _prep_layer(w_in[l], w_pool[l], pool_scale[l], w_spatial[l], b_spatial[l], g_q[l], w_uq[l],
                        g_kv[l], w_ukv[l], w_out[l], w_router[l], router_bias[l], ws_gate[l], ws_up[l],
                        ws_down[l])
        shift1, scale1, gate1, shift2, scale2, gate2 = [
            mod[l, :, k * d:(k + 1) * d].reshape(batch, 1, d) for k in range(6)]
        mix, q, k, v = _mixer_in(xt, shift1, scale1, cos_t, sin_t, p, seq)
        att = _attention(q, k, v, batch, seq)
        x2, h2, cw = _mixer_out(mix, att, xt, gate1, shift2, scale2, p, seq)
        xt = _moe(h2, cw, x2, gate2, p, w_gate, w_up, w_down, l, fg, seq, final_norm=(l == depth - 1))
    return xt.reshape(batch, seq, d)
```

```python
import functools

import jax
import jax.numpy as jnp
from jax import lax
from jax.experimental import pallas as pl
from jax.experimental.pallas import tpu as pltpu

F32 = jnp.float32
BF16 = jnp.bfloat16

EPS = 1e-6
LANES = 128
POOL_WINDOWS = (2, 4, 8, 16)
POOL_WIDTH = 256
POOL_CH = 64
MAX_WINDOW = 16
SG_HEADS = 4
SG_WIDTH = 256
SG_DIM = 64
SG_CHUNK = 128
MLA_HEADS = 8
V_DIM = 64
QK_NOPE = 64
QK_ROPE = 32
HALF_ROPE = QK_ROPE // 2
Q_LORA = 256
KV_LORA = 128
ROPE_THETA = 10000.0
ATTN_SCALE = (QK_NOPE + QK_ROPE) ** -0.5
HEAD_PAD = 128
N_EXPERTS = 64
TOP_K = 8
N_EXPERT_GROUPS = 8
GROUP_SIZE = N_EXPERTS // N_EXPERT_GROUPS
TOPK_GROUPS = 4
ROUTED_SCALE = 2.5
VMEM_LIMIT = 52 * 1024 * 1024


def _dot(a, b):
    return jnp.dot(a, b, preferred_element_type=F32)


def _dot_nt(a, b):
    return lax.dot_general(a, b, (((1,), (1,)), ((), ())), preferred_element_type=F32)


def _rms(x):
    return x * lax.rsqrt(jnp.mean(x * x, axis=-1, keepdims=True) + EPS)


def _gelu_tanh(x):
    c = (2.0 / jnp.pi) ** 0.5
    return x * (0.5 * (1.0 + jnp.tanh(c * (x + 0.044715 * (x * x * x)))))


def _silu(x):
    return x * jax.nn.sigmoid(x)


def _ada_kernel(c_ref, w_ref, b_ref, o_ref):
    cond = _silu(c_ref[...])
    o_ref[0] = _dot(cond.astype(BF16), w_ref[0].astype(BF16)) + b_ref[0]


def _ada_mod(c, w_ada, b_ada):
    depth, d, n = w_ada.shape
    b = c.shape[0]
    nt = 1536
    return pl.pallas_call(
        _ada_kernel,
        grid=(depth, n // nt),
        in_specs=[pl.BlockSpec((b, d), lambda l, j: (0, 0)),
                  pl.BlockSpec((1, d, nt), lambda l, j: (l, 0, j)),
                  pl.BlockSpec((1, 1, nt), lambda l, j: (l, 0, j))],
        out_specs=pl.BlockSpec((1, b, nt), lambda l, j: (l, 0, j)),
        out_shape=jax.ShapeDtypeStruct((depth, b, n), F32),
        compiler_params=pltpu.CompilerParams(vmem_limit_bytes=VMEM_LIMIT),
        name="ada_mod",
    )(c, w_ada, b_ada.reshape(depth, 1, n))


def _rope_kernel(pos_ref, invf_ref, sign_ref, cos_ref, sin_ref):
    ang = pos_ref[...].astype(F32) * invf_ref[...]
    cos_ref[...] = jnp.cos(ang)
    sin_ref[...] = jnp.sin(ang) * sign_ref[...]


def _rope_tables(positions):
    t = positions.size
    tm = 2048
    inv_freq = ROPE_THETA ** (-jnp.arange(0, QK_ROPE, 2, dtype=F32) / QK_ROPE)
    invf = jnp.zeros((1, HEAD_PAD), F32)
    invf = invf.at[0, QK_NOPE:QK_NOPE + HALF_ROPE].set(inv_freq)
    invf = invf.at[0, QK_NOPE + HALF_ROPE:QK_NOPE + QK_ROPE].set(inv_freq)
    sign = jnp.zeros((1, HEAD_PAD), F32)
    sign = sign.at[0, QK_NOPE:QK_NOPE + HALF_ROPE].set(-1.0)
    sign = sign.at[0, QK_NOPE + HALF_ROPE:QK_NOPE + QK_ROPE].set(1.0)
    return pl.pallas_call(
        _rope_kernel,
        grid=(t // tm,),
        in_specs=[pl.BlockSpec((tm, 1), lambda i: (i, 0)),
                  pl.BlockSpec((1, HEAD_PAD), lambda i: (0, 0)),
                  pl.BlockSpec((1, HEAD_PAD), lambda i: (0, 0))],
        out_specs=[pl.BlockSpec((tm, HEAD_PAD), lambda i: (i, 0)),
                   pl.BlockSpec((tm, HEAD_PAD), lambda i: (i, 0))],
        out_shape=[jax.ShapeDtypeStruct((t, HEAD_PAD), F32)] * 2,
        name="rope_tables",
    )(positions.reshape(t, 1), invf, sign)


def _mixer_in_kernel(x_ref, shift_ref, scale_ref, win_ref, wpool_ref, pscale_ref, wsp_ref, bsp_ref,
                     gq_ref, wuq_ref, wuqs_ref, gkv_ref, wk_ref, wv_ref, cos_ref, sin_ref,
                     mix_ref, q_ref, k_ref, v_ref, carry_ref, *, tiles_per_seq):
    tm = x_ref.shape[0]
    ti = pl.program_id(0) % tiles_per_seq
    h = _rms(x_ref[...]) * (1.0 + scale_ref[0]) + shift_ref[0]
    z = _dot(h.astype(BF16), win_ref[...])

    a = z[:, 0:POOL_WIDTH]

    @pl.when(ti == 0)
    def _():
        carry_ref[...] = jnp.zeros_like(carry_ref)

    ext = jnp.concatenate([carry_ref[...], a], axis=0)
    carry_ref[...] = a[tm - MAX_WINDOW:, :]
    p1 = ext + pltpu.roll(ext, 1, 0)
    p2 = p1 + pltpu.roll(p1, 2, 0)
    p3 = p2 + pltpu.roll(p2, 4, 0)
    p4 = p3 + pltpu.roll(p3, 8, 0)
    lane = lax.broadcasted_iota(jnp.int32, (tm, POOL_WIDTH), 1)
    row = lax.broadcasted_iota(jnp.int32, (tm, POOL_WIDTH), 0) + (ti * tm + 1)
    g0, g1, g2 = lane < POOL_CH, lane < 2 * POOL_CH, lane < 3 * POOL_CH
    wsum = jnp.where(g0, p1[MAX_WINDOW:], jnp.where(g1, p2[MAX_WINDOW:],
                     jnp.where(g2, p3[MAX_WINDOW:], p4[MAX_WINDOW:])))
    width = jnp.where(g0, POOL_WINDOWS[0], jnp.where(g1, POOL_WINDOWS[1],
                      jnp.where(g2, POOL_WINDOWS[2], POOL_WINDOWS[3])))
    cnt = jnp.minimum(row, width).astype(F32)
    dlt = wsum / cnt - a
    y_pool = _dot(dlt.astype(BF16), wpool_ref[...]) * pscale_ref[...]
    mix_ref[:, 0:POOL_WIDTH] = y_pool.astype(mix_ref.dtype)

    ug = _gelu_tanh(z[:, POOL_WIDTH:POOL_WIDTH + SG_WIDTH])
    vg = _gelu_tanh(z[:, POOL_WIDTH + SG_WIDTH:POOL_WIDTH + 2 * SG_WIDTH])
    mu = jnp.mean(vg, axis=-1, keepdims=True)
    vc = vg - mu
    var = jnp.mean(vc * vc, axis=-1, keepdims=True)
    vn = (vc * lax.rsqrt(var + EPS)).astype(BF16)
    r_i = lax.broadcasted_iota(jnp.int32, (SG_CHUNK, SG_CHUNK), 0)
    c_i = lax.broadcasted_iota(jnp.int32, (SG_CHUNK, SG_CHUNK), 1)
    wms = [jnp.where(c_i <= r_i, wsp_ref[hh], 0.0).astype(BF16) for hh in range(SG_HEADS)]
    lane_head = lax.broadcasted_iota(jnp.int32, (SG_CHUNK, SG_WIDTH), 1) // SG_DIM
    for cidx in range(tm // SG_CHUNK):
        rows = slice(cidx * SG_CHUNK, (cidx + 1) * SG_CHUNK)
        vchunk = vn[rows]
        mixed = bsp_ref[...]
        for hh in range(SG_HEADS):
            mixed = mixed + jnp.where(lane_head == hh, _dot(wms[hh], vchunk), 0.0)
        mix_ref[rows, POOL_WIDTH:POOL_WIDTH + SG_WIDTH] = (ug[rows] * mixed).astype(mix_ref.dtype)

    o_cq = POOL_WIDTH + 2 * SG_WIDTH
    o_ckv = o_cq + Q_LORA
    o_kpe = o_ckv + KV_LORA
    cos = cos_ref[...]
    sin = sin_ref[...]
    cqn = (_rms(z[:, o_cq:o_ckv]) * gq_ref[...]).astype(BF16)
    q = _dot(cqn, wuq_ref[...])
    qs = _dot(cqn, wuqs_ref[...])
    ckvn = (_rms(z[:, o_ckv:o_kpe]) * gkv_ref[...]).astype(BF16)
    kn = _dot(ckvn, wk_ref[...])
    kpe = z[:, o_kpe:o_kpe + HEAD_PAD] * cos + z[:, o_kpe + HEAD_PAD:o_kpe + 2 * HEAD_PAD] * sin
    for hh in range(MLA_HEADS):
        blk = slice(hh * HEAD_PAD, (hh + 1) * HEAD_PAD)
        q_ref[:, blk] = ((q[:, blk] * cos + qs[:, blk] * sin) * ATTN_SCALE).astype(q_ref.dtype)
        k_ref[:, blk] = (kn[:, blk] + kpe).astype(k_ref.dtype)
    v_ref[...] = _dot(ckvn, wv_ref[...]).astype(v_ref.dtype)


def _mixer_in(x2d, shift, scale, cos_t, sin_t, p, seq):
    t, d = x2d.shape
    tm = 512
    tiles_per_seq = seq // tm
    nz = p["w_in"].shape[1]
    hq = MLA_HEADS * HEAD_PAD

    def full(shape):
        return pl.BlockSpec(shape, lambda i: (0,) * len(shape))

    def mod():
        return pl.BlockSpec((1, 1, d), lambda i: (i // tiles_per_seq, 0, 0))

    return pl.pallas_call(
        functools.partial(_mixer_in_kernel, tiles_per_seq=tiles_per_seq),
        grid=(t // tm,),
        in_specs=[pl.BlockSpec((tm, d), lambda i: (i, 0)), mod(), mod(),
                  full((d, nz)), full((POOL_WIDTH, POOL_WIDTH)), full((1, POOL_WIDTH)),
                  full((SG_HEADS, SG_CHUNK, SG_CHUNK)), full((SG_CHUNK, SG_WIDTH)),
                  full((1, Q_LORA)), full((Q_LORA, hq)), full((Q_LORA, hq)),
                  full((1, KV_LORA)), full((KV_LORA, hq)), full((KV_LORA, MLA_HEADS * V_DIM)),
                  pl.BlockSpec((tm, HEAD_PAD), lambda i: (i, 0)),
                  pl.BlockSpec((tm, HEAD_PAD), lambda i: (i, 0))],
        out_specs=[pl.BlockSpec((tm, POOL_WIDTH + SG_WIDTH), lambda i: (i, 0)),
                   pl.BlockSpec((tm, hq), lambda i: (i, 0)),
                   pl.BlockSpec((tm, hq), lambda i: (i, 0)),
                   pl.BlockSpec((tm, MLA_HEADS * V_DIM), lambda i: (i, 0))],
        out_shape=[jax.ShapeDtypeStruct((t, POOL_WIDTH + SG_WIDTH), BF16),
                   jax.ShapeDtypeStruct((t, hq), BF16),
                   jax.ShapeDtypeStruct((t, hq), BF16),
                   jax.ShapeDtypeStruct((t, MLA_HEADS * V_DIM), BF16)],
        scratch_shapes=[pltpu.VMEM((MAX_WINDOW, POOL_WIDTH), F32)],
        compiler_params=pltpu.CompilerParams(dimension_semantics=("arbitrary",),
                                             vmem_limit_bytes=VMEM_LIMIT),
        name="mixer_in",
    )(x2d, shift, scale, p["w_in"], p["w_pool"], p["pool_scale"], p["w_spatial"], p["b_spatial"],
      p["g_q"], p["w_uq"], p["w_uq_sw"], p["g_kv"], p["w_k"], p["w_v"], cos_t, sin_t)


ATTN_TQ = 256


def _attn_kernel(q_ref, k_ref, v_ref, o_ref):
    seq = q_ref.shape[0]
    tq = ATTN_TQ
    r_i = lax.broadcasted_iota(jnp.int32, (tq, tq), 0)
    c_i = lax.broadcasted_iota(jnp.int32, (tq, tq), 1)
    lane = lax.broadcasted_iota(jnp.int32, (tq, 2 * V_DIM), 1)
    for i in range(seq // tq):
        lo, hi = i * tq, (i + 1) * tq
        outs = []
        for hh in range(2):
            blk = slice(hh * HEAD_PAD, (hh + 1) * HEAD_PAD)
            q = q_ref[lo:hi, blk]
            s_d = jnp.where(c_i <= r_i, _dot_nt(q, k_ref[lo:hi, blk]), -jnp.inf)
            m = jnp.max(s_d, axis=-1, keepdims=True)
            if i > 0:
                s_o = _dot_nt(q, k_ref[0:lo, blk])
                m = jnp.maximum(m, jnp.max(s_o, axis=-1, keepdims=True))
                p_o = jnp.exp(s_o - m)
            p_d = jnp.exp(s_d - m)
            l = jnp.sum(p_d, axis=-1, keepdims=True)
            acc = _dot(p_d.astype(BF16), v_ref[lo:hi, :])
            if i > 0:
                l = l + jnp.sum(p_o, axis=-1, keepdims=True)
                acc = acc + _dot(p_o.astype(BF16), v_ref[0:lo, :])
            outs.append(acc * (1.0 / l))
        o_ref[lo:hi, :] = jnp.where(lane < V_DIM, outs[0], outs[1]).astype(o_ref.dtype)


def _attention(q, k, v, batch, seq):
    t = q.shape[0]
    return pl.pallas_call(
        _attn_kernel,
        grid=(batch, MLA_HEADS // 2),
        in_specs=[pl.BlockSpec((seq, 2 * HEAD_PAD), lambda b, hp: (b, hp)),
                  pl.BlockSpec((seq, 2 * HEAD_PAD), lambda b, hp: (b, hp)),
                  pl.BlockSpec((seq, 2 * V_DIM), lambda b, hp: (b, hp))],
        out_specs=pl.BlockSpec((seq, 2 * V_DIM), lambda b, hp: (b, hp)),
        out_shape=jax.ShapeDtypeStruct((t, MLA_HEADS * V_DIM), BF16),
        compiler_params=pltpu.CompilerParams(
            dimension_semantics=("arbitrary", "arbitrary"), vmem_limit_bytes=VMEM_LIMIT),
        name="attention",
    )(q, k, v)


def _first_max_index(cur, idx, sentinel):
    m = jnp.max(cur, axis=0, keepdims=True)
    first = jnp.min(jnp.where(cur == m, idx, sentinel), axis=0, keepdims=True)
    return m, first


def _route(logits_t, bias_t):
    n_tok = logits_t.shape[1]
    scores = jax.nn.sigmoid(logits_t)
    sel = scores + bias_t
    neg = -jnp.inf
    sub = lax.broadcasted_iota(jnp.int32, (GROUP_SIZE, n_tok), 0).astype(F32)
    gid = lax.broadcasted_iota(jnp.int32, (N_EXPERT_GROUPS, n_tok), 0).astype(F32)
    gscore = jnp.zeros((N_EXPERT_GROUPS, n_tok), F32)
    for g in range(N_EXPERT_GROUPS):
        s = sel[g * GROUP_SIZE:(g + 1) * GROUP_SIZE]
        m1, i1 = _first_max_index(s, sub, float(GROUP_SIZE))
        m2 = jnp.max(jnp.where(sub == i1, neg, s), axis=0, keepdims=True)
        gscore = jnp.where(gid == float(g), m1 + m2, gscore)
    eid = lax.broadcasted_iota(jnp.int32, (N_EXPERTS, n_tok), 0).astype(F32)
    egroup = jnp.floor(eid * (1.0 / GROUP_SIZE))
    allowed = jnp.zeros((N_EXPERTS, n_tok), F32)
    cur = gscore
    for _ in range(TOPK_GROUPS):
        _, gi = _first_max_index(cur, gid, float(N_EXPERT_GROUPS))
        cur = jnp.where(gid == gi, neg, cur)
        allowed = jnp.where(egroup == gi, 1.0, allowed)
    cur = jnp.where(allowed > 0.0, sel, neg)
    chosen = jnp.zeros((N_EXPERTS, n_tok), F32)
    for _ in range(TOP_K):
        _, ei = _first_max_index(cur, eid, float(N_EXPERTS))
        hit = eid == ei
        cur = jnp.where(hit, neg, cur)
        chosen = jnp.where(hit, 1.0, chosen)
    w = jnp.where(chosen > 0.0, scores, 0.0)
    return w / jnp.sum(w, axis=0, keepdims=True) * ROUTED_SCALE


def _split_bf16(x):
    hi = x.astype(BF16)
    lo = (x - hi.astype(F32)).astype(BF16)
    return hi, lo


ROW_TILE = 8


def _mixer_out_kernel(mix_ref, att_ref, x_ref, gate1_ref, shift_ref, scale_ref, gate2_ref, wo_ref, wr_ref,
                      rb_ref, sg_ref, su_ref, sd_ref, base_ref, h2r_ref, cw_ref):
    tm = x_ref.shape[0]
    half = mix_ref.shape[1]
    y = _dot(mix_ref[...], wo_ref[0:half, :]) + _dot(att_ref[...], wo_ref[half:, :])
    x2 = x_ref[...] + gate1_ref[0] * y
    h2 = _rms(x2) * (1.0 + scale_ref[0]) + shift_ref[0]
    for j in range(ROW_TILE):
        h2r_ref[pl.ds(j, tm, stride=ROW_TILE), :] = h2[:, j * LANES:(j + 1) * LANES]
    hb = h2.astype(BF16)
    act = _silu(_dot(hb, sg_ref[...])) * _dot(hb, su_ref[...])
    base_ref[...] = x2 + gate2_ref[0] * _dot(act.astype(BF16), sd_ref[...])
    h_hi, h_lo = _split_bf16(h2)
    w_hi, w_lo = _split_bf16(wr_ref[...])
    logits_t = _dot_nt(w_hi, h_hi) + (_dot_nt(w_hi, h_lo) + _dot_nt(w_lo, h_hi))
    cw_ref[...] = _route(logits_t, rb_ref[...])


def _mixer_out(mix, att, x2d, gate1, shift, scale, gate2, p, seq):
    t, d = x2d.shape
    tm = 512
    tiles_per_seq = seq // tm
    ff = p["ws_gate"].shape[1]

    def full(shape):
        return pl.BlockSpec(shape, lambda i: (0,) * len(shape))

    def mod():
        return pl.BlockSpec((1, 1, d), lambda i: (i // tiles_per_seq, 0, 0))

    return pl.pallas_call(
        _mixer_out_kernel,
        grid=(t // tm,),
        in_specs=[pl.BlockSpec((tm, mix.shape[1]), lambda i: (i, 0)),
                  pl.BlockSpec((tm, att.shape[1]), lambda i: (i, 0)),
                  pl.BlockSpec((tm, d), lambda i: (i, 0)), mod(), mod(), mod(), mod(),
                  full((d, d)), full((N_EXPERTS, d)), full((N_EXPERTS, 1)),
                  full((d, ff)), full((d, ff)), full((ff, d))],
        out_specs=[pl.BlockSpec((tm, d), lambda i: (i, 0)),
                   pl.BlockSpec((tm * ROW_TILE, LANES), lambda i: (i, 0)),
                   pl.BlockSpec((N_EXPERTS, tm), lambda i: (0, i))],
        out_shape=[jax.ShapeDtypeStruct((t, d), F32),
                   jax.ShapeDtypeStruct((t * ROW_TILE, LANES), F32),
                   jax.ShapeDtypeStruct((N_EXPERTS, t), F32)],
        compiler_params=pltpu.CompilerParams(dimension_semantics=("arbitrary",),
                                             vmem_limit_bytes=VMEM_LIMIT),
        name="mixer_out",
    )(mix, att, x2d, gate1, shift, scale, gate2, p["w_out"], p["w_router_t"], p["router_bias"],
      p["ws_gate"], p["ws_up"], p["ws_down"])


EXPERT_BLOCK = 128
TOK_BITS = 12


def _plan_kernel(cw_ref, tok_ref, w_ref, tbl_ref):
    n_e, ts = cw_ref.shape
    cw = cw_ref[...]
    chosen = cw > 0.0
    cf = jnp.where(chosen, 1.0, 0.0).astype(BF16)
    r_i = lax.broadcasted_iota(jnp.int32, (LANES, LANES), 0)
    c_i = lax.broadcasted_iota(jnp.int32, (LANES, LANES), 1)
    before = jnp.where(r_i < c_i, 1.0, 0.0).astype(BF16)
    ones = jnp.ones((LANES, LANES), BF16)
    carry = jnp.zeros((n_e, LANES), F32)
    ranks = []
    for k in range(ts // LANES):
        ck = cf[:, k * LANES:(k + 1) * LANES]
        ranks.append(_dot(ck, before) + carry)
        carry = carry + _dot(ck, ones)
    rank = jnp.concatenate(ranks, axis=1).astype(jnp.int32)
    lane = lax.broadcasted_iota(jnp.int32, (n_e, ts), 1)
    packed = jnp.where(chosen, ((lane - rank) << TOK_BITS) | lane, -1)
    w = jnp.where(chosen, cw, 0.0)
    for bit in range(ts.bit_length() - 1):
        step = 1 << bit
        src_p = pltpu.roll(packed, ts - step, 1)
        src_w = pltpu.roll(w, ts - step, 1)
        take = (src_p >= 0) & (lane < ts - step) & (((src_p >> (TOK_BITS + bit)) & 1) == 1)
        keep = (packed >= 0) & (((packed >> (TOK_BITS + bit)) & 1) == 0)
        packed = jnp.where(take, src_p, jnp.where(keep, packed, -1))
        w = jnp.where(take, src_w, jnp.where(keep, w, 0.0))
    tok_ref[0] = jnp.where(packed >= 0, packed & ((1 << TOK_BITS) - 1), ts)
    w_ref[0] = w

    nb_lanes = tbl_ref.shape[2]
    nblk = jnp.floor((carry + (EXPERT_BLOCK - 1.0)) * (1.0 / EXPERT_BLOCK))
    e_r = lax.broadcasted_iota(jnp.int32, (n_e, n_e), 0)
    e_c = lax.broadcasted_iota(jnp.int32, (n_e, n_e), 1)
    upto = jnp.where(e_c <= e_r, 1.0, 0.0).astype(BF16)
    cum = _dot(upto, nblk.astype(BF16))[:, 0:1]
    total = jnp.max(cum, axis=0, keepdims=True)
    bidx = lax.broadcasted_iota(jnp.int32, (n_e, nb_lanes), 1).astype(F32)
    done = cum <= bidx
    e_of_b = jnp.sum(jnp.where(done, 1.0, 0.0), axis=0, keepdims=True)
    part = bidx[0:1] - jnp.sum(jnp.where(done, nblk[:, 0:1], 0.0), axis=0, keepdims=True)
    e_last = jnp.sum(jnp.where(cum < total, 1.0, 0.0), axis=0, keepdims=True)
    valid = bidx[0:1] < total
    e_of_b = jnp.where(valid, e_of_b, e_last)
    part = jnp.where(valid, part, 0.0)
    row = lax.broadcasted_iota(jnp.int32, (ROW_TILE, nb_lanes), 0)
    tbl = jnp.where(row == 0, e_of_b, jnp.where(row == 1, part, total))
    tbl_ref[0] = tbl.astype(jnp.int32)


def _plan(cw_t, seq):
    n_e, t = cw_t.shape
    n_seq = t // seq
    nb_lanes = 2 * LANES
    assert seq <= (1 << TOK_BITS) and _max_blocks(seq) <= nb_lanes
    return pl.pallas_call(
        _plan_kernel,
        grid=(n_seq,),
        in_specs=[pl.BlockSpec((n_e, seq), lambda s: (0, s))],
        out_specs=[pl.BlockSpec((1, n_e, seq), lambda s: (s, 0, 0)),
                   pl.BlockSpec((1, n_e, seq), lambda s: (s, 0, 0)),
                   pl.BlockSpec((1, ROW_TILE, nb_lanes), lambda s: (s, 0, 0))],
        out_shape=[jax.ShapeDtypeStruct((n_seq, n_e, seq), jnp.int32),
                   jax.ShapeDtypeStruct((n_seq, n_e, seq), F32),
                   jax.ShapeDtypeStruct((n_seq, ROW_TILE, nb_lanes), jnp.int32)],
        compiler_params=pltpu.CompilerParams(dimension_semantics=("arbitrary",),
                                             vmem_limit_bytes=VMEM_LIMIT),
        name="plan",
    )(cw_t)


def _max_blocks(seq):
    return -(-(seq * TOP_K + N_EXPERTS * (EXPERT_BLOCK - 1)) // EXPERT_BLOCK)


XT_PITCH = EXPERT_BLOCK + 1
RMW_BATCH = 8


def _moe_kernel(be_ref, bp_ref, nb_ref, tok_ref, w_ref, h_ref, base_ref, gate_ref, wg_ref, wu_ref, wd_ref,
                fg_ref, o_ref, acc_ref, xt_ref, yt_ref, *, final_norm):
    s = pl.program_id(0)
    b = pl.program_id(1)
    ts = base_ref.shape[0]

    @pl.when(b == 0)
    def _():
        acc_ref[...] = jnp.zeros_like(acc_ref)

    @pl.when(b < nb_ref[s])
    def _():
        for r in range(EXPERT_BLOCK):
            t = jnp.minimum(tok_ref[0, 0, r], ts - 1)
            start = pl.multiple_of(t * ROW_TILE, ROW_TILE)
            xt_ref[pl.ds(r, ROW_TILE, stride=XT_PITCH), :] = h_ref[pl.ds(start, ROW_TILE), :]
        x = jnp.concatenate([xt_ref[pl.ds(j * XT_PITCH, EXPERT_BLOCK), :] for j in range(ROW_TILE)],
                            axis=1).astype(BF16)
        act = _silu(_dot(x, wg_ref[0, 0])) * _dot(x, wu_ref[0, 0])
        y = _dot(act.astype(BF16), wd_ref[0, 0])
        for j in range(ROW_TILE):
            yt_ref[pl.ds(j * XT_PITCH, EXPERT_BLOCK), :] = y[:, j * LANES:(j + 1) * LANES]
        for g in range(EXPERT_BLOCK // RMW_BATCH):
            rows = range(g * RMW_BATCH, (g + 1) * RMW_BATCH)
            starts = [pl.multiple_of(tok_ref[0, 0, r] * ROW_TILE, ROW_TILE) for r in rows]
            olds = [acc_ref[pl.ds(st, ROW_TILE), :] for st in starts]
            news = [old + w_ref[0, 0, r] * yt_ref[pl.ds(r, ROW_TILE, stride=XT_PITCH), :]
                    for r, old in zip(rows, olds)]
            for st, new in zip(starts, news):
                acc_ref[pl.ds(st, ROW_TILE), :] = new

    @pl.when(b == pl.num_programs(1) - 1)
    def _():
        chunk = 256
        for c in range(ts // chunk):
            routed = jnp.concatenate(
                [acc_ref[pl.ds(c * chunk * ROW_TILE + j, chunk, stride=ROW_TILE), :] for j in range(ROW_TILE)],
                axis=1)
            x3 = base_ref[c * chunk:(c + 1) * chunk, :] + gate_ref[0] * routed
            if final_norm:
                x3 = _rms(x3) * fg_ref[...]
            o_ref[c * chunk:(c + 1) * chunk, :] = x3


def _moe(h2rows, base, gate, tok, wts, blk_e, blk_p, n_blk, wg, wu, wd, layer, final_gain, seq, final_norm):
    t, d = base.shape
    n_seq = t // seq
    ff = wg.shape[3]
    parts = seq // EXPERT_BLOCK
    n_b = _max_blocks(seq)
    assert d == ROW_TILE * LANES

    def meta_idx(s, b, be, bp, nb):
        return ((s * N_EXPERTS + be[s, b]) * parts + bp[s, b], 0, 0)

    def seq_idx(s, b, be, bp, nb):
        return (s, 0)

    def w_idx(s, b, be, bp, nb):
        return (layer, be[s, b], 0, 0)

    once = pl.Buffered(1)
    grid_spec = pltpu.PrefetchScalarGridSpec(
        num_scalar_prefetch=3,
        grid=(n_seq, n_b),
        in_specs=[pl.BlockSpec((1, 1, EXPERT_BLOCK), meta_idx, memory_space=pltpu.SMEM),
                  pl.BlockSpec((1, 1, EXPERT_BLOCK), meta_idx, memory_space=pltpu.SMEM),
                  pl.BlockSpec((seq * ROW_TILE, LANES), seq_idx, pipeline_mode=once),
                  pl.BlockSpec((seq, d), seq_idx, pipeline_mode=once),
                  pl.BlockSpec((1, 1, d), lambda s, b, be, bp, nb: (s, 0, 0)),
                  pl.BlockSpec((1, 1, d, ff), w_idx),
                  pl.BlockSpec((1, 1, d, ff), w_idx),
                  pl.BlockSpec((1, 1, ff, d), w_idx),
                  pl.BlockSpec((1, d), lambda s, b, be, bp, nb: (0, 0))],
        out_specs=pl.BlockSpec((seq, d), seq_idx, pipeline_mode=once),
        scratch_shapes=[pltpu.VMEM(((seq + 1) * ROW_TILE, LANES), F32),
                        pltpu.VMEM((ROW_TILE * XT_PITCH, LANES), F32),
                        pltpu.VMEM((ROW_TILE * XT_PITCH, LANES), F32)],
    )
    return pl.pallas_call(
        functools.partial(_moe_kernel, final_norm=final_norm),
        grid_spec=grid_spec,
        out_shape=jax.ShapeDtypeStruct((t, d), F32),
        compiler_params=pltpu.CompilerParams(dimension_semantics=("arbitrary", "arbitrary"),
                                             vmem_limit_bytes=VMEM_LIMIT),
        name="moe",
    )(blk_e, blk_p, n_blk, tok, wts, h2rows, base, gate, wg, wu, wd, final_gain)


def _prep_layer(w_in, w_pool, pool_scale, w_spatial, b_spatial, g_q, w_uq, g_kv, w_ukv, w_out,
                w_router, router_bias, ws_gate, ws_up, ws_down):
    d = w_in.shape[0]
    o_kpe = POOL_WIDTH + 2 * SG_WIDTH + Q_LORA + KV_LORA
    x1 = w_in[:, o_kpe:o_kpe + HALF_ROPE]
    x2 = w_in[:, o_kpe + HALF_ROPE:o_kpe + QK_ROPE]
    zl = jnp.zeros((d, QK_NOPE), F32)
    zr = jnp.zeros((d, HEAD_PAD - QK_NOPE - QK_ROPE), F32)
    w_in_pad = jnp.concatenate([w_in[:, :o_kpe], zl, x1, x2, zr, zl, x2, x1, zr], axis=1)

    uq = w_uq.reshape(Q_LORA, MLA_HEADS, QK_NOPE + QK_ROPE)
    q1 = uq[..., QK_NOPE:QK_NOPE + HALF_ROPE]
    q2 = uq[..., QK_NOPE + HALF_ROPE:]
    zq = jnp.zeros((Q_LORA, MLA_HEADS, HEAD_PAD - QK_NOPE - QK_ROPE), F32)
    w_uq_pad = jnp.concatenate([uq, zq], axis=-1).reshape(Q_LORA, MLA_HEADS * HEAD_PAD)
    w_uq_sw = jnp.concatenate([jnp.zeros_like(uq[..., :QK_NOPE]), q2, q1, zq], axis=-1)
    w_uq_sw = w_uq_sw.reshape(Q_LORA, MLA_HEADS * HEAD_PAD)

    ukv = w_ukv.reshape(KV_LORA, MLA_HEADS, QK_NOPE + V_DIM)
    zk = jnp.zeros((KV_LORA, MLA_HEADS, HEAD_PAD - QK_NOPE), F32)
    w_k = jnp.concatenate([ukv[..., :QK_NOPE], zk], axis=-1).reshape(KV_LORA, MLA_HEADS * HEAD_PAD)
    w_v = ukv[..., QK_NOPE:].reshape(KV_LORA, MLA_HEADS * V_DIM)

    w_pool_bd = jax.scipy.linalg.block_diag(*[w_pool[g] for g in range(len(POOL_WINDOWS))])
    b_sp = jnp.repeat(b_spatial.T, SG_DIM, axis=1)
    return {
        "w_in": w_in_pad.astype(BF16), "w_pool": w_pool_bd.astype(BF16),
        "pool_scale": pool_scale.reshape(1, -1), "w_spatial": w_spatial, "b_spatial": b_sp,
        "g_q": g_q.reshape(1, -1), "w_uq": w_uq_pad.astype(BF16), "w_uq_sw": w_uq_sw.astype(BF16),
        "g_kv": g_kv.reshape(1, -1), "w_k": w_k.astype(BF16), "w_v": w_v.astype(BF16),
        "w_out": w_out.astype(BF16), "w_router_t": w_router.T, "router_bias": router_bias.reshape(-1, 1),
        "ws_gate": ws_gate.astype(BF16), "ws_up": ws_up.astype(BF16), "ws_down": ws_down.astype(BF16),
    }


def kernel(x, c, positions, w_ada, b_ada, w_in, w_pool, pool_scale, w_spatial, b_spatial, g_q, w_uq, g_kv, w_ukv, w_out, w_router, router_bias, w_gate, w_up, w_down, ws_gate, ws_up, ws_down, final_gain):
    batch, seq, d = x.shape
    depth = w_ada.shape[0]
    mod = _ada_mod(c, w_ada, b_ada)
    cos_t, sin_t = _rope_tables(positions)
    xt = x.reshape(batch * seq, d)
    fg = final_gain.reshape(1, d)
    wg_b, wu_b, wd_b = w_gate.astype(BF16), w_up.astype(BF16), w_down.astype(BF16)
    for l in range(depth):
        p = _prep_layer(w_in[l], w_pool[l], pool_scale[l], w_spatial[l], b_spatial[l], g_q[l], w_uq[l],
                        g_kv[l], w_ukv[l], w_out[l], w_router[l], router_bias[l], ws_gate[l], ws_up[l],
                        ws_down[l])
        shift1, scale1, gate1, shift2, scale2, gate2 = [
            mod[l, :, k * d:(k + 1) * d].reshape(batch, 1, d) for k in range(6)]
        mix, q, k, v = _mixer_in(xt, shift1, scale1, cos_t, sin_t, p, seq)
        att = _attention(q, k, v, batch, seq)
        base, h2rows, cw_t = _mixer_out(mix, att, xt, gate1, shift2, scale2, gate2, p, seq)
        tok, wts, tbl = _plan(cw_t, seq)
        n_b = _max_blocks(seq)
        tok = tok.reshape(-1, 1, EXPERT_BLOCK)
        wts = wts.reshape(-1, 1, EXPERT_BLOCK)
        xt = _moe(h2rows, base, gate2, tok, wts, tbl[:, 0, :n_b], tbl[:, 1, :n_b], tbl[:, 2, 0],
                  wg_b, wu_b, wd_b, l, fg, seq, final_norm=(l == depth - 1))
    return xt.reshape(batch, seq, d)
```

```python
import functools

import jax
import jax.numpy as jnp
from jax import lax
from jax.experimental import pallas as pl
from jax.experimental.pallas import tpu as pltpu

F32 = jnp.float32
BF16 = jnp.bfloat16

EPS = 1e-6
LANES = 128
POOL_WINDOWS = (2, 4, 8, 16)
POOL_WIDTH = 256
POOL_CH = 64
MAX_WINDOW = 16
SG_HEADS = 4
SG_WIDTH = 256
SG_DIM = 64
SG_CHUNK = 128
MLA_HEADS = 8
V_DIM = 64
QK_NOPE = 64
QK_ROPE = 32
HALF_ROPE = QK_ROPE // 2
Q_LORA = 256
KV_LORA = 128
ROPE_THETA = 10000.0
ATTN_SCALE = (QK_NOPE + QK_ROPE) ** -0.5
HEAD_PAD = 128
N_EXPERTS = 64
TOP_K = 8
N_EXPERT_GROUPS = 8
GROUP_SIZE = N_EXPERTS // N_EXPERT_GROUPS
TOPK_GROUPS = 4
ROUTED_SCALE = 2.5
VMEM_LIMIT = 52 * 1024 * 1024


def _dot(a, b):
    return jnp.dot(a, b, preferred_element_type=F32)


def _dot_nt(a, b):
    return lax.dot_general(a, b, (((1,), (1,)), ((), ())), preferred_element_type=F32)


def _rms(x):
    return x * lax.rsqrt(jnp.mean(x * x, axis=-1, keepdims=True) + EPS)


def _gelu_tanh(x):
    c = (2.0 / jnp.pi) ** 0.5
    return x * (0.5 * (1.0 + jnp.tanh(c * (x + 0.044715 * (x * x * x)))))


def _silu(x):
    return x * jax.nn.sigmoid(x)


def _ada_kernel(c_ref, w_ref, b_ref, o_ref):
    cond = _silu(c_ref[...])
    o_ref[0] = _dot(cond.astype(BF16), w_ref[0].astype(BF16)) + b_ref[0]


def _ada_mod(c, w_ada, b_ada):
    depth, d, n = w_ada.shape
    b = c.shape[0]
    nt = 1536
    return pl.pallas_call(
        _ada_kernel,
        grid=(depth, n // nt),
        in_specs=[pl.BlockSpec((b, d), lambda l, j: (0, 0)),
                  pl.BlockSpec((1, d, nt), lambda l, j: (l, 0, j)),
                  pl.BlockSpec((1, 1, nt), lambda l, j: (l, 0, j))],
        out_specs=pl.BlockSpec((1, b, nt), lambda l, j: (l, 0, j)),
        out_shape=jax.ShapeDtypeStruct((depth, b, n), F32),
        compiler_params=pltpu.CompilerParams(vmem_limit_bytes=VMEM_LIMIT),
        name="ada_mod",
    )(c, w_ada, b_ada.reshape(depth, 1, n))


def _rope_kernel(pos_ref, invf_ref, sign_ref, cos_ref, sin_ref):
    ang = pos_ref[...].astype(F32) * invf_ref[...]
    cos_ref[...] = jnp.cos(ang)
    sin_ref[...] = jnp.sin(ang) * sign_ref[...]


def _rope_tables(positions):
    t = positions.size
    tm = 2048
    inv_freq = ROPE_THETA ** (-jnp.arange(0, QK_ROPE, 2, dtype=F32) / QK_ROPE)
    invf = jnp.zeros((1, HEAD_PAD), F32)
    invf = invf.at[0, QK_NOPE:QK_NOPE + HALF_ROPE].set(inv_freq)
    invf = invf.at[0, QK_NOPE + HALF_ROPE:QK_NOPE + QK_ROPE].set(inv_freq)
    sign = jnp.zeros((1, HEAD_PAD), F32)
    sign = sign.at[0, QK_NOPE:QK_NOPE + HALF_ROPE].set(-1.0)
    sign = sign.at[0, QK_NOPE + HALF_ROPE:QK_NOPE + QK_ROPE].set(1.0)
    return pl.pallas_call(
        _rope_kernel,
        grid=(t // tm,),
        in_specs=[pl.BlockSpec((tm, 1), lambda i: (i, 0)),
                  pl.BlockSpec((1, HEAD_PAD), lambda i: (0, 0)),
                  pl.BlockSpec((1, HEAD_PAD), lambda i: (0, 0))],
        out_specs=[pl.BlockSpec((tm, HEAD_PAD), lambda i: (i, 0)),
                   pl.BlockSpec((tm, HEAD_PAD), lambda i: (i, 0))],
        out_shape=[jax.ShapeDtypeStruct((t, HEAD_PAD), F32)] * 2,
        name="rope_tables",
    )(positions.reshape(t, 1), invf, sign)


def _mixer_in_kernel(x_ref, shift_ref, scale_ref, win_ref, wpool_ref, pscale_ref, wsp_ref, bsp_ref,
                     gq_ref, wuq_ref, wuqs_ref, gkv_ref, wk_ref, wv_ref, cos_ref, sin_ref,
                     mix_ref, q_ref, k_ref, v_ref, carry_ref, *, tiles_per_seq):
    tm = x_ref.shape[0]
    ti = pl.program_id(0) % tiles_per_seq
    h = _rms(x_ref[...]) * (1.0 + scale_ref[0]) + shift_ref[0]
    z = _dot(h.astype(BF16), win_ref[...])

    a = z[:, 0:POOL_WIDTH]

    @pl.when(ti == 0)
    def _():
        carry_ref[...] = jnp.zeros_like(carry_ref)

    ext = jnp.concatenate([carry_ref[...], a], axis=0)
    carry_ref[...] = a[tm - MAX_WINDOW:, :]
    p1 = ext + pltpu.roll(ext, 1, 0)
    p2 = p1 + pltpu.roll(p1, 2, 0)
    p3 = p2 + pltpu.roll(p2, 4, 0)
    p4 = p3 + pltpu.roll(p3, 8, 0)
    lane = lax.broadcasted_iota(jnp.int32, (tm, POOL_WIDTH), 1)
    row = lax.broadcasted_iota(jnp.int32, (tm, POOL_WIDTH), 0) + (ti * tm + 1)
    g0, g1, g2 = lane < POOL_CH, lane < 2 * POOL_CH, lane < 3 * POOL_CH
    wsum = jnp.where(g0, p1[MAX_WINDOW:], jnp.where(g1, p2[MAX_WINDOW:],
                     jnp.where(g2, p3[MAX_WINDOW:], p4[MAX_WINDOW:])))
    width = jnp.where(g0, POOL_WINDOWS[0], jnp.where(g1, POOL_WINDOWS[1],
                      jnp.where(g2, POOL_WINDOWS[2], POOL_WINDOWS[3])))
    cnt = jnp.minimum(row, width).astype(F32)
    dlt = wsum / cnt - a
    y_pool = _dot(dlt.astype(BF16), wpool_ref[...]) * pscale_ref[...]
    mix_ref[:, 0:POOL_WIDTH] = y_pool.astype(mix_ref.dtype)

    ug = _gelu_tanh(z[:, POOL_WIDTH:POOL_WIDTH + SG_WIDTH])
    vg = _gelu_tanh(z[:, POOL_WIDTH + SG_WIDTH:POOL_WIDTH + 2 * SG_WIDTH])
    mu = jnp.mean(vg, axis=-1, keepdims=True)
    vc = vg - mu
    var = jnp.mean(vc * vc, axis=-1, keepdims=True)
    vn = (vc * lax.rsqrt(var + EPS)).astype(BF16)
    r_i = lax.broadcasted_iota(jnp.int32, (SG_CHUNK, SG_CHUNK), 0)
    c_i = lax.broadcasted_iota(jnp.int32, (SG_CHUNK, SG_CHUNK), 1)
    wms = [jnp.where(c_i <= r_i, wsp_ref[hh], 0.0).astype(BF16) for hh in range(SG_HEADS)]
    lane_head = lax.broadcasted_iota(jnp.int32, (SG_CHUNK, SG_WIDTH), 1) // SG_DIM
    for cidx in range(tm // SG_CHUNK):
        rows = slice(cidx * SG_CHUNK, (cidx + 1) * SG_CHUNK)
        vchunk = vn[rows]
        mixed = bsp_ref[...]
        for hh in range(SG_HEADS):
            mixed = mixed + jnp.where(lane_head == hh, _dot(wms[hh], vchunk), 0.0)
        mix_ref[rows, POOL_WIDTH:POOL_WIDTH + SG_WIDTH] = (ug[rows] * mixed).astype(mix_ref.dtype)

    o_cq = POOL_WIDTH + 2 * SG_WIDTH
    o_ckv = o_cq + Q_LORA
    o_kpe = o_ckv + KV_LORA
    cos = cos_ref[...]
    sin = sin_ref[...]
    cqn = (_rms(z[:, o_cq:o_ckv]) * gq_ref[...]).astype(BF16)
    q = _dot(cqn, wuq_ref[...])
    qs = _dot(cqn, wuqs_ref[...])
    ckvn = (_rms(z[:, o_ckv:o_kpe]) * gkv_ref[...]).astype(BF16)
    kn = _dot(ckvn, wk_ref[...])
    kpe = z[:, o_kpe:o_kpe + HEAD_PAD] * cos + z[:, o_kpe + HEAD_PAD:o_kpe + 2 * HEAD_PAD] * sin
    for hh in range(MLA_HEADS):
        blk = slice(hh * HEAD_PAD, (hh + 1) * HEAD_PAD)
        q_ref[:, blk] = ((q[:, blk] * cos + qs[:, blk] * sin) * ATTN_SCALE).astype(q_ref.dtype)
        k_ref[:, blk] = (kn[:, blk] + kpe).astype(k_ref.dtype)
    v_ref[...] = _dot(ckvn, wv_ref[...]).astype(v_ref.dtype)


def _mixer_in(x2d, shift, scale, cos_t, sin_t, p, seq):
    t, d = x2d.shape
    tm = 512
    tiles_per_seq = seq // tm
    nz = p["w_in"].shape[1]
    hq = MLA_HEADS * HEAD_PAD

    def full(shape):
        return pl.BlockSpec(shape, lambda i: (0,) * len(shape))

    def mod():
        return pl.BlockSpec((1, 1, d), lambda i: (i // tiles_per_seq, 0, 0))

    return pl.pallas_call(
        functools.partial(_mixer_in_kernel, tiles_per_seq=tiles_per_seq),
        grid=(t // tm,),
        in_specs=[pl.BlockSpec((tm, d), lambda i: (i, 0)), mod(), mod(),
                  full((d, nz)), full((POOL_WIDTH, POOL_WIDTH)), full((1, POOL_WIDTH)),
                  full((SG_HEADS, SG_CHUNK, SG_CHUNK)), full((SG_CHUNK, SG_WIDTH)),
                  full((1, Q_LORA)), full((Q_LORA, hq)), full((Q_LORA, hq)),
                  full((1, KV_LORA)), full((KV_LORA, hq)), full((KV_LORA, MLA_HEADS * V_DIM)),
                  pl.BlockSpec((tm, HEAD_PAD), lambda i: (i, 0)),
                  pl.BlockSpec((tm, HEAD_PAD), lambda i: (i, 0))],
        out_specs=[pl.BlockSpec((tm, POOL_WIDTH + SG_WIDTH), lambda i: (i, 0)),
                   pl.BlockSpec((tm, hq), lambda i: (i, 0)),
                   pl.BlockSpec((tm, hq), lambda i: (i, 0)),
                   pl.BlockSpec((tm, MLA_HEADS * V_DIM), lambda i: (i, 0))],
        out_shape=[jax.ShapeDtypeStruct((t, POOL_WIDTH + SG_WIDTH), BF16),
                   jax.ShapeDtypeStruct((t, hq), BF16),
                   jax.ShapeDtypeStruct((t, hq), BF16),
                   jax.ShapeDtypeStruct((t, MLA_HEADS * V_DIM), BF16)],
        scratch_shapes=[pltpu.VMEM((MAX_WINDOW, POOL_WIDTH), F32)],
        compiler_params=pltpu.CompilerParams(dimension_semantics=("arbitrary",),
                                             vmem_limit_bytes=VMEM_LIMIT),
        name="mixer_in",
    )(x2d, shift, scale, p["w_in"], p["w_pool"], p["pool_scale"], p["w_spatial"], p["b_spatial"],
      p["g_q"], p["w_uq"], p["w_uq_sw"], p["g_kv"], p["w_k"], p["w_v"], cos_t, sin_t)


ATTN_TQ = 256


def _attn_kernel(q_ref, k_ref, v_ref, o_ref):
    seq = q_ref.shape[0]
    tq = ATTN_TQ
    r_i = lax.broadcasted_iota(jnp.int32, (tq, tq), 0)
    c_i = lax.broadcasted_iota(jnp.int32, (tq, tq), 1)
    lane = lax.broadcasted_iota(jnp.int32, (tq, 2 * V_DIM), 1)
    for i in range(seq // tq):
        lo, hi = i * tq, (i + 1) * tq
        outs = []
        for hh in range(2):
            blk = slice(hh * HEAD_PAD, (hh + 1) * HEAD_PAD)
            q = q_ref[lo:hi, blk]
            s_d = jnp.where(c_i <= r_i, _dot_nt(q, k_ref[lo:hi, blk]), -jnp.inf)
            m = jnp.max(s_d, axis=-1, keepdims=True)
            if i > 0:
                s_o = _dot_nt(q, k_ref[0:lo, blk])
                m = jnp.maximum(m, jnp.max(s_o, axis=-1, keepdims=True))
                p_o = jnp.exp(s_o - m)
            p_d = jnp.exp(s_d - m)
            l = jnp.sum(p_d, axis=-1, keepdims=True)
            acc = _dot(p_d.astype(BF16), v_ref[lo:hi, :])
            if i > 0:
                l = l + jnp.sum(p_o, axis=-1, keepdims=True)
                acc = acc + _dot(p_o.astype(BF16), v_ref[0:lo, :])
            outs.append(acc * (1.0 / l))
        o_ref[lo:hi, :] = jnp.where(lane < V_DIM, outs[0], outs[1]).astype(o_ref.dtype)


def _attention(q, k, v, batch, seq):
    t = q.shape[0]
    return pl.pallas_call(
        _attn_kernel,
        grid=(batch, MLA_HEADS // 2),
        in_specs=[pl.BlockSpec((seq, 2 * HEAD_PAD), lambda b, hp: (b, hp)),
                  pl.BlockSpec((seq, 2 * HEAD_PAD), lambda b, hp: (b, hp)),
                  pl.BlockSpec((seq, 2 * V_DIM), lambda b, hp: (b, hp))],
        out_specs=pl.BlockSpec((seq, 2 * V_DIM), lambda b, hp: (b, hp)),
        out_shape=jax.ShapeDtypeStruct((t, MLA_HEADS * V_DIM), BF16),
        compiler_params=pltpu.CompilerParams(
            dimension_semantics=("arbitrary", "arbitrary"), vmem_limit_bytes=VMEM_LIMIT),
        name="attention",
    )(q, k, v)


def _first_max_index(cur, idx, sentinel):
    m = jnp.max(cur, axis=0, keepdims=True)
    first = jnp.min(jnp.where(cur == m, idx, sentinel), axis=0, keepdims=True)
    return m, first


def _route(logits_t, bias_t):
    n_tok = logits_t.shape[1]
    scores = jax.nn.sigmoid(logits_t)
    sel = scores + bias_t
    neg = -jnp.inf
    sub = lax.broadcasted_iota(jnp.int32, (GROUP_SIZE, n_tok), 0).astype(F32)
    gid = lax.broadcasted_iota(jnp.int32, (N_EXPERT_GROUPS, n_tok), 0).astype(F32)
    gscore = jnp.zeros((N_EXPERT_GROUPS, n_tok), F32)
    for g in range(N_EXPERT_GROUPS):
        s = sel[g * GROUP_SIZE:(g + 1) * GROUP_SIZE]
        m1, i1 = _first_max_index(s, sub, float(GROUP_SIZE))
        m2 = jnp.max(jnp.where(sub == i1, neg, s), axis=0, keepdims=True)
        gscore = jnp.where(gid == float(g), m1 + m2, gscore)
    eid = lax.broadcasted_iota(jnp.int32, (N_EXPERTS, n_tok), 0).astype(F32)
    egroup = jnp.floor(eid * (1.0 / GROUP_SIZE))
    allowed = jnp.zeros((N_EXPERTS, n_tok), F32)
    cur = gscore
    for _ in range(TOPK_GROUPS):
        _, gi = _first_max_index(cur, gid, float(N_EXPERT_GROUPS))
        cur = jnp.where(gid == gi, neg, cur)
        allowed = jnp.where(egroup == gi, 1.0, allowed)
    cur = jnp.where(allowed > 0.0, sel, neg)
    chosen = jnp.zeros((N_EXPERTS, n_tok), F32)
    for _ in range(TOP_K):
        _, ei = _first_max_index(cur, eid, float(N_EXPERTS))
        hit = eid == ei
        cur = jnp.where(hit, neg, cur)
        chosen = jnp.where(hit, 1.0, chosen)
    w = jnp.where(chosen > 0.0, scores, 0.0)
    return w / jnp.sum(w, axis=0, keepdims=True) * ROUTED_SCALE


def _split_bf16(x):
    hi = x.astype(BF16)
    lo = (x - hi.astype(F32)).astype(BF16)
    return hi, lo


ROW_TILE = 8


def _mixer_out_kernel(mix_ref, att_ref, x_ref, gate1_ref, shift_ref, scale_ref, gate2_ref, wo_ref, wr_ref,
                      rb_ref, sg_ref, su_ref, sd_ref, base_ref, h2r_ref, cw_ref):
    tm = x_ref.shape[0]
    half = mix_ref.shape[1]
    y = _dot(mix_ref[...], wo_ref[0:half, :]) + _dot(att_ref[...], wo_ref[half:, :])
    x2 = x_ref[...] + gate1_ref[0] * y
    h2 = _rms(x2) * (1.0 + scale_ref[0]) + shift_ref[0]
    for j in range(ROW_TILE):
        h2r_ref[pl.ds(j, tm, stride=ROW_TILE), :] = h2[:, j * LANES:(j + 1) * LANES]
    hb = h2.astype(BF16)
    act = _silu(_dot(hb, sg_ref[...])) * _dot(hb, su_ref[...])
    base_ref[...] = x2 + gate2_ref[0] * _dot(act.astype(BF16), sd_ref[...])
    h_hi, h_lo = _split_bf16(h2)
    w_hi, w_lo = _split_bf16(wr_ref[...])
    logits_t = _dot_nt(w_hi, h_hi) + (_dot_nt(w_hi, h_lo) + _dot_nt(w_lo, h_hi))
    cw_ref[...] = _route(logits_t, rb_ref[...])


def _mixer_out(mix, att, x2d, gate1, shift, scale, gate2, p, seq):
    t, d = x2d.shape
    tm = 512
    tiles_per_seq = seq // tm
    ff = p["ws_gate"].shape[1]

    def full(shape):
        return pl.BlockSpec(shape, lambda i: (0,) * len(shape))

    def mod():
        return pl.BlockSpec((1, 1, d), lambda i: (i // tiles_per_seq, 0, 0))

    return pl.pallas_call(
        _mixer_out_kernel,
        grid=(t // tm,),
        in_specs=[pl.BlockSpec((tm, mix.shape[1]), lambda i: (i, 0)),
                  pl.BlockSpec((tm, att.shape[1]), lambda i: (i, 0)),
                  pl.BlockSpec((tm, d), lambda i: (i, 0)), mod(), mod(), mod(), mod(),
                  full((d, d)), full((N_EXPERTS, d)), full((N_EXPERTS, 1)),
                  full((d, ff)), full((d, ff)), full((ff, d))],
        out_specs=[pl.BlockSpec((tm, d), lambda i: (i, 0)),
                   pl.BlockSpec((tm * ROW_TILE, LANES), lambda i: (i, 0)),
                   pl.BlockSpec((N_EXPERTS, tm), lambda i: (0, i))],
        out_shape=[jax.ShapeDtypeStruct((t, d), F32),
                   jax.ShapeDtypeStruct((t * ROW_TILE, LANES), F32),
                   jax.ShapeDtypeStruct((N_EXPERTS, t), F32)],
        compiler_params=pltpu.CompilerParams(dimension_semantics=("arbitrary",),
                                             vmem_limit_bytes=VMEM_LIMIT),
        name="mixer_out",
    )(mix, att, x2d, gate1, shift, scale, gate2, p["w_out"], p["w_router_t"], p["router_bias"],
      p["ws_gate"], p["ws_up"], p["ws_down"])


EXPERT_BLOCK = 128
TOK_BITS = 12


def _plan_kernel(cw_ref, tok_ref, w_ref, tbl_ref):
    n_e, ts = cw_ref.shape
    cw = cw_ref[...]
    chosen = cw > 0.0
    cf = jnp.where(chosen, 1.0, 0.0).astype(BF16)
    r_i = lax.broadcasted_iota(jnp.int32, (LANES, LANES), 0)
    c_i = lax.broadcasted_iota(jnp.int32, (LANES, LANES), 1)
    before = jnp.where(r_i < c_i, 1.0, 0.0).astype(BF16)
    ones = jnp.ones((LANES, LANES), BF16)
    carry = jnp.zeros((n_e, LANES), F32)
    ranks = []
    for k in range(ts // LANES):
        ck = cf[:, k * LANES:(k + 1) * LANES]
        ranks.append(_dot(ck, before) + carry)
        carry = carry + _dot(ck, ones)
    rank = jnp.concatenate(ranks, axis=1).astype(jnp.int32)
    lane = lax.broadcasted_iota(jnp.int32, (n_e, ts), 1)
    packed = jnp.where(chosen, ((lane - rank) << TOK_BITS) | lane, -1)
    w = jnp.where(chosen, cw, 0.0)
    for bit in range(ts.bit_length() - 1):
        step = 1 << bit
        src_p = pltpu.roll(packed, ts - step, 1)
        src_w = pltpu.roll(w, ts - step, 1)
        take = (src_p >= 0) & (lane < ts - step) & (((src_p >> (TOK_BITS + bit)) & 1) == 1)
        keep = (packed >= 0) & (((packed >> (TOK_BITS + bit)) & 1) == 0)
        packed = jnp.where(take, src_p, jnp.where(keep, packed, -1))
        w = jnp.where(take, src_w, jnp.where(keep, w, 0.0))
    tok_ref[0] = jnp.where(packed >= 0, packed & ((1 << TOK_BITS) - 1), ts) * ROW_TILE
    w_ref[0] = w

    nb_lanes = tbl_ref.shape[2]
    nblk = jnp.floor((carry + (EXPERT_BLOCK - 1.0)) * (1.0 / EXPERT_BLOCK))
    e_r = lax.broadcasted_iota(jnp.int32, (n_e, n_e), 0)
    e_c = lax.broadcasted_iota(jnp.int32, (n_e, n_e), 1)
    upto = jnp.where(e_c <= e_r, 1.0, 0.0).astype(BF16)
    cum = _dot(upto, nblk.astype(BF16))[:, 0:1]
    total = jnp.max(cum, axis=0, keepdims=True)
    bidx = lax.broadcasted_iota(jnp.int32, (n_e, nb_lanes), 1).astype(F32)
    done = cum <= bidx
    e_of_b = jnp.sum(jnp.where(done, 1.0, 0.0), axis=0, keepdims=True)
    part = bidx[0:1] - jnp.sum(jnp.where(done, nblk[:, 0:1], 0.0), axis=0, keepdims=True)
    e_last = jnp.sum(jnp.where(cum < total, 1.0, 0.0), axis=0, keepdims=True)
    valid = bidx[0:1] < total
    e_of_b = jnp.where(valid, e_of_b, e_last)
    part = jnp.where(valid, part, 0.0)
    row = lax.broadcasted_iota(jnp.int32, (ROW_TILE, nb_lanes), 0)
    tbl = jnp.where(row == 0, e_of_b, jnp.where(row == 1, part, total))
    tbl_ref[0] = tbl.astype(jnp.int32)


def _plan(cw_t, seq):
    n_e, t = cw_t.shape
    n_seq = t // seq
    nb_lanes = 2 * LANES
    assert seq <= (1 << TOK_BITS) and _max_blocks(seq) <= nb_lanes
    return pl.pallas_call(
        _plan_kernel,
        grid=(n_seq,),
        in_specs=[pl.BlockSpec((n_e, seq), lambda s: (0, s))],
        out_specs=[pl.BlockSpec((1, n_e, seq), lambda s: (s, 0, 0)),
                   pl.BlockSpec((1, n_e, seq), lambda s: (s, 0, 0)),
                   pl.BlockSpec((1, ROW_TILE, nb_lanes), lambda s: (s, 0, 0))],
        out_shape=[jax.ShapeDtypeStruct((n_seq, n_e, seq), jnp.int32),
                   jax.ShapeDtypeStruct((n_seq, n_e, seq), F32),
                   jax.ShapeDtypeStruct((n_seq, ROW_TILE, nb_lanes), jnp.int32)],
        compiler_params=pltpu.CompilerParams(dimension_semantics=("arbitrary",),
                                             vmem_limit_bytes=VMEM_LIMIT),
        name="plan",
    )(cw_t)


def _max_blocks(seq):
    return -(-(seq * TOP_K + N_EXPERTS * (EXPERT_BLOCK - 1)) // EXPERT_BLOCK)


XT_PITCH = EXPERT_BLOCK + 1
RMW_BATCH = 8


def _moe_stage(tok_a_ref, tok_c_ref, w_b_ref, h_ref, wg_ref, wu_ref, wd_ref, acc_ref,
               xt_a, xt_b, yt_b, yt_c, last_row):
    for r in range(EXPERT_BLOCK):
        start = pl.multiple_of(jnp.minimum(tok_a_ref[0, 0, r], last_row), ROW_TILE)
        xt_a[pl.ds(r, ROW_TILE, stride=XT_PITCH), :] = h_ref[pl.ds(start, ROW_TILE), :]
    for grp in range(EXPERT_BLOCK // RMW_BATCH):
        rows = range(grp * RMW_BATCH, (grp + 1) * RMW_BATCH)
        starts = [pl.multiple_of(tok_c_ref[0, 0, r], ROW_TILE) for r in rows]
        olds = [acc_ref[pl.ds(st, ROW_TILE), :] for st in starts]
        news = [old + yt_c[pl.ds(r, ROW_TILE, stride=XT_PITCH), :] for r, old in zip(rows, olds)]
        for st, new in zip(starts, news):
            acc_ref[pl.ds(st, ROW_TILE), :] = new
    x = jnp.concatenate([xt_b[pl.ds(j * XT_PITCH, EXPERT_BLOCK), :] for j in range(ROW_TILE)],
                        axis=1).astype(BF16)
    act = _silu(_dot(x, wg_ref[0, 0])) * _dot(x, wu_ref[0, 0])
    y = _dot(act.astype(BF16), wd_ref[0, 0])
    w_col = jnp.broadcast_to(w_b_ref[0], (EXPERT_BLOCK, EXPERT_BLOCK)).T
    for j in range(ROW_TILE):
        yt_b[pl.ds(j * XT_PITCH, EXPERT_BLOCK), :] = y[:, j * LANES:(j + 1) * LANES] * w_col


def _moe_kernel(be_ref, bp_ref, nb_ref, tok_a_ref, tok_c_ref, w_b_ref, h_ref, base_ref, gate_ref,
                wg_ref, wu_ref, wd_ref, fg_ref, o_ref, acc_ref, xt0, xt1, yt0, yt1, *, final_norm):
    s = pl.program_id(0)
    g = pl.program_id(1)
    ts = base_ref.shape[0]
    n = nb_ref[s]
    last_row = (ts - 1) * ROW_TILE

    @pl.when(g == 0)
    def _():
        acc_ref[...] = jnp.zeros_like(acc_ref)
        yt0[...] = jnp.zeros_like(yt0)
        yt1[...] = jnp.zeros_like(yt1)
        xt1[...] = jnp.zeros_like(xt1)

    live = g <= n + 1

    @pl.when(live & (g % 2 == 0))
    def _():
        _moe_stage(tok_a_ref, tok_c_ref, w_b_ref, h_ref, wg_ref, wu_ref, wd_ref, acc_ref,
                   xt0, xt1, yt1, yt0, last_row)

    @pl.when(live & (g % 2 == 1))
    def _():
        _moe_stage(tok_a_ref, tok_c_ref, w_b_ref, h_ref, wg_ref, wu_ref, wd_ref, acc_ref,
                   xt1, xt0, yt0, yt1, last_row)

    @pl.when(g == n + 1)
    def _():
        chunk = 256
        for c in range(ts // chunk):
            routed = jnp.concatenate(
                [acc_ref[pl.ds(c * chunk * ROW_TILE + j, chunk, stride=ROW_TILE), :] for j in range(ROW_TILE)],
                axis=1)
            x3 = base_ref[c * chunk:(c + 1) * chunk, :] + gate_ref[0] * routed
            if final_norm:
                x3 = _rms(x3) * fg_ref[...]
            o_ref[c * chunk:(c + 1) * chunk, :] = x3


def _moe(h2rows, base, gate, tok, wts, blk_e, blk_p, n_blk, wg, wu, wd, layer, final_gain, seq, final_norm):
    t, d = base.shape
    n_seq = t // seq
    ff = wg.shape[3]
    parts = seq // EXPERT_BLOCK
    n_b = _max_blocks(seq)
    assert d == ROW_TILE * LANES

    dummy_row = tok.shape[0] - 1

    def meta_idx(lag):
        def idx(s, g, be, bp, nb):
            b = g - lag
            bc = jnp.clip(b, 0, n_b - 1)
            row = (s * N_EXPERTS + be[s, bc]) * parts + bp[s, bc]
            return (jnp.where((b >= 0) & (b < nb[s]), row, dummy_row), 0, 0)
        return idx

    def seq_idx(s, g, be, bp, nb):
        return (s, 0)

    def w_idx(s, g, be, bp, nb):
        return (layer, be[s, jnp.clip(g - 1, 0, n_b - 1)], 0, 0)

    once = pl.Buffered(1)
    tile =pltpu.VMEM((ROW_TILE * XT_PITCH, LANES), F32)
    grid_spec = pltpu.PrefetchScalarGridSpec(
        num_scalar_prefetch=3,
        grid=(n_seq, n_b + 2),
        in_specs=[pl.BlockSpec((1, 1, EXPERT_BLOCK), meta_idx(0), memory_space=pltpu.SMEM),
                  pl.BlockSpec((1, 1, EXPERT_BLOCK), meta_idx(2), memory_space=pltpu.SMEM),
                  pl.BlockSpec((1, 1, EXPERT_BLOCK), meta_idx(1)),
                  pl.BlockSpec((seq * ROW_TILE, LANES), seq_idx, pipeline_mode=once),
                  pl.BlockSpec((seq, d), seq_idx, pipeline_mode=once),
                  pl.BlockSpec((1, 1, d), lambda s, g, be, bp, nb: (s, 0, 0)),
                  pl.BlockSpec((1, 1, d, ff), w_idx),
                  pl.BlockSpec((1, 1, d, ff), w_idx),
                  pl.BlockSpec((1, 1, ff, d), w_idx),
                  pl.BlockSpec((1, d), lambda s, g, be, bp, nb: (0, 0))],
        out_specs=pl.BlockSpec((seq, d), seq_idx, pipeline_mode=once),
        scratch_shapes=[pltpu.VMEM(((seq + 1) * ROW_TILE, LANES), F32), tile, tile, tile, tile],
    )
    return pl.pallas_call(
        functools.partial(_moe_kernel, final_norm=final_norm),
        grid_spec=grid_spec,
        out_shape=jax.ShapeDtypeStruct((t, d), F32),
        compiler_params=pltpu.CompilerParams(dimension_semantics=("arbitrary", "arbitrary"),
                                             vmem_limit_bytes=VMEM_LIMIT),
        name="moe",
    )(blk_e, blk_p, n_blk, tok, tok, wts, h2rows, base, gate, wg, wu, wd, final_gain)


def _prep_layer(w_in, w_pool, pool_scale, w_spatial, b_spatial, g_q, w_uq, g_kv, w_ukv, w_out,
                w_router, router_bias, ws_gate, ws_up, ws_down):
    d = w_in.shape[0]
    o_kpe = POOL_WIDTH + 2 * SG_WIDTH + Q_LORA + KV_LORA
    x1 = w_in[:, o_kpe:o_kpe + HALF_ROPE]
    x2 = w_in[:, o_kpe + HALF_ROPE:o_kpe + QK_ROPE]
    zl = jnp.zeros((d, QK_NOPE), F32)
    zr = jnp.zeros((d, HEAD_PAD - QK_NOPE - QK_ROPE), F32)
    w_in_pad = jnp.concatenate([w_in[:, :o_kpe], zl, x1, x2, zr, zl, x2, x1, zr], axis=1)

    uq = w_uq.reshape(Q_LORA, MLA_HEADS, QK_NOPE + QK_ROPE)
    q1 = uq[..., QK_NOPE:QK_NOPE + HALF_ROPE]
    q2 = uq[..., QK_NOPE + HALF_ROPE:]
    zq = jnp.zeros((Q_LORA, MLA_HEADS, HEAD_PAD - QK_NOPE - QK_ROPE), F32)
    w_uq_pad = jnp.concatenate([uq, zq], axis=-1).reshape(Q_LORA, MLA_HEADS * HEAD_PAD)
    w_uq_sw = jnp.concatenate([jnp.zeros_like(uq[..., :QK_NOPE]), q2, q1, zq], axis=-1)
    w_uq_sw = w_uq_sw.reshape(Q_LORA, MLA_HEADS * HEAD_PAD)

    ukv = w_ukv.reshape(KV_LORA, MLA_HEADS, QK_NOPE + V_DIM)
    zk = jnp.zeros((KV_LORA, MLA_HEADS, HEAD_PAD - QK_NOPE), F32)
    w_k = jnp.concatenate([ukv[..., :QK_NOPE], zk], axis=-1).reshape(KV_LORA, MLA_HEADS * HEAD_PAD)
    w_v = ukv[..., QK_NOPE:].reshape(KV_LORA, MLA_HEADS * V_DIM)

    w_pool_bd = jax.scipy.linalg.block_diag(*[w_pool[g] for g in range(len(POOL_WINDOWS))])
    b_sp = jnp.repeat(b_spatial.T, SG_DIM, axis=1)
    return {
        "w_in": w_in_pad.astype(BF16), "w_pool": w_pool_bd.astype(BF16),
        "pool_scale": pool_scale.reshape(1, -1), "w_spatial": w_spatial, "b_spatial": b_sp,
        "g_q": g_q.reshape(1, -1), "w_uq": w_uq_pad.astype(BF16), "w_uq_sw": w_uq_sw.astype(BF16),
        "g_kv": g_kv.reshape(1, -1), "w_k": w_k.astype(BF16), "w_v": w_v.astype(BF16),
        "w_out": w_out.astype(BF16), "w_router_t": w_router.T, "router_bias": router_bias.reshape(-1, 1),
        "ws_gate": ws_gate.astype(BF16), "ws_up": ws_up.astype(BF16), "ws_down": ws_down.astype(BF16),
    }


def kernel(x, c, positions, w_ada, b_ada, w_in, w_pool, pool_scale, w_spatial, b_spatial, g_q, w_uq, g_kv, w_ukv, w_out, w_router, router_bias, w_gate, w_up, w_down, ws_gate, ws_up, ws_down, final_gain):
    batch, seq, d = x.shape
    depth = w_ada.shape[0]
    mod = _ada_mod(c, w_ada, b_ada)
    cos_t, sin_t = _rope_tables(positions)
    xt = x.reshape(batch * seq, d)
    fg = final_gain.reshape(1, d)
    wg_b, wu_b, wd_b = w_gate.astype(BF16), w_up.astype(BF16), w_down.astype(BF16)
    for l in range(depth):
        p = _prep_layer(w_in[l], w_pool[l], pool_scale[l], w_spatial[l], b_spatial[l], g_q[l], w_uq[l],
                        g_kv[l], w_ukv[l], w_out[l], w_router[l], router_bias[l], ws_gate[l], ws_up[l],
                        ws_down[l])
        shift1, scale1, gate1, shift2, scale2, gate2 = [
            mod[l, :, k * d:(k + 1) * d].reshape(batch, 1, d) for k in range(6)]
        mix, q, k, v = _mixer_in(xt, shift1, scale1, cos_t, sin_t, p, seq)
        att = _attention(q, k, v, batch, seq)
        base, h2rows, cw_t = _mixer_out(mix, att, xt, gate1, shift2, scale2, gate2, p, seq)
        tok, wts, tbl = _plan(cw_t, seq)
        n_b = _max_blocks(seq)
        tok = jnp.concatenate([tok.reshape(-1, 1, EXPERT_BLOCK),
                               jnp.full((1, 1, EXPERT_BLOCK), seq * ROW_TILE, jnp.int32)], axis=0)
        wts = jnp.concatenate([wts.reshape(-1, 1, EXPERT_BLOCK), jnp.zeros((1, 1, EXPERT_BLOCK), F32)], axis=0)
        xt = _moe(h2rows, base, gate2, tok, wts, tbl[:, 0, :n_b], tbl[:, 1, :n_b], tbl[:, 2, 0],
                  wg_b, wu_b, wd_b, l, fg, seq, final_norm=(l == depth - 1))
    return xt.reshape(batch, seq, d)
```

```python
import functools

import jax
import jax.numpy as jnp
from jax import lax
from jax.experimental import pallas as pl
from jax.experimental.pallas import tpu as pltpu

F32 = jnp.float32
BF16 = jnp.bfloat16

EPS = 1e-6
LANES = 128
POOL_WINDOWS = (2, 4, 8, 16)
POOL_WIDTH = 256
POOL_CH = 64
MAX_WINDOW = 16
SG_HEADS = 4
SG_WIDTH = 256
SG_DIM = 64
SG_CHUNK = 128
MLA_HEADS = 8
V_DIM = 64
QK_NOPE = 64
QK_ROPE = 32
HALF_ROPE = QK_ROPE // 2
Q_LORA = 256
KV_LORA = 128
ROPE_THETA = 10000.0
ATTN_SCALE = (QK_NOPE + QK_ROPE) ** -0.5
HEAD_PAD = 128
N_EXPERTS = 64
TOP_K = 8
N_EXPERT_GROUPS = 8
GROUP_SIZE = N_EXPERTS // N_EXPERT_GROUPS
TOPK_GROUPS = 4
ROUTED_SCALE = 2.5
VMEM_LIMIT = 52 * 1024 * 1024


def _dot(a, b):
    return jnp.dot(a, b, preferred_element_type=F32)


def _dot_nt(a, b):
    return lax.dot_general(a, b, (((1,), (1,)), ((), ())), preferred_element_type=F32)


def _rms(x):
    return x * lax.rsqrt(jnp.mean(x * x, axis=-1, keepdims=True) + EPS)


def _gelu_tanh(x):
    c = (2.0 / jnp.pi) ** 0.5
    return x * (0.5 * (1.0 + jnp.tanh(c * (x + 0.044715 * (x * x * x)))))


def _silu(x):
    return x * jax.nn.sigmoid(x)


def _ada_kernel(c_ref, w_ref, b_ref, o_ref):
    cond = _silu(c_ref[...])
    o_ref[0] = _dot(cond.astype(BF16), w_ref[0].astype(BF16)) + b_ref[0]


def _ada_mod(c, w_ada, b_ada):
    depth, d, n = w_ada.shape
    b = c.shape[0]
    nt = 1536
    return pl.pallas_call(
        _ada_kernel,
        grid=(depth, n // nt),
        in_specs=[pl.BlockSpec((b, d), lambda l, j: (0, 0)),
                  pl.BlockSpec((1, d, nt), lambda l, j: (l, 0, j)),
                  pl.BlockSpec((1, 1, nt), lambda l, j: (l, 0, j))],
        out_specs=pl.BlockSpec((1, b, nt), lambda l, j: (l, 0, j)),
        out_shape=jax.ShapeDtypeStruct((depth, b, n), F32),
        compiler_params=pltpu.CompilerParams(vmem_limit_bytes=VMEM_LIMIT),
        name="ada_mod",
    )(c, w_ada, b_ada.reshape(depth, 1, n))


def _rope_kernel(pos_ref, invf_ref, sign_ref, cos_ref, sin_ref):
    ang = pos_ref[...].astype(F32) * invf_ref[...]
    cos_ref[...] = jnp.cos(ang)
    sin_ref[...] = jnp.sin(ang) * sign_ref[...]


def _rope_tables(positions):
    t = positions.size
    tm = 2048
    inv_freq = ROPE_THETA ** (-jnp.arange(0, QK_ROPE, 2, dtype=F32) / QK_ROPE)
    invf = jnp.zeros((1, HEAD_PAD), F32)
    invf = invf.at[0, QK_NOPE:QK_NOPE + HALF_ROPE].set(inv_freq)
    invf = invf.at[0, QK_NOPE + HALF_ROPE:QK_NOPE + QK_ROPE].set(inv_freq)
    sign = jnp.zeros((1, HEAD_PAD), F32)
    sign = sign.at[0, QK_NOPE:QK_NOPE + HALF_ROPE].set(-1.0)
    sign = sign.at[0, QK_NOPE + HALF_ROPE:QK_NOPE + QK_ROPE].set(1.0)
    return pl.pallas_call(
        _rope_kernel,
        grid=(t // tm,),
        in_specs=[pl.BlockSpec((tm, 1), lambda i: (i, 0)),
                  pl.BlockSpec((1, HEAD_PAD), lambda i: (0, 0)),
                  pl.BlockSpec((1, HEAD_PAD), lambda i: (0, 0))],
        out_specs=[pl.BlockSpec((tm, HEAD_PAD), lambda i: (i, 0)),
                   pl.BlockSpec((tm, HEAD_PAD), lambda i: (i, 0))],
        out_shape=[jax.ShapeDtypeStruct((t, HEAD_PAD), F32)] * 2,
        name="rope_tables",
    )(positions.reshape(t, 1), invf, sign)


def _mixer_in_kernel(*refs, tiles_per_seq, fused):
    if fused:
        base_ref, r_ref, gprev_ref = refs[:3]
        (shift_ref, scale_ref, win_ref, wpool_ref, pscale_ref, wsp_ref, bsp_ref, gq_ref, wuq_ref, wuqs_ref,
         gkv_ref, wk_ref, wv_ref, cos_ref, sin_ref, mix_ref, q_ref, k_ref, v_ref, x_out_ref, carry_ref) = refs[3:]
        tm = base_ref.shape[0]
        x = base_ref[...] + gprev_ref[0] * jnp.concatenate(
            [r_ref[pl.ds(j, tm, stride=ROW_TILE), :] for j in range(ROW_TILE)], axis=1)
        x_out_ref[...] = x
    else:
        (x_ref, shift_ref, scale_ref, win_ref, wpool_ref, pscale_ref, wsp_ref, bsp_ref, gq_ref, wuq_ref, wuqs_ref,
         gkv_ref, wk_ref, wv_ref, cos_ref, sin_ref, mix_ref, q_ref, k_ref, v_ref, carry_ref) = refs
        tm = x_ref.shape[0]
        x = x_ref[...]
    ti = pl.program_id(0) % tiles_per_seq
    h = _rms(x) * (1.0 + scale_ref[0]) + shift_ref[0]
    z = _dot(h.astype(BF16), win_ref[...])

    a = z[:, 0:POOL_WIDTH]

    @pl.when(ti == 0)
    def _():
        carry_ref[...] = jnp.zeros_like(carry_ref)

    ext = jnp.concatenate([carry_ref[...], a], axis=0)
    carry_ref[...] = a[tm - MAX_WINDOW:, :]
    p1 = ext + pltpu.roll(ext, 1, 0)
    p2 = p1 + pltpu.roll(p1, 2, 0)
    p3 = p2 + pltpu.roll(p2, 4, 0)
    p4 = p3 + pltpu.roll(p3, 8, 0)
    lane = lax.broadcasted_iota(jnp.int32, (tm, POOL_WIDTH), 1)
    row = lax.broadcasted_iota(jnp.int32, (tm, POOL_WIDTH), 0) + (ti * tm + 1)
    g0, g1, g2 = lane < POOL_CH, lane < 2 * POOL_CH, lane < 3 * POOL_CH
    wsum = jnp.where(g0, p1[MAX_WINDOW:], jnp.where(g1, p2[MAX_WINDOW:],
                     jnp.where(g2, p3[MAX_WINDOW:], p4[MAX_WINDOW:])))
    width = jnp.where(g0, POOL_WINDOWS[0], jnp.where(g1, POOL_WINDOWS[1],
                      jnp.where(g2, POOL_WINDOWS[2], POOL_WINDOWS[3])))
    cnt = jnp.minimum(row, width).astype(F32)
    dlt = wsum / cnt - a
    y_pool = _dot(dlt.astype(BF16), wpool_ref[...]) * pscale_ref[...]
    mix_ref[:, 0:POOL_WIDTH] = y_pool.astype(mix_ref.dtype)

    ug = _gelu_tanh(z[:, POOL_WIDTH:POOL_WIDTH + SG_WIDTH])
    vg = _gelu_tanh(z[:, POOL_WIDTH + SG_WIDTH:POOL_WIDTH + 2 * SG_WIDTH])
    mu = jnp.mean(vg, axis=-1, keepdims=True)
    vc = vg - mu
    var = jnp.mean(vc * vc, axis=-1, keepdims=True)
    vn = (vc * lax.rsqrt(var + EPS)).astype(BF16)
    r_i = lax.broadcasted_iota(jnp.int32, (SG_CHUNK, SG_CHUNK), 0)
    c_i = lax.broadcasted_iota(jnp.int32, (SG_CHUNK, SG_CHUNK), 1)
    wms = [jnp.where(c_i <= r_i, wsp_ref[hh], 0.0).astype(BF16) for hh in range(SG_HEADS)]
    lane_head = lax.broadcasted_iota(jnp.int32, (SG_CHUNK, SG_WIDTH), 1) // SG_DIM
    for cidx in range(tm // SG_CHUNK):
        rows = slice(cidx * SG_CHUNK, (cidx + 1) * SG_CHUNK)
        vchunk = vn[rows]
        mixed = bsp_ref[...]
        for hh in range(SG_HEADS):
            mixed = mixed + jnp.where(lane_head == hh, _dot(wms[hh], vchunk), 0.0)
        mix_ref[rows, POOL_WIDTH:POOL_WIDTH + SG_WIDTH] = (ug[rows] * mixed).astype(mix_ref.dtype)

    o_cq = POOL_WIDTH + 2 * SG_WIDTH
    o_ckv = o_cq + Q_LORA
    o_kpe = o_ckv + KV_LORA
    cos = cos_ref[...]
    sin = sin_ref[...]
    cqn = (_rms(z[:, o_cq:o_ckv]) * gq_ref[...]).astype(BF16)
    q = _dot(cqn, wuq_ref[...])
    qs = _dot(cqn, wuqs_ref[...])
    ckvn = (_rms(z[:, o_ckv:o_kpe]) * gkv_ref[...]).astype(BF16)
    kn = _dot(ckvn, wk_ref[...])
    kpe = z[:, o_kpe:o_kpe + HEAD_PAD] * cos + z[:, o_kpe + HEAD_PAD:o_kpe + 2 * HEAD_PAD] * sin
    for hh in range(MLA_HEADS):
        blk = slice(hh * HEAD_PAD, (hh + 1) * HEAD_PAD)
        q_ref[:, blk] = ((q[:, blk] * cos + qs[:, blk] * sin) * ATTN_SCALE).astype(q_ref.dtype)
        k_ref[:, blk] = (kn[:, blk] + kpe).astype(k_ref.dtype)
    v_ref[...] = _dot(ckvn, wv_ref[...]).astype(v_ref.dtype)


def _mixer_in(x_parts, shift, scale, cos_t, sin_t, p, seq):
    fused = len(x_parts) == 3
    t, d = x_parts[0].shape
    tm = 512
    tiles_per_seq = seq // tm
    nz = p["w_in"].shape[1]
    hq = MLA_HEADS * HEAD_PAD

    def full(shape):
        return pl.BlockSpec(shape, lambda i: (0,) * len(shape))

    def mod():
        return pl.BlockSpec((1, 1, d), lambda i: (i // tiles_per_seq, 0, 0))

    x_specs = [pl.BlockSpec((tm, d), lambda i: (i, 0))]
    extra_out_specs, extra_out_shapes = [], []
    if fused:
        x_specs += [pl.BlockSpec((tm * ROW_TILE, LANES), lambda i: (i, 0)), mod()]
        extra_out_specs = [pl.BlockSpec((tm, d), lambda i: (i, 0))]
        extra_out_shapes = [jax.ShapeDtypeStruct((t, d), F32)]
    return pl.pallas_call(
        functools.partial(_mixer_in_kernel, tiles_per_seq=tiles_per_seq, fused=fused),
        grid=(t // tm,),
        in_specs=x_specs + [mod(), mod(),
                  full((d, nz)), full((POOL_WIDTH, POOL_WIDTH)), full((1, POOL_WIDTH)),
                  full((SG_HEADS, SG_CHUNK, SG_CHUNK)), full((SG_CHUNK, SG_WIDTH)),
                  full((1, Q_LORA)), full((Q_LORA, hq)), full((Q_LORA, hq)),
                  full((1, KV_LORA)), full((KV_LORA, hq)), full((KV_LORA, MLA_HEADS * V_DIM)),
                  pl.BlockSpec((tm, HEAD_PAD), lambda i: (i, 0)),
                  pl.BlockSpec((tm, HEAD_PAD), lambda i: (i, 0))],
        out_specs=[pl.BlockSpec((tm, POOL_WIDTH + SG_WIDTH), lambda i: (i, 0)),
                   pl.BlockSpec((tm, hq), lambda i: (i, 0)),
                   pl.BlockSpec((tm, hq), lambda i: (i, 0)),
                   pl.BlockSpec((tm, MLA_HEADS * V_DIM), lambda i: (i, 0))] + extra_out_specs,
        out_shape=[jax.ShapeDtypeStruct((t, POOL_WIDTH + SG_WIDTH), BF16),
                   jax.ShapeDtypeStruct((t, hq), BF16),
                   jax.ShapeDtypeStruct((t, hq), BF16),
                   jax.ShapeDtypeStruct((t, MLA_HEADS * V_DIM), BF16)] + extra_out_shapes,
        scratch_shapes=[pltpu.VMEM((MAX_WINDOW, POOL_WIDTH), F32)],
        compiler_params=pltpu.CompilerParams(dimension_semantics=("arbitrary",),
                                             vmem_limit_bytes=VMEM_LIMIT),
        name="mixer_in",
    )(*x_parts, shift, scale, p["w_in"], p["w_pool"], p["pool_scale"], p["w_spatial"], p["b_spatial"],
      p["g_q"], p["w_uq"], p["w_uq_sw"], p["g_kv"], p["w_k"], p["w_v"], cos_t, sin_t)


ATTN_TQ = 256


def _attn_kernel(q_ref, k_ref, v_ref, o_ref):
    seq = q_ref.shape[0]
    tq = ATTN_TQ
    r_i = lax.broadcasted_iota(jnp.int32, (tq, tq), 0)
    c_i = lax.broadcasted_iota(jnp.int32, (tq, tq), 1)
    lane = lax.broadcasted_iota(jnp.int32, (tq, 2 * V_DIM), 1)
    for i in range(seq // tq):
        lo, hi = i * tq, (i + 1) * tq
        outs = []
        for hh in range(2):
            blk = slice(hh * HEAD_PAD, (hh + 1) * HEAD_PAD)
            q = q_ref[lo:hi, blk]
            s_d = jnp.where(c_i <= r_i, _dot_nt(q, k_ref[lo:hi, blk]), -jnp.inf)
            m = jnp.max(s_d, axis=-1, keepdims=True)
            if i > 0:
                s_o = _dot_nt(q, k_ref[0:lo, blk])
                m = jnp.maximum(m, jnp.max(s_o, axis=-1, keepdims=True))
                p_o = jnp.exp(s_o - m)
            p_d = jnp.exp(s_d - m)
            l = jnp.sum(p_d, axis=-1, keepdims=True)
            acc = _dot(p_d.astype(BF16), v_ref[lo:hi, :])
            if i > 0:
                l = l + jnp.sum(p_o, axis=-1, keepdims=True)
                acc = acc + _dot(p_o.astype(BF16), v_ref[0:lo, :])
            outs.append(acc * (1.0 / l))
        o_ref[lo:hi, :] = jnp.where(lane < V_DIM, outs[0], outs[1]).astype(o_ref.dtype)


def _attention(q, k, v, batch, seq):
    t = q.shape[0]
    return pl.pallas_call(
        _attn_kernel,
        grid=(batch, MLA_HEADS // 2),
        in_specs=[pl.BlockSpec((seq, 2 * HEAD_PAD), lambda b, hp: (b, hp)),
                  pl.BlockSpec((seq, 2 * HEAD_PAD), lambda b, hp: (b, hp)),
                  pl.BlockSpec((seq, 2 * V_DIM), lambda b, hp: (b, hp))],
        out_specs=pl.BlockSpec((seq, 2 * V_DIM), lambda b, hp: (b, hp)),
        out_shape=jax.ShapeDtypeStruct((t, MLA_HEADS * V_DIM), BF16),
        compiler_params=pltpu.CompilerParams(
            dimension_semantics=("arbitrary", "arbitrary"), vmem_limit_bytes=VMEM_LIMIT),
        name="attention",
    )(q, k, v)


def _first_max_index(cur, idx, sentinel):
    m = jnp.max(cur, axis=0, keepdims=True)
    first = jnp.min(jnp.where(cur == m, idx, sentinel), axis=0, keepdims=True)
    return m, first


def _route(logits_t, bias_t):
    n_tok = logits_t.shape[1]
    scores = jax.nn.sigmoid(logits_t)
    sel = scores + bias_t
    neg = -jnp.inf
    sub = lax.broadcasted_iota(jnp.int32, (GROUP_SIZE, n_tok), 0).astype(F32)
    gid = lax.broadcasted_iota(jnp.int32, (N_EXPERT_GROUPS, n_tok), 0).astype(F32)
    gscore = jnp.zeros((N_EXPERT_GROUPS, n_tok), F32)
    for g in range(N_EXPERT_GROUPS):
        s = sel[g * GROUP_SIZE:(g + 1) * GROUP_SIZE]
        m1, i1 = _first_max_index(s, sub, float(GROUP_SIZE))
        m2 = jnp.max(jnp.where(sub == i1, neg, s), axis=0, keepdims=True)
        gscore = jnp.where(gid == float(g), m1 + m2, gscore)
    eid = lax.broadcasted_iota(jnp.int32, (N_EXPERTS, n_tok), 0).astype(F32)
    egroup = jnp.floor(eid * (1.0 / GROUP_SIZE))
    allowed = jnp.zeros((N_EXPERTS, n_tok), F32)
    cur = gscore
    for _ in range(TOPK_GROUPS):
        _, gi = _first_max_index(cur, gid, float(N_EXPERT_GROUPS))
        cur = jnp.where(gid == gi, neg, cur)
        allowed = jnp.where(egroup == gi, 1.0, allowed)
    cur = jnp.where(allowed > 0.0, sel, neg)
    chosen = jnp.zeros((N_EXPERTS, n_tok), F32)
    for _ in range(TOP_K):
        _, ei = _first_max_index(cur, eid, float(N_EXPERTS))
        hit = eid == ei
        cur = jnp.where(hit, neg, cur)
        chosen = jnp.where(hit, 1.0, chosen)
    w = jnp.where(chosen > 0.0, scores, 0.0)
    return w / jnp.sum(w, axis=0, keepdims=True) * ROUTED_SCALE


def _split_bf16(x):
    hi = x.astype(BF16)
    lo = (x - hi.astype(F32)).astype(BF16)
    return hi, lo


ROW_TILE = 8


def _mixer_out_kernel(mix_ref, att_ref, x_ref, gate1_ref, shift_ref, scale_ref, gate2_ref, wo_ref, wr_ref,
                      rb_ref, sg_ref, su_ref, sd_ref, base_ref, h2r_ref, cw_ref):
    tm = x_ref.shape[0]
    half = mix_ref.shape[1]
    y = _dot(mix_ref[...], wo_ref[0:half, :]) + _dot(att_ref[...], wo_ref[half:, :])
    x2 = x_ref[...] + gate1_ref[0] * y
    h2 = _rms(x2) * (1.0 + scale_ref[0]) + shift_ref[0]
    for j in range(ROW_TILE):
        h2r_ref[pl.ds(j, tm, stride=ROW_TILE), :] = h2[:, j * LANES:(j + 1) * LANES]
    hb = h2.astype(BF16)
    act = _silu(_dot(hb, sg_ref[...])) * _dot(hb, su_ref[...])
    base_ref[...] = x2 + gate2_ref[0] * _dot(act.astype(BF16), sd_ref[...])
    h_hi, h_lo = _split_bf16(h2)
    w_hi, w_lo = _split_bf16(wr_ref[...])
    logits_t = _dot_nt(w_hi, h_hi) + (_dot_nt(w_hi, h_lo) + _dot_nt(w_lo, h_hi))
    cw_ref[...] = _route(logits_t, rb_ref[...])


def _mixer_out(mix, att, x2d, gate1, shift, scale, gate2, p, seq):
    t, d = x2d.shape
    tm = 512
    tiles_per_seq = seq // tm
    ff = p["ws_gate"].shape[1]

    def full(shape):
        return pl.BlockSpec(shape, lambda i: (0,) * len(shape))

    def mod():
        return pl.BlockSpec((1, 1, d), lambda i: (i // tiles_per_seq, 0, 0))

    return pl.pallas_call(
        _mixer_out_kernel,
        grid=(t // tm,),
        in_specs=[pl.BlockSpec((tm, mix.shape[1]), lambda i: (i, 0)),
                  pl.BlockSpec((tm, att.shape[1]), lambda i: (i, 0)),
                  pl.BlockSpec((tm, d), lambda i: (i, 0)), mod(), mod(), mod(), mod(),
                  full((d, d)), full((N_EXPERTS, d)), full((N_EXPERTS, 1)),
                  full((d, ff)), full((d, ff)), full((ff, d))],
        out_specs=[pl.BlockSpec((tm, d), lambda i: (i, 0)),
                   pl.BlockSpec((tm * ROW_TILE, LANES), lambda i: (i, 0)),
                   pl.BlockSpec((N_EXPERTS, tm), lambda i: (0, i))],
        out_shape=[jax.ShapeDtypeStruct((t, d), F32),
                   jax.ShapeDtypeStruct((t * ROW_TILE, LANES), F32),
                   jax.ShapeDtypeStruct((N_EXPERTS, t), F32)],
        compiler_params=pltpu.CompilerParams(dimension_semantics=("arbitrary",),
                                             vmem_limit_bytes=VMEM_LIMIT),
        name="mixer_out",
    )(mix, att, x2d, gate1, shift, scale, gate2, p["w_out"], p["w_router_t"], p["router_bias"],
      p["ws_gate"], p["ws_up"], p["ws_down"])


EXPERT_BLOCK = 128
TOK_BITS = 12
RMW_BATCH = 8
MOE_TOKENS = 4096


def _plan_kernel(cw_ref, tok_ref, w_ref, cnt_ref):
    n_e, ts = cw_ref.shape
    cw = cw_ref[...]
    chosen = cw > 0.0
    cf = jnp.where(chosen, 1.0, 0.0).astype(BF16)
    r_i = lax.broadcasted_iota(jnp.int32, (LANES, LANES), 0)
    c_i = lax.broadcasted_iota(jnp.int32, (LANES, LANES), 1)
    before = jnp.where(r_i < c_i, 1.0, 0.0).astype(BF16)
    ones = jnp.ones((LANES, LANES), BF16)
    carry = jnp.zeros((n_e, LANES), F32)
    ranks = []
    for k in range(ts // LANES):
        ck = cf[:, k * LANES:(k + 1) * LANES]
        ranks.append(_dot(ck, before) + carry)
        carry = carry + _dot(ck, ones)
    rank = jnp.concatenate(ranks, axis=1).astype(jnp.int32)
    lane = lax.broadcasted_iota(jnp.int32, (n_e, ts), 1)
    packed = jnp.where(chosen, ((lane - rank) << TOK_BITS) | lane, -1)
    w = jnp.where(chosen, cw, 0.0)
    for bit in range(ts.bit_length() - 1):
        step = 1 << bit
        src_p = pltpu.roll(packed, ts - step, 1)
        src_w = pltpu.roll(w, ts - step, 1)
        take = (src_p >= 0) & (lane < ts - step) & (((src_p >> (TOK_BITS + bit)) & 1) == 1)
        keep = (packed >= 0) & (((packed >> (TOK_BITS + bit)) & 1) == 0)
        packed = jnp.where(take, src_p, jnp.where(keep, packed, -1))
        w = jnp.where(take, src_w, jnp.where(keep, w, 0.0))
    count = carry[:, 0:1].astype(jnp.int32)
    valid = packed >= 0
    tok = packed & ((1 << TOK_BITS) - 1)
    tok_last = jnp.max(jnp.where(valid, tok, 0).astype(F32), axis=1, keepdims=True).astype(jnp.int32)
    w_last = jnp.sum(jnp.where(lane == count - 1, w, 0.0), axis=1, keepdims=True)
    batch_end = (count + (RMW_BATCH - 1)) & (-RMW_BATCH)
    tok = jnp.where(valid, tok, tok_last)
    w = jnp.where(valid, w, jnp.where(lane < batch_end, w_last, 0.0))
    tail = jnp.broadcast_to(tok_last, (n_e, EXPERT_BLOCK))
    tok_ref[0] = jnp.concatenate([tok, tail], axis=1) * ROW_TILE
    w_ref[0] = jnp.concatenate([w, jnp.zeros((n_e, EXPERT_BLOCK), F32)], axis=1)
    cnt_ref[0] = jnp.broadcast_to(count, (n_e, LANES))


def _plan(cw_t, ts):
    n_e, t = cw_t.shape
    n_sup = t // ts
    assert ts <= (1 << TOK_BITS)
    return pl.pallas_call(
        _plan_kernel,
        grid=(n_sup,),
        in_specs=[pl.BlockSpec((n_e, ts), lambda s: (0, s))],
        out_specs=[pl.BlockSpec((1, n_e, ts + EXPERT_BLOCK), lambda s: (s, 0, 0)),
                   pl.BlockSpec((1, n_e, ts + EXPERT_BLOCK), lambda s: (s, 0, 0)),
                   pl.BlockSpec((1, n_e, LANES), lambda s: (s, 0, 0))],
        out_shape=[jax.ShapeDtypeStruct((n_sup, n_e, ts + EXPERT_BLOCK), jnp.int32),
                   jax.ShapeDtypeStruct((n_sup, n_e, ts + EXPERT_BLOCK), F32),
                   jax.ShapeDtypeStruct((n_sup, n_e, LANES), jnp.int32)],
        compiler_params=pltpu.CompilerParams(dimension_semantics=("arbitrary",),
                                             vmem_limit_bytes=VMEM_LIMIT),
        name="plan",
    )(cw_t)


MOE_GROUP = 2
GROUP_ROWS = MOE_GROUP * EXPERT_BLOCK
XT_PITCH = GROUP_ROWS + 1


def _moe_kernel(cnt_ref, tok_ref, w_ref, h_ref, wg_ref, wu_ref, wd_ref, acc_ref, xt_ref, yt_ref):
    s = pl.program_id(0)
    e = pl.program_id(1)

    @pl.when(e == 0)
    def _():
        acc_ref[...] = jnp.zeros_like(acc_ref)

    n_blocks = (cnt_ref[s, e] + (EXPERT_BLOCK - 1)) // EXPERT_BLOCK

    def group(q, carry):
        first = q * GROUP_ROWS
        for r in range(GROUP_ROWS):
            start = pl.multiple_of(tok_ref[0, 0, first + r], ROW_TILE)
            xt_ref[pl.ds(r, ROW_TILE, stride=XT_PITCH), :] = h_ref[pl.ds(start, ROW_TILE), :]
        x = jnp.concatenate([xt_ref[pl.ds(j * XT_PITCH, GROUP_ROWS), :] for j in range(ROW_TILE)],
                            axis=1).astype(BF16)
        act = _silu(_dot(x, wg_ref[0, 0])) * _dot(x, wu_ref[0, 0])
        y = _dot(act.astype(BF16), wd_ref[0, 0])
        w_col = jnp.concatenate(
            [jnp.broadcast_to(w_ref[0, q * MOE_GROUP + k], (EXPERT_BLOCK, EXPERT_BLOCK)).T
             for k in range(MOE_GROUP)], axis=0)
        for j in range(ROW_TILE):
            yt_ref[pl.ds(j * XT_PITCH, GROUP_ROWS), :] = y[:, j * LANES:(j + 1) * LANES] * w_col
        for grp in range(GROUP_ROWS // RMW_BATCH):
            rows = range(grp * RMW_BATCH, (grp + 1) * RMW_BATCH)
            starts = [pl.multiple_of(tok_ref[0, 0, first + r], ROW_TILE) for r in rows]
            olds = [acc_ref[pl.ds(st, ROW_TILE), :] for st in starts]
            news = [old + yt_ref[pl.ds(r, ROW_TILE, stride=XT_PITCH), :] for r, old in zip(rows, olds)]
            for st, new in zip(starts, news):
                acc_ref[pl.ds(st, ROW_TILE), :] = new
        return carry

    lax.fori_loop(0, (n_blocks + (MOE_GROUP - 1)) // MOE_GROUP, group, 0)


def _moe(h2rows, tok, wts, counts, wg, wu, wd, layer, ts):
    rows = h2rows.shape[0]
    n_sup = rows // (ts * ROW_TILE)
    d, ff = wg.shape[2], wg.shape[3]
    n_list = tok.shape[2]
    assert d == ROW_TILE * LANES and n_list == ts + EXPERT_BLOCK

    def list_idx(s, e, cnt):
        return (s * N_EXPERTS + e, 0, 0)

    def sup_idx(s, e, cnt):
        return (s, 0)

    def w_idx(s, e, cnt):
        return (layer, e, 0, 0)

    once = pl.Buffered(1)
    tile = pltpu.VMEM((ROW_TILE * XT_PITCH, LANES), F32)
    grid_spec = pltpu.PrefetchScalarGridSpec(
        num_scalar_prefetch=1,
        grid=(n_sup, N_EXPERTS),
        in_specs=[pl.BlockSpec((1, 1, n_list), list_idx, memory_space=pltpu.SMEM),
                  pl.BlockSpec((1, n_list // EXPERT_BLOCK, 1, EXPERT_BLOCK), lambda s, e, cnt: (s * N_EXPERTS + e, 0, 0, 0)),
                  pl.BlockSpec((ts * ROW_TILE, LANES), sup_idx, pipeline_mode=once),
                  pl.BlockSpec((1, 1, d, ff), w_idx),
                  pl.BlockSpec((1, 1, d, ff), w_idx),
                  pl.BlockSpec((1, 1, ff, d), w_idx)],
        out_specs=pl.BlockSpec((ts * ROW_TILE, LANES), sup_idx, pipeline_mode=once),
        scratch_shapes=[tile, tile],
    )
    return pl.pallas_call(
        _moe_kernel,
        grid_spec=grid_spec,
        out_shape=jax.ShapeDtypeStruct((rows, LANES), F32),
        compiler_params=pltpu.CompilerParams(dimension_semantics=("arbitrary", "arbitrary"),
                                             vmem_limit_bytes=VMEM_LIMIT),
        name="moe",
    )(counts, tok.reshape(n_sup * N_EXPERTS, 1, n_list),
      wts.reshape(n_sup * N_EXPERTS, n_list // EXPERT_BLOCK, 1, EXPERT_BLOCK), h2rows, wg, wu, wd)


def _routed_rows_to_tile(r_ref, tm):
    return jnp.concatenate([r_ref[pl.ds(j, tm, stride=ROW_TILE), :] for j in range(ROW_TILE)], axis=1)


def _final_kernel(base_ref, r_ref, gate_ref, fg_ref, o_ref):
    tm = base_ref.shape[0]
    x = base_ref[...] + gate_ref[0] * _routed_rows_to_tile(r_ref, tm)
    o_ref[...] = _rms(x) * fg_ref[...]


def _final(base, routed, gate, final_gain, seq):
    t, d = base.shape
    tm = 512
    tiles_per_seq = seq // tm
    return pl.pallas_call(
        _final_kernel,
        grid=(t // tm,),
        in_specs=[pl.BlockSpec((tm, d), lambda i: (i, 0)),
                  pl.BlockSpec((tm * ROW_TILE, LANES), lambda i: (i, 0)),
                  pl.BlockSpec((1, 1, d), lambda i: (i // tiles_per_seq, 0, 0)),
                  pl.BlockSpec((1, d), lambda i: (0, 0))],
        out_specs=pl.BlockSpec((tm, d), lambda i: (i, 0)),
        out_shape=jax.ShapeDtypeStruct((t, d), F32),
        compiler_params=pltpu.CompilerParams(dimension_semantics=("arbitrary",),
                                             vmem_limit_bytes=VMEM_LIMIT),
        name="final_norm",
    )(base, routed, gate, final_gain)


def _prep_layer(w_in, w_pool, pool_scale, w_spatial, b_spatial, g_q, w_uq, g_kv, w_ukv, w_out,
                w_router, router_bias, ws_gate, ws_up, ws_down):
    d = w_in.shape[0]
    o_kpe = POOL_WIDTH + 2 * SG_WIDTH + Q_LORA + KV_LORA
    x1 = w_in[:, o_kpe:o_kpe + HALF_ROPE]
    x2 = w_in[:, o_kpe + HALF_ROPE:o_kpe + QK_ROPE]
    zl = jnp.zeros((d, QK_NOPE), F32)
    zr = jnp.zeros((d, HEAD_PAD - QK_NOPE - QK_ROPE), F32)
    w_in_pad = jnp.concatenate([w_in[:, :o_kpe], zl, x1, x2, zr, zl, x2, x1, zr], axis=1)

    uq = w_uq.reshape(Q_LORA, MLA_HEADS, QK_NOPE + QK_ROPE)
    q1 = uq[..., QK_NOPE:QK_NOPE + HALF_ROPE]
    q2 = uq[..., QK_NOPE + HALF_ROPE:]
    zq = jnp.zeros((Q_LORA, MLA_HEADS, HEAD_PAD - QK_NOPE - QK_ROPE), F32)
    w_uq_pad = jnp.concatenate([uq, zq], axis=-1).reshape(Q_LORA, MLA_HEADS * HEAD_PAD)
    w_uq_sw = jnp.concatenate([jnp.zeros_like(uq[..., :QK_NOPE]), q2, q1, zq], axis=-1)
    w_uq_sw = w_uq_sw.reshape(Q_LORA, MLA_HEADS * HEAD_PAD)

    ukv = w_ukv.reshape(KV_LORA, MLA_HEADS, QK_NOPE + V_DIM)
    zk = jnp.zeros((KV_LORA, MLA_HEADS, HEAD_PAD - QK_NOPE), F32)
    w_k = jnp.concatenate([ukv[..., :QK_NOPE], zk], axis=-1).reshape(KV_LORA, MLA_HEADS * HEAD_PAD)
    w_v = ukv[..., QK_NOPE:].reshape(KV_LORA, MLA_HEADS * V_DIM)

    w_pool_bd = jax.scipy.linalg.block_diag(*[w_pool[g] for g in range(len(POOL_WINDOWS))])
    b_sp = jnp.repeat(b_spatial.T, SG_DIM, axis=1)
    return {
        "w_in": w_in_pad.astype(BF16), "w_pool": w_pool_bd.astype(BF16),
        "pool_scale": pool_scale.reshape(1, -1), "w_spatial": w_spatial, "b_spatial": b_sp,
        "g_q": g_q.reshape(1, -1), "w_uq": w_uq_pad.astype(BF16), "w_uq_sw": w_uq_sw.astype(BF16),
        "g_kv": g_kv.reshape(1, -1), "w_k": w_k.astype(BF16), "w_v": w_v.astype(BF16),
        "w_out": w_out.astype(BF16), "w_router_t": w_router.T, "router_bias": router_bias.reshape(-1, 1),
        "ws_gate": ws_gate.astype(BF16), "ws_up": ws_up.astype(BF16), "ws_down": ws_down.astype(BF16),
    }


def kernel(x, c, positions, w_ada, b_ada, w_in, w_pool, pool_scale, w_spatial, b_spatial, g_q, w_uq, g_kv, w_ukv, w_out, w_router, router_bias, w_gate, w_up, w_down, ws_gate, ws_up, ws_down, final_gain):
    batch, seq, d = x.shape
    depth = w_ada.shape[0]
    mod = _ada_mod(c, w_ada, b_ada)
    cos_t, sin_t = _rope_tables(positions)
    x_parts = (x.reshape(batch * seq, d),)
    fg = final_gain.reshape(1, d)
    wg_b, wu_b, wd_b = w_gate.astype(BF16), w_up.astype(BF16), w_down.astype(BF16)
    ts = min(MOE_TOKENS, batch * seq)
    for l in range(depth):
        p = _prep_layer(w_in[l], w_pool[l], pool_scale[l], w_spatial[l], b_spatial[l], g_q[l], w_uq[l],
                        g_kv[l], w_ukv[l], w_out[l], w_router[l], router_bias[l], ws_gate[l], ws_up[l],
                        ws_down[l])
        shift1, scale1, gate1, shift2, scale2, gate2 = [
            mod[l, :, k * d:(k + 1) * d].reshape(batch, 1, d) for k in range(6)]
        outs = _mixer_in(x_parts, shift1, scale1, cos_t, sin_t, p, seq)
        mix, q, k, v = outs[:4]
        xt = outs[4] if len(outs) == 5 else x_parts[0]
        att = _attention(q, k, v, batch, seq)
        base, h2rows, cw_t = _mixer_out(mix, att, xt, gate1, shift2, scale2, gate2, p, seq)
        tok, wts, cnt = _plan(cw_t, ts)
        routed = _moe(h2rows, tok, wts, cnt[:, :, 0], wg_b, wu_b, wd_b, l, ts)
        x_parts = (base, routed, gate2)
    return _final(*x_parts, fg, seq).reshape(batch, seq, d)
```

```python
import functools

import jax
import jax.numpy as jnp
from jax import lax
from jax.experimental import pallas as pl
from jax.experimental.pallas import tpu as pltpu

F32 = jnp.float32
BF16 = jnp.bfloat16

EPS = 1e-6
LANES = 128
POOL_WINDOWS = (2, 4, 8, 16)
POOL_WIDTH = 256
POOL_CH = 64
MAX_WINDOW = 16
SG_HEADS = 4
SG_WIDTH = 256
SG_DIM = 64
SG_CHUNK = 128
MLA_HEADS = 8
V_DIM = 64
QK_NOPE = 64
QK_ROPE = 32
HALF_ROPE = QK_ROPE // 2
Q_LORA = 256
KV_LORA = 128
ROPE_THETA = 10000.0
ATTN_SCALE = (QK_NOPE + QK_ROPE) ** -0.5
HEAD_PAD = 128
N_EXPERTS = 64
TOP_K = 8
N_EXPERT_GROUPS = 8
GROUP_SIZE = N_EXPERTS // N_EXPERT_GROUPS
TOPK_GROUPS = 4
ROUTED_SCALE = 2.5
VMEM_LIMIT = 52 * 1024 * 1024


def _dot(a, b):
    return jnp.dot(a, b, preferred_element_type=F32)


def _dot_nt(a, b):
    return lax.dot_general(a, b, (((1,), (1,)), ((), ())), preferred_element_type=F32)


def _rms(x):
    return x * lax.rsqrt(jnp.mean(x * x, axis=-1, keepdims=True) + EPS)


def _gelu_tanh(x):
    c = (2.0 / jnp.pi) ** 0.5
    return x * (0.5 * (1.0 + jnp.tanh(c * (x + 0.044715 * (x * x * x)))))


def _silu(x):
    return x * jax.nn.sigmoid(x)


def _ada_kernel(c_ref, w_ref, b_ref, o_ref):
    cond = _silu(c_ref[...])
    o_ref[0] = _dot(cond.astype(BF16), w_ref[0].astype(BF16)) + b_ref[0]


def _ada_mod(c, w_ada, b_ada):
    depth, d, n = w_ada.shape
    b = c.shape[0]
    nt = 1536
    return pl.pallas_call(
        _ada_kernel,
        grid=(depth, n // nt),
        in_specs=[pl.BlockSpec((b, d), lambda l, j: (0, 0)),
                  pl.BlockSpec((1, d, nt), lambda l, j: (l, 0, j)),
                  pl.BlockSpec((1, 1, nt), lambda l, j: (l, 0, j))],
        out_specs=pl.BlockSpec((1, b, nt), lambda l, j: (l, 0, j)),
        out_shape=jax.ShapeDtypeStruct((depth, b, n), F32),
        compiler_params=pltpu.CompilerParams(vmem_limit_bytes=VMEM_LIMIT),
        name="ada_mod",
    )(c, w_ada, b_ada.reshape(depth, 1, n))


def _rope_kernel(pos_ref, invf_ref, sign_ref, cos_ref, sin_ref):
    ang = pos_ref[...].astype(F32) * invf_ref[...]
    cos_ref[...] = jnp.cos(ang)
    sin_ref[...] = jnp.sin(ang) * sign_ref[...]


def _rope_tables(positions):
    t = positions.size
    tm = 2048
    inv_freq = ROPE_THETA ** (-jnp.arange(0, QK_ROPE, 2, dtype=F32) / QK_ROPE)
    invf = jnp.zeros((1, HEAD_PAD), F32)
    invf = invf.at[0, QK_NOPE:QK_NOPE + HALF_ROPE].set(inv_freq)
    invf = invf.at[0, QK_NOPE + HALF_ROPE:QK_NOPE + QK_ROPE].set(inv_freq)
    sign = jnp.zeros((1, HEAD_PAD), F32)
    sign = sign.at[0, QK_NOPE:QK_NOPE + HALF_ROPE].set(-1.0)
    sign = sign.at[0, QK_NOPE + HALF_ROPE:QK_NOPE + QK_ROPE].set(1.0)
    return pl.pallas_call(
        _rope_kernel,
        grid=(t // tm,),
        in_specs=[pl.BlockSpec((tm, 1), lambda i: (i, 0)),
                  pl.BlockSpec((1, HEAD_PAD), lambda i: (0, 0)),
                  pl.BlockSpec((1, HEAD_PAD), lambda i: (0, 0))],
        out_specs=[pl.BlockSpec((tm, HEAD_PAD), lambda i: (i, 0)),
                   pl.BlockSpec((tm, HEAD_PAD), lambda i: (i, 0))],
        out_shape=[jax.ShapeDtypeStruct((t, HEAD_PAD), F32)] * 2,
        name="rope_tables",
    )(positions.reshape(t, 1), invf, sign)


def _mixer_in_kernel(*refs, tiles_per_seq, fused):
    if fused:
        base_ref, r_ref, gprev_ref = refs[:3]
        (shift_ref, scale_ref, win_ref, wpool_ref, pscale_ref, wsp_ref, bsp_ref, gq_ref, wuq_ref, wuqs_ref,
         gkv_ref, wk_ref, wv_ref, cos_ref, sin_ref, mix_ref, q_ref, k_ref, v_ref, x_out_ref, carry_ref) = refs[3:]
        tm = base_ref.shape[0]
        x = base_ref[...] + gprev_ref[0] * jnp.concatenate(
            [r_ref[pl.ds(j, tm, stride=ROW_TILE), :] for j in range(ROW_TILE)], axis=1)
        x_out_ref[...] = x
    else:
        (x_ref, shift_ref, scale_ref, win_ref, wpool_ref, pscale_ref, wsp_ref, bsp_ref, gq_ref, wuq_ref, wuqs_ref,
         gkv_ref, wk_ref, wv_ref, cos_ref, sin_ref, mix_ref, q_ref, k_ref, v_ref, carry_ref) = refs
        tm = x_ref.shape[0]
        x = x_ref[...]
    ti = pl.program_id(0) % tiles_per_seq
    h = _rms(x) * (1.0 + scale_ref[0]) + shift_ref[0]
    z = _dot(h.astype(BF16), win_ref[...])

    a = z[:, 0:POOL_WIDTH]

    @pl.when(ti == 0)
    def _():
        carry_ref[...] = jnp.zeros_like(carry_ref)

    ext = jnp.concatenate([carry_ref[...], a], axis=0)
    carry_ref[...] = a[tm - MAX_WINDOW:, :]
    p1 = ext + pltpu.roll(ext, 1, 0)
    p2 = p1 + pltpu.roll(p1, 2, 0)
    p3 = p2 + pltpu.roll(p2, 4, 0)
    p4 = p3 + pltpu.roll(p3, 8, 0)
    lane = lax.broadcasted_iota(jnp.int32, (tm, POOL_WIDTH), 1)
    row = lax.broadcasted_iota(jnp.int32, (tm, POOL_WIDTH), 0) + (ti * tm + 1)
    g0, g1, g2 = lane < POOL_CH, lane < 2 * POOL_CH, lane < 3 * POOL_CH
    wsum = jnp.where(g0, p1[MAX_WINDOW:], jnp.where(g1, p2[MAX_WINDOW:],
                     jnp.where(g2, p3[MAX_WINDOW:], p4[MAX_WINDOW:])))
    width = jnp.where(g0, POOL_WINDOWS[0], jnp.where(g1, POOL_WINDOWS[1],
                      jnp.where(g2, POOL_WINDOWS[2], POOL_WINDOWS[3])))
    cnt = jnp.minimum(row, width).astype(F32)
    dlt = wsum / cnt - a
    y_pool = _dot(dlt.astype(BF16), wpool_ref[...]) * pscale_ref[...]
    mix_ref[:, 0:POOL_WIDTH] = y_pool.astype(mix_ref.dtype)

    ug = _gelu_tanh(z[:, POOL_WIDTH:POOL_WIDTH + SG_WIDTH])
    vg = _gelu_tanh(z[:, POOL_WIDTH + SG_WIDTH:POOL_WIDTH + 2 * SG_WIDTH])
    mu = jnp.mean(vg, axis=-1, keepdims=True)
    vc = vg - mu
    var = jnp.mean(vc * vc, axis=-1, keepdims=True)
    vn = (vc * lax.rsqrt(var + EPS)).astype(BF16)
    r_i = lax.broadcasted_iota(jnp.int32, (SG_CHUNK, SG_CHUNK), 0)
    c_i = lax.broadcasted_iota(jnp.int32, (SG_CHUNK, SG_CHUNK), 1)
    wms = [jnp.where(c_i <= r_i, wsp_ref[hh], 0.0).astype(BF16) for hh in range(SG_HEADS)]
    lane_head = lax.broadcasted_iota(jnp.int32, (SG_CHUNK, SG_WIDTH), 1) // SG_DIM
    for cidx in range(tm // SG_CHUNK):
        rows = slice(cidx * SG_CHUNK, (cidx + 1) * SG_CHUNK)
        vchunk = vn[rows]
        mixed = bsp_ref[...]
        for hh in range(SG_HEADS):
            mixed = mixed + jnp.where(lane_head == hh, _dot(wms[hh], vchunk), 0.0)
        mix_ref[rows, POOL_WIDTH:POOL_WIDTH + SG_WIDTH] = (ug[rows] * mixed).astype(mix_ref.dtype)

    o_cq = POOL_WIDTH + 2 * SG_WIDTH
    o_ckv = o_cq + Q_LORA
    o_kpe = o_ckv + KV_LORA
    cos = cos_ref[...]
    sin = sin_ref[...]
    cqn = (_rms(z[:, o_cq:o_ckv]) * gq_ref[...]).astype(BF16)
    q = _dot(cqn, wuq_ref[...])
    qs = _dot(cqn, wuqs_ref[...])
    ckvn = (_rms(z[:, o_ckv:o_kpe]) * gkv_ref[...]).astype(BF16)
    kn = _dot(ckvn, wk_ref[...])
    kpe = z[:, o_kpe:o_kpe + HEAD_PAD] * cos + z[:, o_kpe + HEAD_PAD:o_kpe + 2 * HEAD_PAD] * sin
    for hh in range(MLA_HEADS):
        blk = slice(hh * HEAD_PAD, (hh + 1) * HEAD_PAD)
        q_ref[:, blk] = ((q[:, blk] * cos + qs[:, blk] * sin) * ATTN_SCALE).astype(q_ref.dtype)
        k_ref[:, blk] = (kn[:, blk] + kpe).astype(k_ref.dtype)
    v_ref[...] = _dot(ckvn, wv_ref[...]).astype(v_ref.dtype)


def _mixer_in(x_parts, shift, scale, cos_t, sin_t, p, seq):
    fused = len(x_parts) == 3
    t, d = x_parts[0].shape
    tm = 512
    tiles_per_seq = seq // tm
    nz = p["w_in"].shape[1]
    hq = MLA_HEADS * HEAD_PAD

    def full(shape):
        return pl.BlockSpec(shape, lambda i: (0,) * len(shape))

    def mod():
        return pl.BlockSpec((1, 1, d), lambda i: (i // tiles_per_seq, 0, 0))

    x_specs = [pl.BlockSpec((tm, d), lambda i: (i, 0))]
    extra_out_specs, extra_out_shapes = [], []
    if fused:
        x_specs += [pl.BlockSpec((tm * ROW_TILE, LANES), lambda i: (i, 0)), mod()]
        extra_out_specs = [pl.BlockSpec((tm, d), lambda i: (i, 0))]
        extra_out_shapes = [jax.ShapeDtypeStruct((t, d), F32)]
    return pl.pallas_call(
        functools.partial(_mixer_in_kernel, tiles_per_seq=tiles_per_seq, fused=fused),
        grid=(t // tm,),
        in_specs=x_specs + [mod(), mod(),
                  full((d, nz)), full((POOL_WIDTH, POOL_WIDTH)), full((1, POOL_WIDTH)),
                  full((SG_HEADS, SG_CHUNK, SG_CHUNK)), full((SG_CHUNK, SG_WIDTH)),
                  full((1, Q_LORA)), full((Q_LORA, hq)), full((Q_LORA, hq)),
                  full((1, KV_LORA)), full((KV_LORA, hq)), full((KV_LORA, MLA_HEADS * V_DIM)),
                  pl.BlockSpec((tm, HEAD_PAD), lambda i: (i, 0)),
                  pl.BlockSpec((tm, HEAD_PAD), lambda i: (i, 0))],
        out_specs=[pl.BlockSpec((tm, POOL_WIDTH + SG_WIDTH), lambda i: (i, 0)),
                   pl.BlockSpec((tm, hq), lambda i: (i, 0)),
                   pl.BlockSpec((tm, hq), lambda i: (i, 0)),
                   pl.BlockSpec((tm, MLA_HEADS * V_DIM), lambda i: (i, 0))] + extra_out_specs,
        out_shape=[jax.ShapeDtypeStruct((t, POOL_WIDTH + SG_WIDTH), BF16),
                   jax.ShapeDtypeStruct((t, hq), BF16),
                   jax.ShapeDtypeStruct((t, hq), BF16),
                   jax.ShapeDtypeStruct((t, MLA_HEADS * V_DIM), BF16)] + extra_out_shapes,
        scratch_shapes=[pltpu.VMEM((MAX_WINDOW, POOL_WIDTH), F32)],
        compiler_params=pltpu.CompilerParams(dimension_semantics=("arbitrary",),
                                             vmem_limit_bytes=VMEM_LIMIT),
        name="mixer_in",
    )(*x_parts, shift, scale, p["w_in"], p["w_pool"], p["pool_scale"], p["w_spatial"], p["b_spatial"],
      p["g_q"], p["w_uq"], p["w_uq_sw"], p["g_kv"], p["w_k"], p["w_v"], cos_t, sin_t)


ATTN_TQ = 256


def _attn_kernel(q_ref, k_ref, v_ref, o_ref):
    seq = q_ref.shape[0]
    tq = ATTN_TQ
    r_i = lax.broadcasted_iota(jnp.int32, (tq, tq), 0)
    c_i = lax.broadcasted_iota(jnp.int32, (tq, tq), 1)
    lane = lax.broadcasted_iota(jnp.int32, (tq, 2 * V_DIM), 1)
    for i in range(seq // tq):
        lo, hi = i * tq, (i + 1) * tq
        outs = []
        for hh in range(2):
            blk = slice(hh * HEAD_PAD, (hh + 1) * HEAD_PAD)
            q = q_ref[lo:hi, blk]
            s_d = jnp.where(c_i <= r_i, _dot_nt(q, k_ref[lo:hi, blk]), -jnp.inf)
            m = jnp.max(s_d, axis=-1, keepdims=True)
            if i > 0:
                s_o = _dot_nt(q, k_ref[0:lo, blk])
                m = jnp.maximum(m, jnp.max(s_o, axis=-1, keepdims=True))
                p_o = jnp.exp(s_o - m)
            p_d = jnp.exp(s_d - m)
            l = jnp.sum(p_d, axis=-1, keepdims=True)
            acc = _dot(p_d.astype(BF16), v_ref[lo:hi, :])
            if i > 0:
                l = l + jnp.sum(p_o, axis=-1, keepdims=True)
                acc = acc + _dot(p_o.astype(BF16), v_ref[0:lo, :])
            outs.append(acc * (1.0 / l))
        o_ref[lo:hi, :] = jnp.where(lane < V_DIM, outs[0], outs[1]).astype(o_ref.dtype)


def _attention(q, k, v, batch, seq):
    t = q.shape[0]
    return pl.pallas_call(
        _attn_kernel,
        grid=(batch, MLA_HEADS // 2),
        in_specs=[pl.BlockSpec((seq, 2 * HEAD_PAD), lambda b, hp: (b, hp)),
                  pl.BlockSpec((seq, 2 * HEAD_PAD), lambda b, hp: (b, hp)),
                  pl.BlockSpec((seq, 2 * V_DIM), lambda b, hp: (b, hp))],
        out_specs=pl.BlockSpec((seq, 2 * V_DIM), lambda b, hp: (b, hp)),
        out_shape=jax.ShapeDtypeStruct((t, MLA_HEADS * V_DIM), BF16),
        compiler_params=pltpu.CompilerParams(
            dimension_semantics=("arbitrary", "arbitrary"), vmem_limit_bytes=VMEM_LIMIT),
        name="attention",
    )(q, k, v)


def _first_max_index(cur, idx, sentinel):
    m = jnp.max(cur, axis=0, keepdims=True)
    first = jnp.min(jnp.where(cur == m, idx, sentinel), axis=0, keepdims=True)
    return m, first


def _route(logits_t, bias_t):
    n_tok = logits_t.shape[1]
    scores = jax.nn.sigmoid(logits_t)
    sel = scores + bias_t
    neg = -jnp.inf
    sub = lax.broadcasted_iota(jnp.int32, (GROUP_SIZE, n_tok), 0).astype(F32)
    gid = lax.broadcasted_iota(jnp.int32, (N_EXPERT_GROUPS, n_tok), 0).astype(F32)
    gscore = jnp.zeros((N_EXPERT_GROUPS, n_tok), F32)
    for g in range(N_EXPERT_GROUPS):
        s = sel[g * GROUP_SIZE:(g + 1) * GROUP_SIZE]
        m1, i1 = _first_max_index(s, sub, float(GROUP_SIZE))
        m2 = jnp.max(jnp.where(sub == i1, neg, s), axis=0, keepdims=True)
        gscore = jnp.where(gid == float(g), m1 + m2, gscore)
    eid = lax.broadcasted_iota(jnp.int32, (N_EXPERTS, n_tok), 0).astype(F32)
    egroup = jnp.floor(eid * (1.0 / GROUP_SIZE))
    allowed = jnp.zeros((N_EXPERTS, n_tok), F32)
    cur = gscore
    for _ in range(TOPK_GROUPS):
        _, gi = _first_max_index(cur, gid, float(N_EXPERT_GROUPS))
        cur = jnp.where(gid == gi, neg, cur)
        allowed = jnp.where(egroup == gi, 1.0, allowed)
    cur = jnp.where(allowed > 0.0, sel, neg)
    chosen = jnp.zeros((N_EXPERTS, n_tok), F32)
    for _ in range(TOP_K):
        _, ei = _first_max_index(cur, eid, float(N_EXPERTS))
        hit = eid == ei
        cur = jnp.where(hit, neg, cur)
        chosen = jnp.where(hit, 1.0, chosen)
    w = jnp.where(chosen > 0.0, scores, 0.0)
    return w / jnp.sum(w, axis=0, keepdims=True) * ROUTED_SCALE


def _split_bf16(x):
    hi = x.astype(BF16)
    lo = (x - hi.astype(F32)).astype(BF16)
    return hi, lo


ROW_TILE = 8


def _mixer_out_kernel(mix_ref, att_ref, x_ref, gate1_ref, shift_ref, scale_ref, gate2_ref, wo_ref, wr_ref,
                      rb_ref, sg_ref, su_ref, sd_ref, base_ref, h2r_ref, cw_ref):
    tm = x_ref.shape[0]
    half = mix_ref.shape[1]
    y = _dot(mix_ref[...], wo_ref[0:half, :]) + _dot(att_ref[...], wo_ref[half:, :])
    x2 = x_ref[...] + gate1_ref[0] * y
    h2 = _rms(x2) * (1.0 + scale_ref[0]) + shift_ref[0]
    for j in range(ROW_TILE):
        h2r_ref[pl.ds(j, tm, stride=ROW_TILE), :] = h2[:, j * LANES:(j + 1) * LANES]
    hb = h2.astype(BF16)
    act = _silu(_dot(hb, sg_ref[...])) * _dot(hb, su_ref[...])
    base_ref[...] = x2 + gate2_ref[0] * _dot(act.astype(BF16), sd_ref[...])
    h_hi, h_lo = _split_bf16(h2)
    w_hi, w_lo = _split_bf16(wr_ref[...])
    logits_t = _dot_nt(w_hi, h_hi) + (_dot_nt(w_hi, h_lo) + _dot_nt(w_lo, h_hi))
    cw_ref[...] = _route(logits_t, rb_ref[...])


def _mixer_out(mix, att, x2d, gate1, shift, scale, gate2, p, seq):
    t, d = x2d.shape
    tm = 512
    tiles_per_seq = seq // tm
    ff = p["ws_gate"].shape[1]

    def full(shape):
        return pl.BlockSpec(shape, lambda i: (0,) * len(shape))

    def mod():
        return pl.BlockSpec((1, 1, d), lambda i: (i // tiles_per_seq, 0, 0))

    return pl.pallas_call(
        _mixer_out_kernel,
        grid=(t // tm,),
        in_specs=[pl.BlockSpec((tm, mix.shape[1]), lambda i: (i, 0)),
                  pl.BlockSpec((tm, att.shape[1]), lambda i: (i, 0)),
                  pl.BlockSpec((tm, d), lambda i: (i, 0)), mod(), mod(), mod(), mod(),
                  full((d, d)), full((N_EXPERTS, d)), full((N_EXPERTS, 1)),
                  full((d, ff)), full((d, ff)), full((ff, d))],
        out_specs=[pl.BlockSpec((tm, d), lambda i: (i, 0)),
                   pl.BlockSpec((tm * ROW_TILE, LANES), lambda i: (i, 0)),
                   pl.BlockSpec((N_EXPERTS, tm), lambda i: (0, i))],
        out_shape=[jax.ShapeDtypeStruct((t, d), F32),
                   jax.ShapeDtypeStruct((t * ROW_TILE, LANES), F32),
                   jax.ShapeDtypeStruct((N_EXPERTS, t), F32)],
        compiler_params=pltpu.CompilerParams(dimension_semantics=("arbitrary",),
                                             vmem_limit_bytes=VMEM_LIMIT),
        name="mixer_out",
    )(mix, att, x2d, gate1, shift, scale, gate2, p["w_out"], p["w_router_t"], p["router_bias"],
      p["ws_gate"], p["ws_up"], p["ws_down"])


EXPERT_BLOCK = 128
TOK_BITS = 12
RMW_BATCH = 8
MOE_TOKENS = 4096


def _plan_kernel(cw_ref, tok_ref, w_ref, cnt_ref):
    n_e, ts = cw_ref.shape
    cw = cw_ref[...]
    chosen = cw > 0.0
    cf = jnp.where(chosen, 1.0, 0.0).astype(BF16)
    r_i = lax.broadcasted_iota(jnp.int32, (LANES, LANES), 0)
    c_i = lax.broadcasted_iota(jnp.int32, (LANES, LANES), 1)
    before = jnp.where(r_i < c_i, 1.0, 0.0).astype(BF16)
    ones = jnp.ones((LANES, LANES), BF16)
    carry = jnp.zeros((n_e, LANES), F32)
    ranks = []
    for k in range(ts // LANES):
        ck = cf[:, k * LANES:(k + 1) * LANES]
        ranks.append(_dot(ck, before) + carry)
        carry = carry + _dot(ck, ones)
    rank = jnp.concatenate(ranks, axis=1).astype(jnp.int32)
    lane = lax.broadcasted_iota(jnp.int32, (n_e, ts), 1)
    packed = jnp.where(chosen, ((lane - rank) << TOK_BITS) | lane, -1)
    w = jnp.where(chosen, cw, 0.0)
    for bit in range(ts.bit_length() - 1):
        step = 1 << bit
        src_p = pltpu.roll(packed, ts - step, 1)
        src_w = pltpu.roll(w, ts - step, 1)
        take = (src_p >= 0) & (lane < ts - step) & (((src_p >> (TOK_BITS + bit)) & 1) == 1)
        keep = (packed >= 0) & (((packed >> (TOK_BITS + bit)) & 1) == 0)
        packed = jnp.where(take, src_p, jnp.where(keep, packed, -1))
        w = jnp.where(take, src_w, jnp.where(keep, w, 0.0))
    count = carry[:, 0:1].astype(jnp.int32)
    valid = packed >= 0
    tok = packed & ((1 << TOK_BITS) - 1)
    tok_last = jnp.max(jnp.where(valid, tok, 0).astype(F32), axis=1, keepdims=True).astype(jnp.int32)
    w_last = jnp.sum(jnp.where(lane == count - 1, w, 0.0), axis=1, keepdims=True)
    batch_end = (count + (RMW_BATCH - 1)) & (-RMW_BATCH)
    tok = jnp.where(valid, tok, tok_last)
    w = jnp.where(valid, w, jnp.where(lane < batch_end, w_last, 0.0))
    tail = jnp.broadcast_to(tok_last, (n_e, EXPERT_BLOCK))
    tok_ref[0] = jnp.concatenate([tok, tail], axis=1) * ROW_TILE
    w_ref[0] = jnp.concatenate([w, jnp.zeros((n_e, EXPERT_BLOCK), F32)], axis=1)
    cnt_ref[0] = jnp.broadcast_to(count, (n_e, LANES))


def _plan(cw_t, ts):
    n_e, t = cw_t.shape
    n_sup = t // ts
    assert ts <= (1 << TOK_BITS)
    return pl.pallas_call(
        _plan_kernel,
        grid=(n_sup,),
        in_specs=[pl.BlockSpec((n_e, ts), lambda s: (0, s))],
        out_specs=[pl.BlockSpec((1, n_e, ts + EXPERT_BLOCK), lambda s: (s, 0, 0)),
                   pl.BlockSpec((1, n_e, ts + EXPERT_BLOCK), lambda s: (s, 0, 0)),
                   pl.BlockSpec((1, n_e, LANES), lambda s: (s, 0, 0))],
        out_shape=[jax.ShapeDtypeStruct((n_sup, n_e, ts + EXPERT_BLOCK), jnp.int32),
                   jax.ShapeDtypeStruct((n_sup, n_e, ts + EXPERT_BLOCK), F32),
                   jax.ShapeDtypeStruct((n_sup, n_e, LANES), jnp.int32)],
        compiler_params=pltpu.CompilerParams(dimension_semantics=("arbitrary",),
                                             vmem_limit_bytes=VMEM_LIMIT),
        name="plan",
    )(cw_t)


MOE_GROUP = 2
GROUP_ROWS = MOE_GROUP * EXPERT_BLOCK
XT_PITCH = GROUP_ROWS + 1


def _moe_gather(list_ref, first, h_ref, xt_ref):
    for r in range(GROUP_ROWS):
        start = pl.multiple_of(list_ref[0, 0, first + r], ROW_TILE)
        xt_ref[pl.ds(r, ROW_TILE, stride=XT_PITCH), :] = h_ref[pl.ds(start, ROW_TILE), :]


def _moe_scatter_add(list_ref, first, yt_ref, acc_ref):
    for grp in range(GROUP_ROWS // RMW_BATCH):
        rows = range(grp * RMW_BATCH, (grp + 1) * RMW_BATCH)
        starts = [pl.multiple_of(list_ref[0, 0, first + r], ROW_TILE) for r in rows]
        olds = [acc_ref[pl.ds(st, ROW_TILE), :] for st in starts]
        news = [old + yt_ref[pl.ds(r, ROW_TILE, stride=XT_PITCH), :] for r, old in zip(rows, olds)]
        for st, new in zip(starts, news):
            acc_ref[pl.ds(st, ROW_TILE), :] = new


def _moe_pack(xt_ref, xb_ref, slot):
    xb_ref[slot] = jnp.concatenate(
        [xt_ref[pl.ds(j * XT_PITCH, GROUP_ROWS), :] for j in range(ROW_TILE)], axis=1).astype(BF16)


def _moe_experts(q, slot, xb_ref, w_ref, wg_ref, wu_ref, wd_ref, yt_ref):
    x = xb_ref[slot]
    act = _silu(_dot(x, wg_ref[0, 0])) * _dot(x, wu_ref[0, 0])
    y = _dot(act.astype(BF16), wd_ref[0, 0])
    w_col = jnp.concatenate(
        [jnp.broadcast_to(w_ref[0, q * MOE_GROUP + k], (EXPERT_BLOCK, EXPERT_BLOCK)).T
         for k in range(MOE_GROUP)], axis=0)
    for j in range(ROW_TILE):
        yt_ref[pl.ds(j * XT_PITCH, GROUP_ROWS), :] = y[:, j * LANES:(j + 1) * LANES] * w_col


def _groups(count):
    return jnp.maximum((count + (GROUP_ROWS - 1)) // GROUP_ROWS, 1)


def _moe_kernel(cnt_ref, gstart_ref, tokp_ref, tok_ref, tokn_ref, w_ref, h_ref, wg_ref, wu_ref, wd_ref, acc_ref,
                xt_ref, xb_ref, yt_ref):
    s = pl.program_id(0)
    e = pl.program_id(1)
    n = _groups(cnt_ref[s, e])
    n_prev = _groups(cnt_ref[s, jnp.maximum(e - 1, 0)])
    g0 = gstart_ref[s, e]

    @pl.when(e == 0)
    def _():
        acc_ref[...] = jnp.zeros_like(acc_ref)
        yt_ref[...] = jnp.zeros_like(yt_ref)
        _moe_gather(tok_ref, 0, h_ref, xt_ref)
        _moe_pack(xt_ref, xb_ref, 0)

    def stage(q, add_ref, add_first, gather_ref, gather_first):
        slot = (g0 + q) % 2
        _moe_gather(gather_ref, gather_first, h_ref, xt_ref)
        _moe_scatter_add(add_ref, add_first, yt_ref, acc_ref)
        _moe_experts(q, slot, xb_ref, w_ref, wg_ref, wu_ref, wd_ref, yt_ref)
        _moe_pack(xt_ref, xb_ref, 1 - slot)

    prev_first = (n_prev - 1) * GROUP_ROWS

    @pl.when(n == 1)
    def _():
        stage(0, tokp_ref, prev_first, tokn_ref, 0)

    @pl.when(n > 1)
    def _():
        stage(0, tokp_ref, prev_first, tok_ref, GROUP_ROWS)

        def middle(q, carry):
            stage(q, tok_ref, (q - 1) * GROUP_ROWS, tok_ref, (q + 1) * GROUP_ROWS)
            return carry

        lax.fori_loop(1, n - 1, middle, 0)
        stage(n - 1, tok_ref, (n - 2) * GROUP_ROWS, tokn_ref, 0)

    @pl.when(e == pl.num_programs(1) - 1)
    def _():
        _moe_scatter_add(tok_ref, (n - 1) * GROUP_ROWS, yt_ref, acc_ref)


def _moe(h2rows, tok, wts, counts, wg, wu, wd, layer, ts):
    rows = h2rows.shape[0]
    n_sup = rows // (ts * ROW_TILE)
    d, ff = wg.shape[2], wg.shape[3]
    n_list = tok.shape[2]
    assert d == ROW_TILE * LANES and n_list == ts + EXPERT_BLOCK

    def list_idx(shift):
        def idx(s, e, cnt, gst):
            return (s * N_EXPERTS + jnp.clip(e + shift, 0, N_EXPERTS - 1), 0, 0)
        return idx

    def sup_idx(s, e, cnt, gst):
        return (s, 0)

    def w_idx(s, e, cnt, gst):
        return (layer, e, 0, 0)

    def list_spec(shift):
        return pl.BlockSpec((1, 1, n_list), list_idx(shift), memory_space=pltpu.SMEM)

    groups = _groups(counts)
    gstart = jnp.cumsum(groups, axis=1) - groups

    once = pl.Buffered(1)
    tile = pltpu.VMEM((ROW_TILE * XT_PITCH, LANES), F32)
    grid_spec = pltpu.PrefetchScalarGridSpec(
        num_scalar_prefetch=2,
        grid=(n_sup, N_EXPERTS),
        in_specs=[list_spec(-1), list_spec(0), list_spec(1),
                  pl.BlockSpec((1, n_list // EXPERT_BLOCK, 1, EXPERT_BLOCK),
                               lambda s, e, cnt, gst: (s * N_EXPERTS + e, 0, 0, 0)),
                  pl.BlockSpec((ts * ROW_TILE, LANES), sup_idx, pipeline_mode=once),
                  pl.BlockSpec((1, 1, d, ff), w_idx),
                  pl.BlockSpec((1, 1, d, ff), w_idx),
                  pl.BlockSpec((1, 1, ff, d), w_idx)],
        out_specs=pl.BlockSpec((ts * ROW_TILE, LANES), sup_idx, pipeline_mode=once),
        scratch_shapes=[tile, pltpu.VMEM((2, GROUP_ROWS, d), BF16), tile],
    )
    lists = tok.reshape(n_sup * N_EXPERTS, 1, n_list)
    return pl.pallas_call(
        _moe_kernel,
        grid_spec=grid_spec,
        out_shape=jax.ShapeDtypeStruct((rows, LANES), F32),
        compiler_params=pltpu.CompilerParams(dimension_semantics=("arbitrary", "arbitrary"),
                                             vmem_limit_bytes=VMEM_LIMIT),
        name="moe",
    )(counts, gstart.astype(jnp.int32), lists, lists, lists,
      wts.reshape(n_sup * N_EXPERTS, n_list // EXPERT_BLOCK, 1, EXPERT_BLOCK), h2rows, wg, wu, wd)


def _routed_rows_to_tile(r_ref, tm):
    return jnp.concatenate([r_ref[pl.ds(j, tm, stride=ROW_TILE), :] for j in range(ROW_TILE)], axis=1)


def _final_kernel(base_ref, r_ref, gate_ref, fg_ref, o_ref):
    tm = base_ref.shape[0]
    x = base_ref[...] + gate_ref[0] * _routed_rows_to_tile(r_ref, tm)
    o_ref[...] = _rms(x) * fg_ref[...]


def _final(base, routed, gate, final_gain, seq):
    t, d = base.shape
    tm = 512
    tiles_per_seq = seq // tm
    return pl.pallas_call(
        _final_kernel,
        grid=(t // tm,),
        in_specs=[pl.BlockSpec((tm, d), lambda i: (i, 0)),
                  pl.BlockSpec((tm * ROW_TILE, LANES), lambda i: (i, 0)),
                  pl.BlockSpec((1, 1, d), lambda i: (i // tiles_per_seq, 0, 0)),
                  pl.BlockSpec((1, d), lambda i: (0, 0))],
        out_specs=pl.BlockSpec((tm, d), lambda i: (i, 0)),
        out_shape=jax.ShapeDtypeStruct((t, d), F32),
        compiler_params=pltpu.CompilerParams(dimension_semantics=("arbitrary",),
                                             vmem_limit_bytes=VMEM_LIMIT),
        name="final_norm",
    )(base, routed, gate, final_gain)


def _prep_layer(w_in, w_pool, pool_scale, w_spatial, b_spatial, g_q, w_uq, g_kv, w_ukv, w_out,
                w_router, router_bias, ws_gate, ws_up, ws_down):
    d = w_in.shape[0]
    o_kpe = POOL_WIDTH + 2 * SG_WIDTH + Q_LORA + KV_LORA
    x1 = w_in[:, o_kpe:o_kpe + HALF_ROPE]
    x2 = w_in[:, o_kpe + HALF_ROPE:o_kpe + QK_ROPE]
    zl = jnp.zeros((d, QK_NOPE), F32)
    zr = jnp.zeros((d, HEAD_PAD - QK_NOPE - QK_ROPE), F32)
    w_in_pad = jnp.concatenate([w_in[:, :o_kpe], zl, x1, x2, zr, zl, x2, x1, zr], axis=1)

    uq = w_uq.reshape(Q_LORA, MLA_HEADS, QK_NOPE + QK_ROPE)
    q1 = uq[..., QK_NOPE:QK_NOPE + HALF_ROPE]
    q2 = uq[..., QK_NOPE + HALF_ROPE:]
    zq = jnp.zeros((Q_LORA, MLA_HEADS, HEAD_PAD - QK_NOPE - QK_ROPE), F32)
    w_uq_pad = jnp.concatenate([uq, zq], axis=-1).reshape(Q_LORA, MLA_HEADS * HEAD_PAD)
    w_uq_sw = jnp.concatenate([jnp.zeros_like(uq[..., :QK_NOPE]), q2, q1, zq], axis=-1)
    w_uq_sw = w_uq_sw.reshape(Q_LORA, MLA_HEADS * HEAD_PAD)

    ukv = w_ukv.reshape(KV_LORA, MLA_HEADS, QK_NOPE + V_DIM)
    zk = jnp.zeros((KV_LORA, MLA_HEADS, HEAD_PAD - QK_NOPE), F32)
    w_k = jnp.concatenate([ukv[..., :QK_NOPE], zk], axis=-1).reshape(KV_LORA, MLA_HEADS * HEAD_PAD)
    w_v = ukv[..., QK_NOPE:].reshape(KV_LORA, MLA_HEADS * V_DIM)

    w_pool_bd = jax.scipy.linalg.block_diag(*[w_pool[g] for g in range(len(POOL_WINDOWS))])
    b_sp = jnp.repeat(b_spatial.T, SG_DIM, axis=1)
    return {
        "w_in": w_in_pad.astype(BF16), "w_pool": w_pool_bd.astype(BF16),
        "pool_scale": pool_scale.reshape(1, -1), "w_spatial": w_spatial, "b_spatial": b_sp,
        "g_q": g_q.reshape(1, -1), "w_uq": w_uq_pad.astype(BF16), "w_uq_sw": w_uq_sw.astype(BF16),
        "g_kv": g_kv.reshape(1, -1), "w_k": w_k.astype(BF16), "w_v": w_v.astype(BF16),
        "w_out": w_out.astype(BF16), "w_router_t": w_router.T, "router_bias": router_bias.reshape(-1, 1),
        "ws_gate": ws_gate.astype(BF16), "ws_up": ws_up.astype(BF16), "ws_down": ws_down.astype(BF16),
    }


def kernel(x, c, positions, w_ada, b_ada, w_in, w_pool, pool_scale, w_spatial, b_spatial, g_q, w_uq, g_kv, w_ukv, w_out, w_router, router_bias, w_gate, w_up, w_down, ws_gate, ws_up, ws_down, final_gain):
    batch, seq, d = x.shape
    depth = w_ada.shape[0]
    mod = _ada_mod(c, w_ada, b_ada)
    cos_t, sin_t = _rope_tables(positions)
    x_parts = (x.reshape(batch * seq, d),)
    fg = final_gain.reshape(1, d)
    wg_b, wu_b, wd_b = w_gate.astype(BF16), w_up.astype(BF16), w_down.astype(BF16)
    ts = min(MOE_TOKENS, batch * seq)
    for l in range(depth):
        p = _prep_layer(w_in[l], w_pool[l], pool_scale[l], w_spatial[l], b_spatial[l], g_q[l], w_uq[l],
                        g_kv[l], w_ukv[l], w_out[l], w_router[l], router_bias[l], ws_gate[l], ws_up[l],
                        ws_down[l])
        shift1, scale1, gate1, shift2, scale2, gate2 = [
            mod[l, :, k * d:(k + 1) * d].reshape(batch, 1, d) for k in range(6)]
        outs = _mixer_in(x_parts, shift1, scale1, cos_t, sin_t, p, seq)
        mix, q, k, v = outs[:4]
        xt = outs[4] if len(outs) == 5 else x_parts[0]
        att = _attention(q, k, v, batch, seq)
        base, h2rows, cw_t = _mixer_out(mix, att, xt, gate1, shift2, scale2, gate2, p, seq)
        tok, wts, cnt = _plan(cw_t, ts)
        routed = _moe(h2rows, tok, wts, cnt[:, :, 0], wg_b, wu_b, wd_b, l, ts)
        x_parts = (base, routed, gate2)
    return _final(*x_parts, fg, seq).reshape(batch, seq, d)
```

```python
import functools

import jax
import jax.numpy as jnp
from jax import lax
from jax.experimental import pallas as pl
from jax.experimental.pallas import tpu as pltpu

F32 = jnp.float32
BF16 = jnp.bfloat16

EPS = 1e-6
LANES = 128
POOL_WINDOWS = (2, 4, 8, 16)
POOL_WIDTH = 256
POOL_CH = 64
MAX_WINDOW = 16
SG_HEADS = 4
SG_WIDTH = 256
SG_DIM = 64
SG_CHUNK = 128
MLA_HEADS = 8
V_DIM = 64
QK_NOPE = 64
QK_ROPE = 32
HALF_ROPE = QK_ROPE // 2
Q_LORA = 256
KV_LORA = 128
ROPE_THETA = 10000.0
ATTN_SCALE = (QK_NOPE + QK_ROPE) ** -0.5
HEAD_PAD = 128
N_EXPERTS = 64
TOP_K = 8
N_EXPERT_GROUPS = 8
GROUP_SIZE = N_EXPERTS // N_EXPERT_GROUPS
TOPK_GROUPS = 4
ROUTED_SCALE = 2.5
VMEM_LIMIT = 52 * 1024 * 1024


def _dot(a, b):
    return jnp.dot(a, b, preferred_element_type=F32)


def _dot_nt(a, b):
    return lax.dot_general(a, b, (((1,), (1,)), ((), ())), preferred_element_type=F32)


def _rms(x):
    return x * lax.rsqrt(jnp.mean(x * x, axis=-1, keepdims=True) + EPS)


def _gelu_tanh(x):
    c = (2.0 / jnp.pi) ** 0.5
    return x * (0.5 * (1.0 + jnp.tanh(c * (x + 0.044715 * (x * x * x)))))


def _silu(x):
    return x * jax.nn.sigmoid(x)


def _ada_kernel(c_ref, w_ref, b_ref, o_ref):
    cond = _silu(c_ref[...])
    o_ref[0] = _dot(cond.astype(BF16), w_ref[0].astype(BF16)) + b_ref[0]


def _ada_mod(c, w_ada, b_ada):
    depth, d, n = w_ada.shape
    b = c.shape[0]
    nt = 1536
    return pl.pallas_call(
        _ada_kernel,
        grid=(depth, n // nt),
        in_specs=[pl.BlockSpec((b, d), lambda l, j: (0, 0)),
                  pl.BlockSpec((1, d, nt), lambda l, j: (l, 0, j)),
                  pl.BlockSpec((1, 1, nt), lambda l, j: (l, 0, j))],
        out_specs=pl.BlockSpec((1, b, nt), lambda l, j: (l, 0, j)),
        out_shape=jax.ShapeDtypeStruct((depth, b, n), F32),
        compiler_params=pltpu.CompilerParams(vmem_limit_bytes=VMEM_LIMIT),
        name="ada_mod",
    )(c, w_ada, b_ada.reshape(depth, 1, n))


def _rope_kernel(pos_ref, invf_ref, sign_ref, cos_ref, sin_ref):
    ang = pos_ref[...].astype(F32) * invf_ref[...]
    cos_ref[...] = jnp.cos(ang)
    sin_ref[...] = jnp.sin(ang) * sign_ref[...]


def _rope_tables(positions):
    t = positions.size
    tm = 2048
    inv_freq = ROPE_THETA ** (-jnp.arange(0, QK_ROPE, 2, dtype=F32) / QK_ROPE)
    invf = jnp.zeros((1, HEAD_PAD), F32)
    invf = invf.at[0, QK_NOPE:QK_NOPE + HALF_ROPE].set(inv_freq)
    invf = invf.at[0, QK_NOPE + HALF_ROPE:QK_NOPE + QK_ROPE].set(inv_freq)
    sign = jnp.zeros((1, HEAD_PAD), F32)
    sign = sign.at[0, QK_NOPE:QK_NOPE + HALF_ROPE].set(-1.0)
    sign = sign.at[0, QK_NOPE + HALF_ROPE:QK_NOPE + QK_ROPE].set(1.0)
    return pl.pallas_call(
        _rope_kernel,
        grid=(t // tm,),
        in_specs=[pl.BlockSpec((tm, 1), lambda i: (i, 0)),
                  pl.BlockSpec((1, HEAD_PAD), lambda i: (0, 0)),
                  pl.BlockSpec((1, HEAD_PAD), lambda i: (0, 0))],
        out_specs=[pl.BlockSpec((tm, HEAD_PAD), lambda i: (i, 0)),
                   pl.BlockSpec((tm, HEAD_PAD), lambda i: (i, 0))],
        out_shape=[jax.ShapeDtypeStruct((t, HEAD_PAD), F32)] * 2,
        name="rope_tables",
    )(positions.reshape(t, 1), invf, sign)


def _mixer_in_kernel(*refs, tiles_per_seq, fused):
    if fused:
        base_ref, r_ref, gprev_ref = refs[:3]
        (shift_ref, scale_ref, win_ref, wpool_ref, pscale_ref, wsp_ref, bsp_ref, gq_ref, wuq_ref, wuqs_ref,
         gkv_ref, wk_ref, wv_ref, cos_ref, sin_ref, mix_ref, q_ref, k_ref, v_ref, x_out_ref, carry_ref) = refs[3:]
        tm = base_ref.shape[0]
        x = base_ref[...] + gprev_ref[0] * jnp.concatenate(
            [r_ref[pl.ds(j, tm, stride=ROW_TILE), :] for j in range(ROW_TILE)], axis=1)
        x_out_ref[...] = x
    else:
        (x_ref, shift_ref, scale_ref, win_ref, wpool_ref, pscale_ref, wsp_ref, bsp_ref, gq_ref, wuq_ref, wuqs_ref,
         gkv_ref, wk_ref, wv_ref, cos_ref, sin_ref, mix_ref, q_ref, k_ref, v_ref, carry_ref) = refs
        tm = x_ref.shape[0]
        x = x_ref[...]
    ti = pl.program_id(0) % tiles_per_seq
    h = _rms(x) * (1.0 + scale_ref[0]) + shift_ref[0]
    z = _dot(h.astype(BF16), win_ref[...])

    a = z[:, 0:POOL_WIDTH]

    @pl.when(ti == 0)
    def _():
        carry_ref[...] = jnp.zeros_like(carry_ref)

    ext = jnp.concatenate([carry_ref[...], a], axis=0)
    carry_ref[...] = a[tm - MAX_WINDOW:, :]
    p1 = ext + pltpu.roll(ext, 1, 0)
    p2 = p1 + pltpu.roll(p1, 2, 0)
    p3 = p2 + pltpu.roll(p2, 4, 0)
    p4 = p3 + pltpu.roll(p3, 8, 0)
    lane = lax.broadcasted_iota(jnp.int32, (tm, POOL_WIDTH), 1)
    row = lax.broadcasted_iota(jnp.int32, (tm, POOL_WIDTH), 0) + (ti * tm + 1)
    g0, g1, g2 = lane < POOL_CH, lane < 2 * POOL_CH, lane < 3 * POOL_CH
    wsum = jnp.where(g0, p1[MAX_WINDOW:], jnp.where(g1, p2[MAX_WINDOW:],
                     jnp.where(g2, p3[MAX_WINDOW:], p4[MAX_WINDOW:])))
    width = jnp.where(g0, POOL_WINDOWS[0], jnp.where(g1, POOL_WINDOWS[1],
                      jnp.where(g2, POOL_WINDOWS[2], POOL_WINDOWS[3])))
    cnt = jnp.minimum(row, width).astype(F32)
    dlt = wsum / cnt - a
    y_pool = _dot(dlt.astype(BF16), wpool_ref[...]) * pscale_ref[...]
    mix_ref[:, 0:POOL_WIDTH] = y_pool.astype(mix_ref.dtype)

    ug = _gelu_tanh(z[:, POOL_WIDTH:POOL_WIDTH + SG_WIDTH])
    vg = _gelu_tanh(z[:, POOL_WIDTH + SG_WIDTH:POOL_WIDTH + 2 * SG_WIDTH])
    mu = jnp.mean(vg, axis=-1, keepdims=True)
    vc = vg - mu
    var = jnp.mean(vc * vc, axis=-1, keepdims=True)
    vn = (vc * lax.rsqrt(var + EPS)).astype(BF16)
    r_i = lax.broadcasted_iota(jnp.int32, (SG_CHUNK, SG_CHUNK), 0)
    c_i = lax.broadcasted_iota(jnp.int32, (SG_CHUNK, SG_CHUNK), 1)
    wms = [jnp.where(c_i <= r_i, wsp_ref[hh], 0.0).astype(BF16) for hh in range(SG_HEADS)]
    lane_head = lax.broadcasted_iota(jnp.int32, (SG_CHUNK, SG_WIDTH), 1) // SG_DIM
    for cidx in range(tm // SG_CHUNK):
        rows = slice(cidx * SG_CHUNK, (cidx + 1) * SG_CHUNK)
        vchunk = vn[rows]
        mixed = bsp_ref[...]
        for hh in range(SG_HEADS):
            mixed = mixed + jnp.where(lane_head == hh, _dot(wms[hh], vchunk), 0.0)
        mix_ref[rows, POOL_WIDTH:POOL_WIDTH + SG_WIDTH] = (ug[rows] * mixed).astype(mix_ref.dtype)

    o_cq = POOL_WIDTH + 2 * SG_WIDTH
    o_ckv = o_cq + Q_LORA
    o_kpe = o_ckv + KV_LORA
    cos = cos_ref[...]
    sin = sin_ref[...]
    cqn = (_rms(z[:, o_cq:o_ckv]) * gq_ref[...]).astype(BF16)
    q = _dot(cqn, wuq_ref[...])
    qs = _dot(cqn, wuqs_ref[...])
    ckvn = (_rms(z[:, o_ckv:o_kpe]) * gkv_ref[...]).astype(BF16)
    kn = _dot(ckvn, wk_ref[...])
    kpe = z[:, o_kpe:o_kpe + HEAD_PAD] * cos + z[:, o_kpe + HEAD_PAD:o_kpe + 2 * HEAD_PAD] * sin
    for hh in range(MLA_HEADS):
        blk = slice(hh * HEAD_PAD, (hh + 1) * HEAD_PAD)
        q_ref[:, blk] = ((q[:, blk] * cos + qs[:, blk] * sin) * ATTN_SCALE).astype(q_ref.dtype)
        k_ref[:, blk] = (kn[:, blk] + kpe).astype(k_ref.dtype)
    v_ref[...] = _dot(ckvn, wv_ref[...]).astype(v_ref.dtype)


def _mixer_in(x_parts, shift, scale, cos_t, sin_t, p, seq):
    fused = len(x_parts) == 3
    t, d = x_parts[0].shape
    tm = 512
    tiles_per_seq = seq // tm
    nz = p["w_in"].shape[1]
    hq = MLA_HEADS * HEAD_PAD

    def full(shape):
        return pl.BlockSpec(shape, lambda i: (0,) * len(shape))

    def mod():
        return pl.BlockSpec((1, 1, d), lambda i: (i // tiles_per_seq, 0, 0))

    x_specs = [pl.BlockSpec((tm, d), lambda i: (i, 0))]
    extra_out_specs, extra_out_shapes = [], []
    if fused:
        x_specs += [pl.BlockSpec((tm * ROW_TILE, LANES), lambda i: (i, 0)), mod()]
        extra_out_specs = [pl.BlockSpec((tm, d), lambda i: (i, 0))]
        extra_out_shapes = [jax.ShapeDtypeStruct((t, d), F32)]
    return pl.pallas_call(
        functools.partial(_mixer_in_kernel, tiles_per_seq=tiles_per_seq, fused=fused),
        grid=(t // tm,),
        in_specs=x_specs + [mod(), mod(),
                  full((d, nz)), full((POOL_WIDTH, POOL_WIDTH)), full((1, POOL_WIDTH)),
                  full((SG_HEADS, SG_CHUNK, SG_CHUNK)), full((SG_CHUNK, SG_WIDTH)),
                  full((1, Q_LORA)), full((Q_LORA, hq)), full((Q_LORA, hq)),
                  full((1, KV_LORA)), full((KV_LORA, hq)), full((KV_LORA, MLA_HEADS * V_DIM)),
                  pl.BlockSpec((tm, HEAD_PAD), lambda i: (i, 0)),
                  pl.BlockSpec((tm, HEAD_PAD), lambda i: (i, 0))],
        out_specs=[pl.BlockSpec((tm, POOL_WIDTH + SG_WIDTH), lambda i: (i, 0)),
                   pl.BlockSpec((tm, hq), lambda i: (i, 0)),
                   pl.BlockSpec((tm, hq), lambda i: (i, 0)),
                   pl.BlockSpec((tm, MLA_HEADS * V_DIM), lambda i: (i, 0))] + extra_out_specs,
        out_shape=[jax.ShapeDtypeStruct((t, POOL_WIDTH + SG_WIDTH), BF16),
                   jax.ShapeDtypeStruct((t, hq), BF16),
                   jax.ShapeDtypeStruct((t, hq), BF16),
                   jax.ShapeDtypeStruct((t, MLA_HEADS * V_DIM), BF16)] + extra_out_shapes,
        scratch_shapes=[pltpu.VMEM((MAX_WINDOW, POOL_WIDTH), F32)],
        compiler_params=pltpu.CompilerParams(dimension_semantics=("arbitrary",),
                                             vmem_limit_bytes=VMEM_LIMIT),
        name="mixer_in",
    )(*x_parts, shift, scale, p["w_in"], p["w_pool"], p["pool_scale"], p["w_spatial"], p["b_spatial"],
      p["g_q"], p["w_uq"], p["w_uq_sw"], p["g_kv"], p["w_k"], p["w_v"], cos_t, sin_t)


ATTN_TQ = 256


def _attn_kernel(q_ref, k_ref, v_ref, o_ref):
    seq = q_ref.shape[0]
    tq = ATTN_TQ
    r_i = lax.broadcasted_iota(jnp.int32, (tq, tq), 0)
    c_i = lax.broadcasted_iota(jnp.int32, (tq, tq), 1)
    lane = lax.broadcasted_iota(jnp.int32, (tq, 2 * V_DIM), 1)
    for i in range(seq // tq):
        lo, hi = i * tq, (i + 1) * tq
        outs = []
        for hh in range(2):
            blk = slice(hh * HEAD_PAD, (hh + 1) * HEAD_PAD)
            q = q_ref[lo:hi, blk]
            s_d = jnp.where(c_i <= r_i, _dot_nt(q, k_ref[lo:hi, blk]), -jnp.inf)
            m = jnp.max(s_d, axis=-1, keepdims=True)
            if i > 0:
                s_o = _dot_nt(q, k_ref[0:lo, blk])
                m = jnp.maximum(m, jnp.max(s_o, axis=-1, keepdims=True))
                p_o = jnp.exp(s_o - m)
            p_d = jnp.exp(s_d - m)
            l = jnp.sum(p_d, axis=-1, keepdims=True)
            acc = _dot(p_d.astype(BF16), v_ref[lo:hi, :])
            if i > 0:
                l = l + jnp.sum(p_o, axis=-1, keepdims=True)
                acc = acc + _dot(p_o.astype(BF16), v_ref[0:lo, :])
            outs.append(acc * (1.0 / l))
        o_ref[lo:hi, :] = jnp.where(lane < V_DIM, outs[0], outs[1]).astype(o_ref.dtype)


def _attention(q, k, v, batch, seq):
    t = q.shape[0]
    return pl.pallas_call(
        _attn_kernel,
        grid=(batch, MLA_HEADS // 2),
        in_specs=[pl.BlockSpec((seq, 2 * HEAD_PAD), lambda b, hp: (b, hp)),
                  pl.BlockSpec((seq, 2 * HEAD_PAD), lambda b, hp: (b, hp)),
                  pl.BlockSpec((seq, 2 * V_DIM), lambda b, hp: (b, hp))],
        out_specs=pl.BlockSpec((seq, 2 * V_DIM), lambda b, hp: (b, hp)),
        out_shape=jax.ShapeDtypeStruct((t, MLA_HEADS * V_DIM), BF16),
        compiler_params=pltpu.CompilerParams(
            dimension_semantics=("arbitrary", "arbitrary"), vmem_limit_bytes=VMEM_LIMIT),
        name="attention",
    )(q, k, v)


def _first_max_index(cur, idx, sentinel):
    m = jnp.max(cur, axis=0, keepdims=True)
    first = jnp.min(jnp.where(cur == m, idx, sentinel), axis=0, keepdims=True)
    return m, first


def _route(logits_t, bias_t):
    n_tok = logits_t.shape[1]
    scores = jax.nn.sigmoid(logits_t)
    sel = scores + bias_t
    neg = -jnp.inf
    sub = lax.broadcasted_iota(jnp.int32, (GROUP_SIZE, n_tok), 0).astype(F32)
    gid = lax.broadcasted_iota(jnp.int32, (N_EXPERT_GROUPS, n_tok), 0).astype(F32)
    gscore = jnp.zeros((N_EXPERT_GROUPS, n_tok), F32)
    for g in range(N_EXPERT_GROUPS):
        s = sel[g * GROUP_SIZE:(g + 1) * GROUP_SIZE]
        m1, i1 = _first_max_index(s, sub, float(GROUP_SIZE))
        m2 = jnp.max(jnp.where(sub == i1, neg, s), axis=0, keepdims=True)
        gscore = jnp.where(gid == float(g), m1 + m2, gscore)
    eid = lax.broadcasted_iota(jnp.int32, (N_EXPERTS, n_tok), 0).astype(F32)
    egroup = jnp.floor(eid * (1.0 / GROUP_SIZE))
    allowed = jnp.zeros((N_EXPERTS, n_tok), F32)
    cur = gscore
    for _ in range(TOPK_GROUPS):
        _, gi = _first_max_index(cur, gid, float(N_EXPERT_GROUPS))
        cur = jnp.where(gid == gi, neg, cur)
        allowed = jnp.where(egroup == gi, 1.0, allowed)
    cur = jnp.where(allowed > 0.0, sel, neg)
    chosen = jnp.zeros((N_EXPERTS, n_tok), F32)
    for _ in range(TOP_K):
        _, ei = _first_max_index(cur, eid, float(N_EXPERTS))
        hit = eid == ei
        cur = jnp.where(hit, neg, cur)
        chosen = jnp.where(hit, 1.0, chosen)
    w = jnp.where(chosen > 0.0, scores, 0.0)
    return w / jnp.sum(w, axis=0, keepdims=True) * ROUTED_SCALE


def _split_bf16(x):
    hi = x.astype(BF16)
    lo = (x - hi.astype(F32)).astype(BF16)
    return hi, lo


ROW_TILE = 8


def _mixer_out_kernel(mix_ref, att_ref, x_ref, gate1_ref, shift_ref, scale_ref, gate2_ref, wo_ref, wr_ref,
                      rb_ref, sg_ref, su_ref, sd_ref, base_ref, h2r_ref, cw_ref):
    tm = x_ref.shape[0]
    half = mix_ref.shape[1]
    y = _dot(mix_ref[...], wo_ref[0:half, :]) + _dot(att_ref[...], wo_ref[half:, :])
    x2 = x_ref[...] + gate1_ref[0] * y
    h2 = _rms(x2) * (1.0 + scale_ref[0]) + shift_ref[0]
    for j in range(ROW_TILE):
        h2r_ref[pl.ds(j, tm, stride=ROW_TILE), :] = h2[:, j * LANES:(j + 1) * LANES]
    hb = h2.astype(BF16)
    act = _silu(_dot(hb, sg_ref[...])) * _dot(hb, su_ref[...])
    base_ref[...] = x2 + gate2_ref[0] * _dot(act.astype(BF16), sd_ref[...])
    h_hi, h_lo = _split_bf16(h2)
    w_hi, w_lo = _split_bf16(wr_ref[...])
    logits_t = _dot_nt(w_hi, h_hi) + (_dot_nt(w_hi, h_lo) + _dot_nt(w_lo, h_hi))
    cw_ref[...] = _route(logits_t, rb_ref[...])


def _mixer_out(mix, att, x2d, gate1, shift, scale, gate2, p, seq):
    t, d = x2d.shape
    tm = 512
    tiles_per_seq = seq // tm
    ff = p["ws_gate"].shape[1]

    def full(shape):
        return pl.BlockSpec(shape, lambda i: (0,) * len(shape))

    def mod():
        return pl.BlockSpec((1, 1, d), lambda i: (i // tiles_per_seq, 0, 0))

    return pl.pallas_call(
        _mixer_out_kernel,
        grid=(t // tm,),
        in_specs=[pl.BlockSpec((tm, mix.shape[1]), lambda i: (i, 0)),
                  pl.BlockSpec((tm, att.shape[1]), lambda i: (i, 0)),
                  pl.BlockSpec((tm, d), lambda i: (i, 0)), mod(), mod(), mod(), mod(),
                  full((d, d)), full((N_EXPERTS, d)), full((N_EXPERTS, 1)),
                  full((d, ff)), full((d, ff)), full((ff, d))],
        out_specs=[pl.BlockSpec((tm, d), lambda i: (i, 0)),
                   pl.BlockSpec((tm * ROW_TILE, LANES), lambda i: (i, 0)),
                   pl.BlockSpec((N_EXPERTS, tm), lambda i: (0, i))],
        out_shape=[jax.ShapeDtypeStruct((t, d), F32),
                   jax.ShapeDtypeStruct((t * ROW_TILE, LANES), F32),
                   jax.ShapeDtypeStruct((N_EXPERTS, t), F32)],
        compiler_params=pltpu.CompilerParams(dimension_semantics=("arbitrary",),
                                             vmem_limit_bytes=VMEM_LIMIT),
        name="mixer_out",
    )(mix, att, x2d, gate1, shift, scale, gate2, p["w_out"], p["w_router_t"], p["router_bias"],
      p["ws_gate"], p["ws_up"], p["ws_down"])


EXPERT_BLOCK = 128
TOK_BITS = 12
RMW_BATCH = 8
MOE_TOKENS = 4096
MOE_GROUP = 5
GROUP_ROWS = MOE_GROUP * EXPERT_BLOCK


def _list_len(ts):
    return -(-ts // GROUP_ROWS) * GROUP_ROWS


def _plan_kernel(cw_ref, tok_ref, w_ref, cnt_ref):
    n_e, ts = cw_ref.shape
    cw = cw_ref[...]
    chosen = cw > 0.0
    cf = jnp.where(chosen, 1.0, 0.0).astype(BF16)
    r_i = lax.broadcasted_iota(jnp.int32, (LANES, LANES), 0)
    c_i = lax.broadcasted_iota(jnp.int32, (LANES, LANES), 1)
    before = jnp.where(r_i < c_i, 1.0, 0.0).astype(BF16)
    ones = jnp.ones((LANES, LANES), BF16)
    carry = jnp.zeros((n_e, LANES), F32)
    ranks = []
    for k in range(ts // LANES):
        ck = cf[:, k * LANES:(k + 1) * LANES]
        ranks.append(_dot(ck, before) + carry)
        carry = carry + _dot(ck, ones)
    rank = jnp.concatenate(ranks, axis=1).astype(jnp.int32)
    lane = lax.broadcasted_iota(jnp.int32, (n_e, ts), 1)
    packed = jnp.where(chosen, ((lane - rank) << TOK_BITS) | lane, -1)
    w = jnp.where(chosen, cw, 0.0)
    for bit in range(ts.bit_length() - 1):
        step = 1 << bit
        src_p = pltpu.roll(packed, ts - step, 1)
        src_w = pltpu.roll(w, ts - step, 1)
        take = (src_p >= 0) & (lane < ts - step) & (((src_p >> (TOK_BITS + bit)) & 1) == 1)
        keep = (packed >= 0) & (((packed >> (TOK_BITS + bit)) & 1) == 0)
        packed = jnp.where(take, src_p, jnp.where(keep, packed, -1))
        w = jnp.where(take, src_w, jnp.where(keep, w, 0.0))
    count = carry[:, 0:1].astype(jnp.int32)
    valid = packed >= 0
    tok = packed & ((1 << TOK_BITS) - 1)
    tok_last = jnp.max(jnp.where(valid, tok, 0).astype(F32), axis=1, keepdims=True).astype(jnp.int32)
    w_last = jnp.sum(jnp.where(lane == count - 1, w, 0.0), axis=1, keepdims=True)
    batch_end = (count + (RMW_BATCH - 1)) & (-RMW_BATCH)
    tok = jnp.where(valid, tok, tok_last)
    w = jnp.where(valid, w, jnp.where(lane < batch_end, w_last, 0.0))
    n_tail = tok_ref.shape[2] - ts
    if n_tail:
        tok = jnp.concatenate([tok, jnp.broadcast_to(tok_last, (n_e, n_tail))], axis=1)
        w = jnp.concatenate([w, jnp.zeros((n_e, n_tail), F32)], axis=1)
    tok_ref[0] = tok * ROW_TILE
    w_ref[0] = w
    cnt_ref[0] = jnp.broadcast_to(count, (n_e, LANES))


def _plan(cw_t, ts):
    n_e, t = cw_t.shape
    n_sup = t // ts
    n_list = _list_len(ts)
    assert ts <= (1 << TOK_BITS)
    return pl.pallas_call(
        _plan_kernel,
        grid=(n_sup,),
        in_specs=[pl.BlockSpec((n_e, ts), lambda s: (0, s))],
        out_specs=[pl.BlockSpec((1, n_e, n_list), lambda s: (s, 0, 0)),
                   pl.BlockSpec((1, n_e, n_list), lambda s: (s, 0, 0)),
                   pl.BlockSpec((1, n_e, LANES), lambda s: (s, 0, 0))],
        out_shape=[jax.ShapeDtypeStruct((n_sup, n_e, n_list), jnp.int32),
                   jax.ShapeDtypeStruct((n_sup, n_e, n_list), F32),
                   jax.ShapeDtypeStruct((n_sup, n_e, LANES), jnp.int32)],
        compiler_params=pltpu.CompilerParams(dimension_semantics=("arbitrary",),
                                             vmem_limit_bytes=VMEM_LIMIT),
        name="plan",
    )(cw_t)


XT_PITCH = GROUP_ROWS + 1


def _moe_gather(list_ref, first, h_ref, xt_ref):
    for r in range(GROUP_ROWS):
        start = pl.multiple_of(list_ref[0, 0, first + r], ROW_TILE)
        xt_ref[pl.ds(r, ROW_TILE, stride=XT_PITCH), :] = h_ref[pl.ds(start, ROW_TILE), :]


def _moe_scatter_add(list_ref, first, yt_ref, acc_ref):
    for grp in range(GROUP_ROWS // RMW_BATCH):
        rows = range(grp * RMW_BATCH, (grp + 1) * RMW_BATCH)
        starts = [pl.multiple_of(list_ref[0, 0, first + r], ROW_TILE) for r in rows]
        olds = [acc_ref[pl.ds(st, ROW_TILE), :] for st in starts]
        news = [old + yt_ref[pl.ds(r, ROW_TILE, stride=XT_PITCH), :] for r, old in zip(rows, olds)]
        for st, new in zip(starts, news):
            acc_ref[pl.ds(st, ROW_TILE), :] = new


def _moe_pack(xt_ref, xb_ref, slot):
    xb_ref[slot] = jnp.concatenate(
        [xt_ref[pl.ds(j * XT_PITCH, GROUP_ROWS), :] for j in range(ROW_TILE)], axis=1).astype(BF16)


def _moe_experts(q, slot, xb_ref, w_ref, wg_ref, wu_ref, wd_ref, yt_ref):
    x = xb_ref[slot]
    act = _silu(_dot(x, wg_ref[0, 0])) * _dot(x, wu_ref[0, 0])
    y = _dot(act.astype(BF16), wd_ref[0, 0])
    w_col = jnp.concatenate(
        [jnp.broadcast_to(w_ref[0, q * MOE_GROUP + k], (EXPERT_BLOCK, EXPERT_BLOCK)).T
         for k in range(MOE_GROUP)], axis=0)
    for j in range(ROW_TILE):
        yt_ref[pl.ds(j * XT_PITCH, GROUP_ROWS), :] = y[:, j * LANES:(j + 1) * LANES] * w_col


def _groups(count):
    return jnp.maximum((count + (GROUP_ROWS - 1)) // GROUP_ROWS, 1)


def _moe_kernel(cnt_ref, gstart_ref, tokp_ref, tok_ref, tokn_ref, w_ref, h_ref, wg_ref, wu_ref, wd_ref, acc_ref,
                xt_ref, xb_ref, yt_ref):
    s = pl.program_id(0)
    e = pl.program_id(1)
    n = _groups(cnt_ref[s, e])
    n_prev = _groups(cnt_ref[s, jnp.maximum(e - 1, 0)])
    g0 = gstart_ref[s, e]

    @pl.when(e == 0)
    def _():
        acc_ref[...] = jnp.zeros_like(acc_ref)
        yt_ref[...] = jnp.zeros_like(yt_ref)
        _moe_gather(tok_ref, 0, h_ref, xt_ref)
        _moe_pack(xt_ref, xb_ref, 0)

    def stage(q, add_ref, add_first, gather_ref, gather_first):
        slot = (g0 + q) % 2
        _moe_gather(gather_ref, gather_first, h_ref, xt_ref)
        _moe_scatter_add(add_ref, add_first, yt_ref, acc_ref)
        _moe_experts(q, slot, xb_ref, w_ref, wg_ref, wu_ref, wd_ref, yt_ref)
        _moe_pack(xt_ref, xb_ref, 1 - slot)

    prev_first = (n_prev - 1) * GROUP_ROWS

    @pl.when((n == 1) & (n_prev == 1))
    def _():
        stage(0, tokp_ref, 0, tokn_ref, 0)

    @pl.when((n == 1) & (n_prev > 1))
    def _():
        stage(0, tokp_ref, prev_first, tokn_ref, 0)

    @pl.when(n > 1)
    def _():
        stage(0, tokp_ref, prev_first, tok_ref, GROUP_ROWS)

        def middle(q, carry):
            stage(q, tok_ref, (q - 1) * GROUP_ROWS, tok_ref, (q + 1) * GROUP_ROWS)
            return carry

        lax.fori_loop(1, n - 1, middle, 0)
        stage(n - 1, tok_ref, (n - 2) * GROUP_ROWS, tokn_ref, 0)

    @pl.when(e == pl.num_programs(1) - 1)
    def _():
        _moe_scatter_add(tok_ref, (n - 1) * GROUP_ROWS, yt_ref, acc_ref)


def _moe(h2rows, tok, wts, counts, wg, wu, wd, layer, ts):
    rows = h2rows.shape[0]
    n_sup = rows // (ts * ROW_TILE)
    d, ff = wg.shape[2], wg.shape[3]
    n_list = tok.shape[2]
    assert d == ROW_TILE * LANES and n_list == _list_len(ts)

    def list_idx(shift):
        def idx(s, e, cnt, gst):
            return (s * N_EXPERTS + jnp.clip(e + shift, 0, N_EXPERTS - 1), 0, 0)
        return idx

    def sup_idx(s, e, cnt, gst):
        return (s, 0)

    def w_idx(s, e, cnt, gst):
        return (layer, e, 0, 0)

    def list_spec(shift):
        return pl.BlockSpec((1, 1, n_list), list_idx(shift), memory_space=pltpu.SMEM)

    groups = _groups(counts)
    gstart = jnp.cumsum(groups, axis=1) - groups

    once = pl.Buffered(1)
    tile = pltpu.VMEM((ROW_TILE * XT_PITCH, LANES), F32)
    grid_spec = pltpu.PrefetchScalarGridSpec(
        num_scalar_prefetch=2,
        grid=(n_sup, N_EXPERTS),
        in_specs=[list_spec(-1), list_spec(0), list_spec(1),
                  pl.BlockSpec((1, n_list // EXPERT_BLOCK, 1, EXPERT_BLOCK),
                               lambda s, e, cnt, gst: (s * N_EXPERTS + e, 0, 0, 0)),
                  pl.BlockSpec((ts * ROW_TILE, LANES), sup_idx, pipeline_mode=once),
                  pl.BlockSpec((1, 1, d, ff), w_idx),
                  pl.BlockSpec((1, 1, d, ff), w_idx),
                  pl.BlockSpec((1, 1, ff, d), w_idx)],
        out_specs=pl.BlockSpec((ts * ROW_TILE, LANES), sup_idx, pipeline_mode=once),
        scratch_shapes=[tile, pltpu.VMEM((2, GROUP_ROWS, d), BF16), tile],
    )
    lists = tok.reshape(n_sup * N_EXPERTS, 1, n_list)
    return pl.pallas_call(
        _moe_kernel,
        grid_spec=grid_spec,
        out_shape=jax.ShapeDtypeStruct((rows, LANES), F32),
        compiler_params=pltpu.CompilerParams(dimension_semantics=("arbitrary", "arbitrary"),
                                             vmem_limit_bytes=VMEM_LIMIT),
        name="moe",
    )(counts, gstart.astype(jnp.int32), lists, lists, lists,
      wts.reshape(n_sup * N_EXPERTS, n_list // EXPERT_BLOCK, 1, EXPERT_BLOCK), h2rows, wg, wu, wd)


def _routed_rows_to_tile(r_ref, tm):
    return jnp.concatenate([r_ref[pl.ds(j, tm, stride=ROW_TILE), :] for j in range(ROW_TILE)], axis=1)


def _final_kernel(base_ref, r_ref, gate_ref, fg_ref, o_ref):
    tm = base_ref.shape[0]
    x = base_ref[...] + gate_ref[0] * _routed_rows_to_tile(r_ref, tm)
    o_ref[...] = _rms(x) * fg_ref[...]


def _final(base, routed, gate, final_gain, seq):
    t, d = base.shape
    tm = 512
    tiles_per_seq = seq // tm
    return pl.pallas_call(
        _final_kernel,
        grid=(t // tm,),
        in_specs=[pl.BlockSpec((tm, d), lambda i: (i, 0)),
                  pl.BlockSpec((tm * ROW_TILE, LANES), lambda i: (i, 0)),
                  pl.BlockSpec((1, 1, d), lambda i: (i // tiles_per_seq, 0, 0)),
                  pl.BlockSpec((1, d), lambda i: (0, 0))],
        out_specs=pl.BlockSpec((tm, d), lambda i: (i, 0)),
        out_shape=jax.ShapeDtypeStruct((t, d), F32),
        compiler_params=pltpu.CompilerParams(dimension_semantics=("arbitrary",),
                                             vmem_limit_bytes=VMEM_LIMIT),
        name="final_norm",
    )(base, routed, gate, final_gain)


def _prep_layer(w_in, w_pool, pool_scale, w_spatial, b_spatial, g_q, w_uq, g_kv, w_ukv, w_out,
                w_router, router_bias, ws_gate, ws_up, ws_down):
    d = w_in.shape[0]
    o_kpe = POOL_WIDTH + 2 * SG_WIDTH + Q_LORA + KV_LORA
    x1 = w_in[:, o_kpe:o_kpe + HALF_ROPE]
    x2 = w_in[:, o_kpe + HALF_ROPE:o_kpe + QK_ROPE]
    zl = jnp.zeros((d, QK_NOPE), F32)
    zr = jnp.zeros((d, HEAD_PAD - QK_NOPE - QK_ROPE), F32)
    w_in_pad = jnp.concatenate([w_in[:, :o_kpe], zl, x1, x2, zr, zl, x2, x1, zr], axis=1)

    uq = w_uq.reshape(Q_LORA, MLA_HEADS, QK_NOPE + QK_ROPE)
    q1 = uq[..., QK_NOPE:QK_NOPE + HALF_ROPE]
    q2 = uq[..., QK_NOPE + HALF_ROPE:]
    zq = jnp.zeros((Q_LORA, MLA_HEADS, HEAD_PAD - QK_NOPE - QK_ROPE), F32)
    w_uq_pad = jnp.concatenate([uq, zq], axis=-1).reshape(Q_LORA, MLA_HEADS * HEAD_PAD)
    w_uq_sw = jnp.concatenate([jnp.zeros_like(uq[..., :QK_NOPE]), q2, q1, zq], axis=-1)
    w_uq_sw = w_uq_sw.reshape(Q_LORA, MLA_HEADS * HEAD_PAD)

    ukv = w_ukv.reshape(KV_LORA, MLA_HEADS, QK_NOPE + V_DIM)
    zk = jnp.zeros((KV_LORA, MLA_HEADS, HEAD_PAD - QK_NOPE), F32)
    w_k = jnp.concatenate([ukv[..., :QK_NOPE], zk], axis=-1).reshape(KV_LORA, MLA_HEADS * HEAD_PAD)
    w_v = ukv[..., QK_NOPE:].reshape(KV_LORA, MLA_HEADS * V_DIM)

    w_pool_bd = jax.scipy.linalg.block_diag(*[w_pool[g] for g in range(len(POOL_WINDOWS))])
    b_sp = jnp.repeat(b_spatial.T, SG_DIM, axis=1)
    return {
        "w_in": w_in_pad.astype(BF16), "w_pool": w_pool_bd.astype(BF16),
        "pool_scale": pool_scale.reshape(1, -1), "w_spatial": w_spatial, "b_spatial": b_sp,
        "g_q": g_q.reshape(1, -1), "w_uq": w_uq_pad.astype(BF16), "w_uq_sw": w_uq_sw.astype(BF16),
        "g_kv": g_kv.reshape(1, -1), "w_k": w_k.astype(BF16), "w_v": w_v.astype(BF16),
        "w_out": w_out.astype(BF16), "w_router_t": w_router.T, "router_bias": router_bias.reshape(-1, 1),
        "ws_gate": ws_gate.astype(BF16), "ws_up": ws_up.astype(BF16), "ws_down": ws_down.astype(BF16),
    }


def kernel(x, c, positions, w_ada, b_ada, w_in, w_pool, pool_scale, w_spatial, b_spatial, g_q, w_uq, g_kv, w_ukv, w_out, w_router, router_bias, w_gate, w_up, w_down, ws_gate, ws_up, ws_down, final_gain):
    batch, seq, d = x.shape
    depth = w_ada.shape[0]
    mod = _ada_mod(c, w_ada, b_ada)
    cos_t, sin_t = _rope_tables(positions)
    x_parts = (x.reshape(batch * seq, d),)
    fg = final_gain.reshape(1, d)
    wg_b, wu_b, wd_b = w_gate.astype(BF16), w_up.astype(BF16), w_down.astype(BF16)
    ts = min(MOE_TOKENS, batch * seq)
    for l in range(depth):
        p = _prep_layer(w_in[l], w_pool[l], pool_scale[l], w_spatial[l], b_spatial[l], g_q[l], w_uq[l],
                        g_kv[l], w_ukv[l], w_out[l], w_router[l], router_bias[l], ws_gate[l], ws_up[l],
                        ws_down[l])
        shift1, scale1, gate1, shift2, scale2, gate2 = [
            mod[l, :, k * d:(k + 1) * d].reshape(batch, 1, d) for k in range(6)]
        outs = _mixer_in(x_parts, shift1, scale1, cos_t, sin_t, p, seq)
        mix, q, k, v = outs[:4]
        xt = outs[4] if len(outs) == 5 else x_parts[0]
        att = _attention(q, k, v, batch, seq)
        base, h2rows, cw_t = _mixer_out(mix, att, xt, gate1, shift2, scale2, gate2, p, seq)
        tok, wts, cnt = _plan(cw_t, ts)
        routed = _moe(h2rows, tok, wts, cnt[:, :, 0], wg_b, wu_b, wd_b, l, ts)
        x_parts = (base, routed, gate2)
    return _final(*x_parts, fg, seq).reshape(batch, seq, d)
```

```python
import functools

import jax
import jax.numpy as jnp
from jax import lax
from jax.experimental import pallas as pl
from jax.experimental.pallas import tpu as pltpu

F32 = jnp.float32
BF16 = jnp.bfloat16

EPS = 1e-6
LANES = 128
POOL_WINDOWS = (2, 4, 8, 16)
POOL_WIDTH = 256
POOL_CH = 64
MAX_WINDOW = 16
SG_HEADS = 4
SG_WIDTH = 256
SG_DIM = 64
SG_CHUNK = 128
MLA_HEADS = 8
V_DIM = 64
QK_NOPE = 64
QK_ROPE = 32
HALF_ROPE = QK_ROPE // 2
Q_LORA = 256
KV_LORA = 128
ROPE_THETA = 10000.0
ATTN_SCALE = (QK_NOPE + QK_ROPE) ** -0.5
HEAD_PAD = 128
N_EXPERTS = 64
TOP_K = 8
N_EXPERT_GROUPS = 8
GROUP_SIZE = N_EXPERTS // N_EXPERT_GROUPS
TOPK_GROUPS = 4
ROUTED_SCALE = 2.5
VMEM_LIMIT = 52 * 1024 * 1024


def _dot(a, b):
    return jnp.dot(a, b, preferred_element_type=F32)


def _dot_nt(a, b):
    return lax.dot_general(a, b, (((1,), (1,)), ((), ())), preferred_element_type=F32)


def _rms(x):
    return x * lax.rsqrt(jnp.mean(x * x, axis=-1, keepdims=True) + EPS)


def _gelu_tanh(x):
    c = (2.0 / jnp.pi) ** 0.5
    return x * (0.5 * (1.0 + jnp.tanh(c * (x + 0.044715 * (x * x * x)))))


def _silu(x):
    return x * jax.nn.sigmoid(x)


def _ada_kernel(c_ref, w_ref, b_ref, o_ref):
    cond = _silu(c_ref[...])
    o_ref[0] = _dot(cond.astype(BF16), w_ref[0].astype(BF16)) + b_ref[0]


def _ada_mod(c, w_ada, b_ada):
    depth, d, n = w_ada.shape
    b = c.shape[0]
    nt = 1536
    return pl.pallas_call(
        _ada_kernel,
        grid=(depth, n // nt),
        in_specs=[pl.BlockSpec((b, d), lambda l, j: (0, 0)),
                  pl.BlockSpec((1, d, nt), lambda l, j: (l, 0, j)),
                  pl.BlockSpec((1, 1, nt), lambda l, j: (l, 0, j))],
        out_specs=pl.BlockSpec((1, b, nt), lambda l, j: (l, 0, j)),
        out_shape=jax.ShapeDtypeStruct((depth, b, n), F32),
        compiler_params=pltpu.CompilerParams(vmem_limit_bytes=VMEM_LIMIT),
        name="ada_mod",
    )(c, w_ada, b_ada.reshape(depth, 1, n))


def _rope_kernel(pos_ref, invf_ref, sign_ref, cos_ref, sin_ref):
    ang = pos_ref[...].astype(F32) * invf_ref[...]
    cos_ref[...] = jnp.cos(ang)
    sin_ref[...] = jnp.sin(ang) * sign_ref[...]


def _rope_tables(positions):
    t = positions.size
    tm = 2048
    inv_freq = ROPE_THETA ** (-jnp.arange(0, QK_ROPE, 2, dtype=F32) / QK_ROPE)
    invf = jnp.zeros((1, HEAD_PAD), F32)
    invf = invf.at[0, QK_NOPE:QK_NOPE + HALF_ROPE].set(inv_freq)
    invf = invf.at[0, QK_NOPE + HALF_ROPE:QK_NOPE + QK_ROPE].set(inv_freq)
    sign = jnp.zeros((1, HEAD_PAD), F32)
    sign = sign.at[0, QK_NOPE:QK_NOPE + HALF_ROPE].set(-1.0)
    sign = sign.at[0, QK_NOPE + HALF_ROPE:QK_NOPE + QK_ROPE].set(1.0)
    return pl.pallas_call(
        _rope_kernel,
        grid=(t // tm,),
        in_specs=[pl.BlockSpec((tm, 1), lambda i: (i, 0)),
                  pl.BlockSpec((1, HEAD_PAD), lambda i: (0, 0)),
                  pl.BlockSpec((1, HEAD_PAD), lambda i: (0, 0))],
        out_specs=[pl.BlockSpec((tm, HEAD_PAD), lambda i: (i, 0)),
                   pl.BlockSpec((tm, HEAD_PAD), lambda i: (i, 0))],
        out_shape=[jax.ShapeDtypeStruct((t, HEAD_PAD), F32)] * 2,
        name="rope_tables",
    )(positions.reshape(t, 1), invf, sign)


def _mixer_in_kernel(*refs, tiles_per_seq, fused):
    if fused:
        base_ref, r_ref, gprev_ref = refs[:3]
        (shift_ref, scale_ref, win_ref, wpool_ref, pscale_ref, wsp_ref, bsp_ref, gq_ref, wuq_ref, wuqs_ref,
         gkv_ref, wk_ref, wv_ref, cos_ref, sin_ref, mix_ref, q_ref, k_ref, v_ref, x_out_ref, carry_ref) = refs[3:]
        tm = base_ref.shape[0]
        x = base_ref[...] + gprev_ref[0] * jnp.concatenate(
            [r_ref[pl.ds(j, tm, stride=ROW_TILE), :] for j in range(ROW_TILE)], axis=1)
        x_out_ref[...] = x
    else:
        (x_ref, shift_ref, scale_ref, win_ref, wpool_ref, pscale_ref, wsp_ref, bsp_ref, gq_ref, wuq_ref, wuqs_ref,
         gkv_ref, wk_ref, wv_ref, cos_ref, sin_ref, mix_ref, q_ref, k_ref, v_ref, carry_ref) = refs
        tm = x_ref.shape[0]
        x = x_ref[...]
    ti = pl.program_id(0) % tiles_per_seq
    h = _rms(x) * (1.0 + scale_ref[0]) + shift_ref[0]
    z = _dot(h.astype(BF16), win_ref[...])

    a = z[:, 0:POOL_WIDTH]

    @pl.when(ti == 0)
    def _():
        carry_ref[...] = jnp.zeros_like(carry_ref)

    ext = jnp.concatenate([carry_ref[...], a], axis=0)
    carry_ref[...] = a[tm - MAX_WINDOW:, :]
    p1 = ext + pltpu.roll(ext, 1, 0)
    p2 = p1 + pltpu.roll(p1, 2, 0)
    p3 = p2 + pltpu.roll(p2, 4, 0)
    p4 = p3 + pltpu.roll(p3, 8, 0)
    lane = lax.broadcasted_iota(jnp.int32, (tm, POOL_WIDTH), 1)
    row = lax.broadcasted_iota(jnp.int32, (tm, POOL_WIDTH), 0) + (ti * tm + 1)
    g0, g1, g2 = lane < POOL_CH, lane < 2 * POOL_CH, lane < 3 * POOL_CH
    wsum = jnp.where(g0, p1[MAX_WINDOW:], jnp.where(g1, p2[MAX_WINDOW:],
                     jnp.where(g2, p3[MAX_WINDOW:], p4[MAX_WINDOW:])))
    width = jnp.where(g0, POOL_WINDOWS[0], jnp.where(g1, POOL_WINDOWS[1],
                      jnp.where(g2, POOL_WINDOWS[2], POOL_WINDOWS[3])))
    cnt = jnp.minimum(row, width).astype(F32)
    dlt = wsum / cnt - a
    y_pool = _dot(dlt.astype(BF16), wpool_ref[...]) * pscale_ref[...]
    mix_ref[:, 0:POOL_WIDTH] = y_pool.astype(mix_ref.dtype)

    ug = _gelu_tanh(z[:, POOL_WIDTH:POOL_WIDTH + SG_WIDTH])
    vg = _gelu_tanh(z[:, POOL_WIDTH + SG_WIDTH:POOL_WIDTH + 2 * SG_WIDTH])
    mu = jnp.mean(vg, axis=-1, keepdims=True)
    vc = vg - mu
    var = jnp.mean(vc * vc, axis=-1, keepdims=True)
    vn = (vc * lax.rsqrt(var + EPS)).astype(BF16)
    r_i = lax.broadcasted_iota(jnp.int32, (SG_CHUNK, SG_CHUNK), 0)
    c_i = lax.broadcasted_iota(jnp.int32, (SG_CHUNK, SG_CHUNK), 1)
    wms = [jnp.where(c_i <= r_i, wsp_ref[hh], 0.0).astype(BF16) for hh in range(SG_HEADS)]
    lane_head = lax.broadcasted_iota(jnp.int32, (SG_CHUNK, SG_WIDTH), 1) // SG_DIM
    for cidx in range(tm // SG_CHUNK):
        rows = slice(cidx * SG_CHUNK, (cidx + 1) * SG_CHUNK)
        vchunk = vn[rows]
        mixed = bsp_ref[...]
        for hh in range(SG_HEADS):
            mixed = mixed + jnp.where(lane_head == hh, _dot(wms[hh], vchunk), 0.0)
        mix_ref[rows, POOL_WIDTH:POOL_WIDTH + SG_WIDTH] = (ug[rows] * mixed).astype(mix_ref.dtype)

    o_cq = POOL_WIDTH + 2 * SG_WIDTH
    o_ckv = o_cq + Q_LORA
    o_kpe = o_ckv + KV_LORA
    cos = cos_ref[...]
    sin = sin_ref[...]
    cqn = (_rms(z[:, o_cq:o_ckv]) * gq_ref[...]).astype(BF16)
    q = _dot(cqn, wuq_ref[...])
    qs = _dot(cqn, wuqs_ref[...])
    ckvn = (_rms(z[:, o_ckv:o_kpe]) * gkv_ref[...]).astype(BF16)
    kn = _dot(ckvn, wk_ref[...])
    kpe = z[:, o_kpe:o_kpe + HEAD_PAD] * cos + z[:, o_kpe + HEAD_PAD:o_kpe + 2 * HEAD_PAD] * sin
    for hh in range(MLA_HEADS):
        blk = slice(hh * HEAD_PAD, (hh + 1) * HEAD_PAD)
        q_ref[:, blk] = ((q[:, blk] * cos + qs[:, blk] * sin) * ATTN_SCALE).astype(q_ref.dtype)
        k_ref[:, blk] = (kn[:, blk] + kpe).astype(k_ref.dtype)
    v_ref[...] = _dot(ckvn, wv_ref[...]).astype(v_ref.dtype)


def _mixer_in(x_parts, shift, scale, cos_t, sin_t, p, seq):
    fused = len(x_parts) == 3
    t, d = x_parts[0].shape
    tm = 512
    tiles_per_seq = seq // tm
    nz = p["w_in"].shape[1]
    hq = MLA_HEADS * HEAD_PAD

    def full(shape):
        return pl.BlockSpec(shape, lambda i: (0,) * len(shape))

    def mod():
        return pl.BlockSpec((1, 1, d), lambda i: (i // tiles_per_seq, 0, 0))

    x_specs = [pl.BlockSpec((tm, d), lambda i: (i, 0))]
    extra_out_specs, extra_out_shapes = [], []
    if fused:
        x_specs += [pl.BlockSpec((tm * ROW_TILE, LANES), lambda i: (i, 0)), mod()]
        extra_out_specs = [pl.BlockSpec((tm, d), lambda i: (i, 0))]
        extra_out_shapes = [jax.ShapeDtypeStruct((t, d), F32)]
    return pl.pallas_call(
        functools.partial(_mixer_in_kernel, tiles_per_seq=tiles_per_seq, fused=fused),
        grid=(t // tm,),
        in_specs=x_specs + [mod(), mod(),
                  full((d, nz)), full((POOL_WIDTH, POOL_WIDTH)), full((1, POOL_WIDTH)),
                  full((SG_HEADS, SG_CHUNK, SG_CHUNK)), full((SG_CHUNK, SG_WIDTH)),
                  full((1, Q_LORA)), full((Q_LORA, hq)), full((Q_LORA, hq)),
                  full((1, KV_LORA)), full((KV_LORA, hq)), full((KV_LORA, MLA_HEADS * V_DIM)),
                  pl.BlockSpec((tm, HEAD_PAD), lambda i: (i, 0)),
                  pl.BlockSpec((tm, HEAD_PAD), lambda i: (i, 0))],
        out_specs=[pl.BlockSpec((tm, POOL_WIDTH + SG_WIDTH), lambda i: (i, 0)),
                   pl.BlockSpec((tm, hq), lambda i: (i, 0)),
                   pl.BlockSpec((tm, hq), lambda i: (i, 0)),
                   pl.BlockSpec((tm, MLA_HEADS * V_DIM), lambda i: (i, 0))] + extra_out_specs,
        out_shape=[jax.ShapeDtypeStruct((t, POOL_WIDTH + SG_WIDTH), BF16),
                   jax.ShapeDtypeStruct((t, hq), BF16),
                   jax.ShapeDtypeStruct((t, hq), BF16),
                   jax.ShapeDtypeStruct((t, MLA_HEADS * V_DIM), BF16)] + extra_out_shapes,
        scratch_shapes=[pltpu.VMEM((MAX_WINDOW, POOL_WIDTH), F32)],
        compiler_params=pltpu.CompilerParams(dimension_semantics=("arbitrary",),
                                             vmem_limit_bytes=VMEM_LIMIT),
        name="mixer_in",
    )(*x_parts, shift, scale, p["w_in"], p["w_pool"], p["pool_scale"], p["w_spatial"], p["b_spatial"],
      p["g_q"], p["w_uq"], p["w_uq_sw"], p["g_kv"], p["w_k"], p["w_v"], cos_t, sin_t)


ATTN_TQ = 256


def _attn_kernel(q_ref, k_ref, v_ref, o_ref):
    seq = q_ref.shape[0]
    tq = ATTN_TQ
    r_i = lax.broadcasted_iota(jnp.int32, (tq, tq), 0)
    c_i = lax.broadcasted_iota(jnp.int32, (tq, tq), 1)
    lane = lax.broadcasted_iota(jnp.int32, (tq, 2 * V_DIM), 1)
    for i in range(seq // tq):
        lo, hi = i * tq, (i + 1) * tq
        outs = []
        for hh in range(2):
            blk = slice(hh * HEAD_PAD, (hh + 1) * HEAD_PAD)
            q = q_ref[lo:hi, blk]
            s_d = jnp.where(c_i <= r_i, _dot_nt(q, k_ref[lo:hi, blk]), -jnp.inf)
            m = jnp.max(s_d, axis=-1, keepdims=True)
            if i > 0:
                s_o = _dot_nt(q, k_ref[0:lo, blk])
                m = jnp.maximum(m, jnp.max(s_o, axis=-1, keepdims=True))
                p_o = jnp.exp(s_o - m)
            p_d = jnp.exp(s_d - m)
            l = jnp.sum(p_d, axis=-1, keepdims=True)
            acc = _dot(p_d.astype(BF16), v_ref[lo:hi, :])
            if i > 0:
                l = l + jnp.sum(p_o, axis=-1, keepdims=True)
                acc = acc + _dot(p_o.astype(BF16), v_ref[0:lo, :])
            outs.append(acc * (1.0 / l))
        o_ref[lo:hi, :] = jnp.where(lane < V_DIM, outs[0], outs[1]).astype(o_ref.dtype)


def _attention(q, k, v, batch, seq):
    t = q.shape[0]
    return pl.pallas_call(
        _attn_kernel,
        grid=(batch, MLA_HEADS // 2),
        in_specs=[pl.BlockSpec((seq, 2 * HEAD_PAD), lambda b, hp: (b, hp)),
                  pl.BlockSpec((seq, 2 * HEAD_PAD), lambda b, hp: (b, hp)),
                  pl.BlockSpec((seq, 2 * V_DIM), lambda b, hp: (b, hp))],
        out_specs=pl.BlockSpec((seq, 2 * V_DIM), lambda b, hp: (b, hp)),
        out_shape=jax.ShapeDtypeStruct((t, MLA_HEADS * V_DIM), BF16),
        compiler_params=pltpu.CompilerParams(
            dimension_semantics=("arbitrary", "arbitrary"), vmem_limit_bytes=VMEM_LIMIT),
        name="attention",
    )(q, k, v)


def _first_max_index(cur, idx, sentinel):
    m = jnp.max(cur, axis=0, keepdims=True)
    first = jnp.min(jnp.where(cur == m, idx, sentinel), axis=0, keepdims=True)
    return m, first


def _route(logits_t, bias_t):
    n_tok = logits_t.shape[1]
    scores = jax.nn.sigmoid(logits_t)
    sel = scores + bias_t
    neg = -jnp.inf
    sub = lax.broadcasted_iota(jnp.int32, (GROUP_SIZE, n_tok), 0).astype(F32)
    gid = lax.broadcasted_iota(jnp.int32, (N_EXPERT_GROUPS, n_tok), 0).astype(F32)
    gscore = jnp.zeros((N_EXPERT_GROUPS, n_tok), F32)
    for g in range(N_EXPERT_GROUPS):
        s = sel[g * GROUP_SIZE:(g + 1) * GROUP_SIZE]
        m1, i1 = _first_max_index(s, sub, float(GROUP_SIZE))
        m2 = jnp.max(jnp.where(sub == i1, neg, s), axis=0, keepdims=True)
        gscore = jnp.where(gid == float(g), m1 + m2, gscore)
    eid = lax.broadcasted_iota(jnp.int32, (N_EXPERTS, n_tok), 0).astype(F32)
    egroup = jnp.floor(eid * (1.0 / GROUP_SIZE))
    allowed = jnp.zeros((N_EXPERTS, n_tok), F32)
    cur = gscore
    for _ in range(TOPK_GROUPS):
        _, gi = _first_max_index(cur, gid, float(N_EXPERT_GROUPS))
        cur = jnp.where(gid == gi, neg, cur)
        allowed = jnp.where(egroup == gi, 1.0, allowed)
    cur = jnp.where(allowed > 0.0, sel, neg)
    chosen = jnp.zeros((N_EXPERTS, n_tok), F32)
    for _ in range(TOP_K):
        _, ei = _first_max_index(cur, eid, float(N_EXPERTS))
        hit = eid == ei
        cur = jnp.where(hit, neg, cur)
        chosen = jnp.where(hit, 1.0, chosen)
    w = jnp.where(chosen > 0.0, scores, 0.0)
    return w / jnp.sum(w, axis=0, keepdims=True) * ROUTED_SCALE


def _split_bf16(x):
    hi = x.astype(BF16)
    lo = (x - hi.astype(F32)).astype(BF16)
    return hi, lo


ROW_TILE = 8


def _mixer_out_kernel(mix_ref, att_ref, x_ref, gate1_ref, shift_ref, scale_ref, gate2_ref, wo_ref, wr_ref,
                      rb_ref, sg_ref, su_ref, sd_ref, base_ref, h2r_ref, cw_ref):
    tm = x_ref.shape[0]
    half = mix_ref.shape[1]
    y = _dot(mix_ref[...], wo_ref[0:half, :]) + _dot(att_ref[...], wo_ref[half:, :])
    x2 = x_ref[...] + gate1_ref[0] * y
    h2 = _rms(x2) * (1.0 + scale_ref[0]) + shift_ref[0]
    for j in range(ROW_TILE):
        h2r_ref[pl.ds(j, tm, stride=ROW_TILE), :] = h2[:, j * LANES:(j + 1) * LANES]
    hb = h2.astype(BF16)
    act = _silu(_dot(hb, sg_ref[...])) * _dot(hb, su_ref[...])
    base_ref[...] = x2 + gate2_ref[0] * _dot(act.astype(BF16), sd_ref[...])
    h_hi, h_lo = _split_bf16(h2)
    w_hi, w_lo = _split_bf16(wr_ref[...])
    logits_t = _dot_nt(w_hi, h_hi) + (_dot_nt(w_hi, h_lo) + _dot_nt(w_lo, h_hi))
    cw_ref[...] = _route(logits_t, rb_ref[...])


def _mixer_out(mix, att, x2d, gate1, shift, scale, gate2, p, seq):
    t, d = x2d.shape
    tm = 512
    tiles_per_seq = seq // tm
    ff = p["ws_gate"].shape[1]

    def full(shape):
        return pl.BlockSpec(shape, lambda i: (0,) * len(shape))

    def mod():
        return pl.BlockSpec((1, 1, d), lambda i: (i // tiles_per_seq, 0, 0))

    return pl.pallas_call(
        _mixer_out_kernel,
        grid=(t // tm,),
        in_specs=[pl.BlockSpec((tm, mix.shape[1]), lambda i: (i, 0)),
                  pl.BlockSpec((tm, att.shape[1]), lambda i: (i, 0)),
                  pl.BlockSpec((tm, d), lambda i: (i, 0)), mod(), mod(), mod(), mod(),
                  full((d, d)), full((N_EXPERTS, d)), full((N_EXPERTS, 1)),
                  full((d, ff)), full((d, ff)), full((ff, d))],
        out_specs=[pl.BlockSpec((tm, d), lambda i: (i, 0)),
                   pl.BlockSpec((tm * ROW_TILE, LANES), lambda i: (i, 0)),
                   pl.BlockSpec((N_EXPERTS, tm), lambda i: (0, i))],
        out_shape=[jax.ShapeDtypeStruct((t, d), F32),
                   jax.ShapeDtypeStruct((t * ROW_TILE, LANES), F32),
                   jax.ShapeDtypeStruct((N_EXPERTS, t), F32)],
        compiler_params=pltpu.CompilerParams(dimension_semantics=("arbitrary",),
                                             vmem_limit_bytes=VMEM_LIMIT),
        name="mixer_out",
    )(mix, att, x2d, gate1, shift, scale, gate2, p["w_out"], p["w_router_t"], p["router_bias"],
      p["ws_gate"], p["ws_up"], p["ws_down"])


EXPERT_BLOCK = 128
TOK_BITS = 12
RMW_BATCH = 16
MOE_TOKENS = 4096
MOE_GROUP = 5
GROUP_ROWS = MOE_GROUP * EXPERT_BLOCK


def _list_len(ts):
    return -(-ts // GROUP_ROWS) * GROUP_ROWS


def _plan_kernel(cw_ref, tok_ref, w_ref, cnt_ref):
    n_e, ts = cw_ref.shape
    cw = cw_ref[...]
    chosen = cw > 0.0
    cf = jnp.where(chosen, 1.0, 0.0).astype(BF16)
    r_i = lax.broadcasted_iota(jnp.int32, (LANES, LANES), 0)
    c_i = lax.broadcasted_iota(jnp.int32, (LANES, LANES), 1)
    before = jnp.where(r_i < c_i, 1.0, 0.0).astype(BF16)
    ones = jnp.ones((LANES, LANES), BF16)
    carry = jnp.zeros((n_e, LANES), F32)
    ranks = []
    for k in range(ts // LANES):
        ck = cf[:, k * LANES:(k + 1) * LANES]
        ranks.append(_dot(ck, before) + carry)
        carry = carry + _dot(ck, ones)
    rank = jnp.concatenate(ranks, axis=1).astype(jnp.int32)
    lane = lax.broadcasted_iota(jnp.int32, (n_e, ts), 1)
    packed = jnp.where(chosen, ((lane - rank) << TOK_BITS) | lane, -1)
    w = jnp.where(chosen, cw, 0.0)
    for bit in range(ts.bit_length() - 1):
        step = 1 << bit
        src_p = pltpu.roll(packed, ts - step, 1)
        src_w = pltpu.roll(w, ts - step, 1)
        take = (src_p >= 0) & (lane < ts - step) & (((src_p >> (TOK_BITS + bit)) & 1) == 1)
        keep = (packed >= 0) & (((packed >> (TOK_BITS + bit)) & 1) == 0)
        packed = jnp.where(take, src_p, jnp.where(keep, packed, -1))
        w = jnp.where(take, src_w, jnp.where(keep, w, 0.0))
    count = carry[:, 0:1].astype(jnp.int32)
    valid = packed >= 0
    tok = packed & ((1 << TOK_BITS) - 1)
    tok_last = jnp.max(jnp.where(valid, tok, 0).astype(F32), axis=1, keepdims=True).astype(jnp.int32)
    w_last = jnp.sum(jnp.where(lane == count - 1, w, 0.0), axis=1, keepdims=True)
    batch_end = (count + (RMW_BATCH - 1)) & (-RMW_BATCH)
    tok = jnp.where(valid, tok, tok_last)
    w = jnp.where(valid, w, jnp.where(lane < batch_end, w_last, 0.0))
    n_tail = tok_ref.shape[2] - ts
    if n_tail:
        tok = jnp.concatenate([tok, jnp.broadcast_to(tok_last, (n_e, n_tail))], axis=1)
        w = jnp.concatenate([w, jnp.zeros((n_e, n_tail), F32)], axis=1)
    tok_ref[0] = tok * ROW_TILE
    w_ref[0] = w
    cnt_ref[0] = jnp.broadcast_to(count, (n_e, LANES))


def _plan(cw_t, ts):
    n_e, t = cw_t.shape
    n_sup = t // ts
    n_list = _list_len(ts)
    assert ts <= (1 << TOK_BITS)
    return pl.pallas_call(
        _plan_kernel,
        grid=(n_sup,),
        in_specs=[pl.BlockSpec((n_e, ts), lambda s: (0, s))],
        out_specs=[pl.BlockSpec((1, n_e, n_list), lambda s: (s, 0, 0)),
                   pl.BlockSpec((1, n_e, n_list), lambda s: (s, 0, 0)),
                   pl.BlockSpec((1, n_e, LANES), lambda s: (s, 0, 0))],
        out_shape=[jax.ShapeDtypeStruct((n_sup, n_e, n_list), jnp.int32),
                   jax.ShapeDtypeStruct((n_sup, n_e, n_list), F32),
                   jax.ShapeDtypeStruct((n_sup, n_e, LANES), jnp.int32)],
        compiler_params=pltpu.CompilerParams(dimension_semantics=("arbitrary",),
                                             vmem_limit_bytes=VMEM_LIMIT),
        name="plan",
    )(cw_t)


XT_PITCH = GROUP_ROWS + 1


def _moe_gather(list_ref, first, h_ref, xt_ref):
    for r in range(GROUP_ROWS):
        start = pl.multiple_of(list_ref[0, 0, first + r], ROW_TILE)
        xt_ref[pl.ds(r, ROW_TILE, stride=XT_PITCH), :] = h_ref[pl.ds(start, ROW_TILE), :]


def _moe_scatter_add(list_ref, first, yt_ref, acc_ref):
    for grp in range(GROUP_ROWS // RMW_BATCH):
        rows = range(grp * RMW_BATCH, (grp + 1) * RMW_BATCH)
        starts = [pl.multiple_of(list_ref[0, 0, first + r], ROW_TILE) for r in rows]
        olds = [acc_ref[pl.ds(st, ROW_TILE), :] for st in starts]
        news = [old + yt_ref[pl.ds(r, ROW_TILE, stride=XT_PITCH), :] for r, old in zip(rows, olds)]
        for st, new in zip(starts, news):
            acc_ref[pl.ds(st, ROW_TILE), :] = new


def _moe_experts(q, xt_ref, w_ref, wg_ref, wu_ref, wd_ref, yt_ref):
    x = jnp.concatenate(
        [xt_ref[pl.ds(j * XT_PITCH, GROUP_ROWS), :] for j in range(ROW_TILE)], axis=1).astype(BF16)
    act = _silu(_dot(x, wg_ref[0, 0])) * _dot(x, wu_ref[0, 0])
    y = _dot(act.astype(BF16), wd_ref[0, 0])
    w_col = jnp.concatenate(
        [jnp.broadcast_to(w_ref[0, q * MOE_GROUP + k], (EXPERT_BLOCK, EXPERT_BLOCK)).T
         for k in range(MOE_GROUP)], axis=0)
    for j in range(ROW_TILE):
        yt_ref[pl.ds(j * XT_PITCH, GROUP_ROWS), :] = y[:, j * LANES:(j + 1) * LANES] * w_col


def _groups(count):
    return jnp.maximum((count + (GROUP_ROWS - 1)) // GROUP_ROWS, 1)


def _moe_kernel(cnt_ref, tokp_ref, tok_ref, tokn_ref, w_ref, h_ref, wg_ref, wu_ref, wd_ref, acc_ref,
                xt0_ref, xt1_ref, yt_ref):
    s = pl.program_id(0)
    e = pl.program_id(1)
    n = _groups(cnt_ref[s, e])
    n_prev = _groups(cnt_ref[s, jnp.maximum(e - 1, 0)])
    prev_first = (n_prev - 1) * GROUP_ROWS

    @pl.when(e == 0)
    def _():
        acc_ref[...] = jnp.zeros_like(acc_ref)
        yt_ref[...] = jnp.zeros_like(yt_ref)
        _moe_gather(tok_ref, 0, h_ref, xt0_ref)

    def experts(q, xt_ref):
        _moe_experts(q, xt_ref, w_ref, wg_ref, wu_ref, wd_ref, yt_ref)

    def step(xt_cur, xt_next):
        @pl.when((n == 1) & (n_prev == 1))
        def _():
            _moe_gather(tokn_ref, 0, h_ref, xt_next)
            _moe_scatter_add(tokp_ref, 0, yt_ref, acc_ref)
            experts(0, xt_cur)

        @pl.when((n == 1) & (n_prev > 1))
        def _():
            _moe_gather(tokn_ref, 0, h_ref, xt_next)
            _moe_scatter_add(tokp_ref, prev_first, yt_ref, acc_ref)
            experts(0, xt_cur)

        @pl.when(n > 1)
        def _():
            _moe_gather(tokn_ref, 0, h_ref, xt_next)
            _moe_scatter_add(tokp_ref, prev_first, yt_ref, acc_ref)
            experts(0, xt_cur)

            def more(q, carry):
                _moe_scatter_add(tok_ref, (q - 1) * GROUP_ROWS, yt_ref, acc_ref)
                _moe_gather(tok_ref, q * GROUP_ROWS, h_ref, xt_cur)
                experts(q, xt_cur)
                return carry

            lax.fori_loop(1, n, more, 0)

    @pl.when(e % 2 == 0)
    def _():
        step(xt0_ref, xt1_ref)

    @pl.when(e % 2 == 1)
    def _():
        step(xt1_ref, xt0_ref)

    @pl.when(e == pl.num_programs(1) - 1)
    def _():
        _moe_scatter_add(tok_ref, (n - 1) * GROUP_ROWS, yt_ref, acc_ref)


def _moe(h2rows, tok, wts, counts, wg, wu, wd, layer, ts):
    rows = h2rows.shape[0]
    n_sup = rows // (ts * ROW_TILE)
    d, ff = wg.shape[2], wg.shape[3]
    n_list = tok.shape[2]
    assert d == ROW_TILE * LANES and n_list == _list_len(ts)

    assert N_EXPERTS % 2 == 0

    def list_idx(shift):
        def idx(s, e, cnt):
            return (s * N_EXPERTS + jnp.clip(e + shift, 0, N_EXPERTS - 1), 0, 0)
        return idx

    def sup_idx(s, e, cnt):
        return (s, 0)

    def w_idx(s, e, cnt):
        return (layer, e, 0, 0)

    def list_spec(shift):
        return pl.BlockSpec((1, 1, n_list), list_idx(shift), memory_space=pltpu.SMEM)

    once = pl.Buffered(1)
    tile = pltpu.VMEM((ROW_TILE * XT_PITCH, LANES), F32)
    grid_spec = pltpu.PrefetchScalarGridSpec(
        num_scalar_prefetch=1,
        grid=(n_sup, N_EXPERTS),
        in_specs=[list_spec(-1), list_spec(0), list_spec(1),
                  pl.BlockSpec((1, n_list // EXPERT_BLOCK, 1, EXPERT_BLOCK),
                               lambda s, e, cnt: (s * N_EXPERTS + e, 0, 0, 0)),
                  pl.BlockSpec((ts * ROW_TILE, LANES), sup_idx, pipeline_mode=once),
                  pl.BlockSpec((1, 1, d, ff), w_idx),
                  pl.BlockSpec((1, 1, d, ff), w_idx),
                  pl.BlockSpec((1, 1, ff, d), w_idx)],
        out_specs=pl.BlockSpec((ts * ROW_TILE, LANES), sup_idx, pipeline_mode=once),
        scratch_shapes=[tile, tile, tile],
    )
    lists = tok.reshape(n_sup * N_EXPERTS, 1, n_list)
    return pl.pallas_call(
        _moe_kernel,
        grid_spec=grid_spec,
        out_shape=jax.ShapeDtypeStruct((rows, LANES), F32),
        compiler_params=pltpu.CompilerParams(dimension_semantics=("arbitrary", "arbitrary"),
                                             vmem_limit_bytes=VMEM_LIMIT),
        name="moe",
    )(counts, lists, lists, lists,
      wts.reshape(n_sup * N_EXPERTS, n_list // EXPERT_BLOCK, 1, EXPERT_BLOCK), h2rows, wg, wu, wd)


def _routed_rows_to_tile(r_ref, tm):
    return jnp.concatenate([r_ref[pl.ds(j, tm, stride=ROW_TILE), :] for j in range(ROW_TILE)], axis=1)


def _final_kernel(base_ref, r_ref, gate_ref, fg_ref, o_ref):
    tm = base_ref.shape[0]
    x = base_ref[...] + gate_ref[0] * _routed_rows_to_tile(r_ref, tm)
    o_ref[...] = _rms(x) * fg_ref[...]


def _final(base, routed, gate, final_gain, seq):
    t, d = base.shape
    tm = 512
    tiles_per_seq = seq // tm
    return pl.pallas_call(
        _final_kernel,
        grid=(t // tm,),
        in_specs=[pl.BlockSpec((tm, d), lambda i: (i, 0)),
                  pl.BlockSpec((tm * ROW_TILE, LANES), lambda i: (i, 0)),
                  pl.BlockSpec((1, 1, d), lambda i: (i // tiles_per_seq, 0, 0)),
                  pl.BlockSpec((1, d), lambda i: (0, 0))],
        out_specs=pl.BlockSpec((tm, d), lambda i: (i, 0)),
        out_shape=jax.ShapeDtypeStruct((t, d), F32),
        compiler_params=pltpu.CompilerParams(dimension_semantics=("arbitrary",),
                                             vmem_limit_bytes=VMEM_LIMIT),
        name="final_norm",
    )(base, routed, gate, final_gain)


def _prep_layer(w_in, w_pool, pool_scale, w_spatial, b_spatial, g_q, w_uq, g_kv, w_ukv, w_out,
                w_router, router_bias, ws_gate, ws_up, ws_down):
    d = w_in.shape[0]
    o_kpe = POOL_WIDTH + 2 * SG_WIDTH + Q_LORA + KV_LORA
    x1 = w_in[:, o_kpe:o_kpe + HALF_ROPE]
    x2 = w_in[:, o_kpe + HALF_ROPE:o_kpe + QK_ROPE]
    zl = jnp.zeros((d, QK_NOPE), F32)
    zr = jnp.zeros((d, HEAD_PAD - QK_NOPE - QK_ROPE), F32)
    w_in_pad = jnp.concatenate([w_in[:, :o_kpe], zl, x1, x2, zr, zl, x2, x1, zr], axis=1)

    uq = w_uq.reshape(Q_LORA, MLA_HEADS, QK_NOPE + QK_ROPE)
    q1 = uq[..., QK_NOPE:QK_NOPE + HALF_ROPE]
    q2 = uq[..., QK_NOPE + HALF_ROPE:]
    zq = jnp.zeros((Q_LORA, MLA_HEADS, HEAD_PAD - QK_NOPE - QK_ROPE), F32)
    w_uq_pad = jnp.concatenate([uq, zq], axis=-1).reshape(Q_LORA, MLA_HEADS * HEAD_PAD)
    w_uq_sw = jnp.concatenate([jnp.zeros_like(uq[..., :QK_NOPE]), q2, q1, zq], axis=-1)
    w_uq_sw = w_uq_sw.reshape(Q_LORA, MLA_HEADS * HEAD_PAD)

    ukv = w_ukv.reshape(KV_LORA, MLA_HEADS, QK_NOPE + V_DIM)
    zk = jnp.zeros((KV_LORA, MLA_HEADS, HEAD_PAD - QK_NOPE), F32)
    w_k = jnp.concatenate([ukv[..., :QK_NOPE], zk], axis=-1).reshape(KV_LORA, MLA_HEADS * HEAD_PAD)
    w_v = ukv[..., QK_NOPE:].reshape(KV_LORA, MLA_HEADS * V_DIM)

    w_pool_bd = jax.scipy.linalg.block_diag(*[w_pool[g] for g in range(len(POOL_WINDOWS))])
    b_sp = jnp.repeat(b_spatial.T, SG_DIM, axis=1)
    return {
        "w_in": w_in_pad.astype(BF16), "w_pool": w_pool_bd.astype(BF16),
        "pool_scale": pool_scale.reshape(1, -1), "w_spatial": w_spatial, "b_spatial": b_sp,
        "g_q": g_q.reshape(1, -1), "w_uq": w_uq_pad.astype(BF16), "w_uq_sw": w_uq_sw.astype(BF16),
        "g_kv": g_kv.reshape(1, -1), "w_k": w_k.astype(BF16), "w_v": w_v.astype(BF16),
        "w_out": w_out.astype(BF16), "w_router_t": w_router.T, "router_bias": router_bias.reshape(-1, 1),
        "ws_gate": ws_gate.astype(BF16), "ws_up": ws_up.astype(BF16), "ws_down": ws_down.astype(BF16),
    }


def kernel(x, c, positions, w_ada, b_ada, w_in, w_pool, pool_scale, w_spatial, b_spatial, g_q, w_uq, g_kv, w_ukv, w_out, w_router, router_bias, w_gate, w_up, w_down, ws_gate, ws_up, ws_down, final_gain):
    batch, seq, d = x.shape
    depth = w_ada.shape[0]
    mod = _ada_mod(c, w_ada, b_ada)
    cos_t, sin_t = _rope_tables(positions)
    x_parts = (x.reshape(batch * seq, d),)
    fg = final_gain.reshape(1, d)
    wg_b, wu_b, wd_b = w_gate.astype(BF16), w_up.astype(BF16), w_down.astype(BF16)
    ts = min(MOE_TOKENS, batch * seq)
    for l in range(depth):
        p = _prep_layer(w_in[l], w_pool[l], pool_scale[l], w_spatial[l], b_spatial[l], g_q[l], w_uq[l],
                        g_kv[l], w_ukv[l], w_out[l], w_router[l], router_bias[l], ws_gate[l], ws_up[l],
                        ws_down[l])
        shift1, scale1, gate1, shift2, scale2, gate2 = [
            mod[l, :, k * d:(k + 1) * d].reshape(batch, 1, d) for k in range(6)]
        outs = _mixer_in(x_parts, shift1, scale1, cos_t, sin_t, p, seq)
        mix, q, k, v = outs[:4]
        xt = outs[4] if len(outs) == 5 else x_parts[0]
        att = _attention(q, k, v, batch, seq)
        base, h2rows, cw_t = _mixer_out(mix, att, xt, gate1, shift2, scale2, gate2, p, seq)
        tok, wts, cnt = _plan(cw_t, ts)
        routed = _moe(h2rows, tok, wts, cnt[:, :, 0], wg_b, wu_b, wd_b, l, ts)
        x_parts = (base, routed, gate2)
    return _final(*x_parts, fg, seq).reshape(batch, seq, d)
```

```python
import functools

import jax
import jax.numpy as jnp
from jax import lax
from jax.experimental import pallas as pl
from jax.experimental.pallas import tpu as pltpu

F32 = jnp.float32
BF16 = jnp.bfloat16

EPS = 1e-6
LANES = 128
POOL_WINDOWS = (2, 4, 8, 16)
POOL_WIDTH = 256
POOL_CH = 64
MAX_WINDOW = 16
SG_HEADS = 4
SG_WIDTH = 256
SG_DIM = 64
SG_CHUNK = 128
MLA_HEADS = 8
V_DIM = 64
QK_NOPE = 64
QK_ROPE = 32
HALF_ROPE = QK_ROPE // 2
Q_LORA = 256
KV_LORA = 128
ROPE_THETA = 10000.0
ATTN_SCALE = (QK_NOPE + QK_ROPE) ** -0.5
HEAD_PAD = 128
N_EXPERTS = 64
TOP_K = 8
N_EXPERT_GROUPS = 8
GROUP_SIZE = N_EXPERTS // N_EXPERT_GROUPS
TOPK_GROUPS = 4
ROUTED_SCALE = 2.5
VMEM_LIMIT = 52 * 1024 * 1024


def _dot(a, b):
    return jnp.dot(a, b, preferred_element_type=F32)


def _dot_nt(a, b):
    return lax.dot_general(a, b, (((1,), (1,)), ((), ())), preferred_element_type=F32)


def _rms(x):
    return x * lax.rsqrt(jnp.mean(x * x, axis=-1, keepdims=True) + EPS)


def _gelu_tanh(x):
    c = (2.0 / jnp.pi) ** 0.5
    return x * (0.5 * (1.0 + jnp.tanh(c * (x + 0.044715 * (x * x * x)))))


def _silu(x):
    return x * jax.nn.sigmoid(x)


def _ada_kernel(c_ref, w_ref, b_ref, o_ref):
    cond = _silu(c_ref[...])
    o_ref[0] = _dot(cond.astype(BF16), w_ref[0].astype(BF16)) + b_ref[0]


def _ada_mod(c, w_ada, b_ada):
    depth, d, n = w_ada.shape
    b = c.shape[0]
    nt = 1536
    return pl.pallas_call(
        _ada_kernel,
        grid=(depth, n // nt),
        in_specs=[pl.BlockSpec((b, d), lambda l, j: (0, 0)),
                  pl.BlockSpec((1, d, nt), lambda l, j: (l, 0, j)),
                  pl.BlockSpec((1, 1, nt), lambda l, j: (l, 0, j))],
        out_specs=pl.BlockSpec((1, b, nt), lambda l, j: (l, 0, j)),
        out_shape=jax.ShapeDtypeStruct((depth, b, n), F32),
        compiler_params=pltpu.CompilerParams(vmem_limit_bytes=VMEM_LIMIT),
        name="ada_mod",
    )(c, w_ada, b_ada.reshape(depth, 1, n))


def _rope_kernel(pos_ref, invf_ref, sign_ref, cos_ref, sin_ref):
    ang = pos_ref[...].astype(F32) * invf_ref[...]
    cos_ref[...] = jnp.cos(ang)
    sin_ref[...] = jnp.sin(ang) * sign_ref[...]


def _rope_tables(positions):
    t = positions.size
    tm = 2048
    inv_freq = ROPE_THETA ** (-jnp.arange(0, QK_ROPE, 2, dtype=F32) / QK_ROPE)
    invf = jnp.zeros((1, HEAD_PAD), F32)
    invf = invf.at[0, QK_NOPE:QK_NOPE + HALF_ROPE].set(inv_freq)
    invf = invf.at[0, QK_NOPE + HALF_ROPE:QK_NOPE + QK_ROPE].set(inv_freq)
    sign = jnp.zeros((1, HEAD_PAD), F32)
    sign = sign.at[0, QK_NOPE:QK_NOPE + HALF_ROPE].set(-1.0)
    sign = sign.at[0, QK_NOPE + HALF_ROPE:QK_NOPE + QK_ROPE].set(1.0)
    return pl.pallas_call(
        _rope_kernel,
        grid=(t // tm,),
        in_specs=[pl.BlockSpec((tm, 1), lambda i: (i, 0)),
                  pl.BlockSpec((1, HEAD_PAD), lambda i: (0, 0)),
                  pl.BlockSpec((1, HEAD_PAD), lambda i: (0, 0))],
        out_specs=[pl.BlockSpec((tm, HEAD_PAD), lambda i: (i, 0)),
                   pl.BlockSpec((tm, HEAD_PAD), lambda i: (i, 0))],
        out_shape=[jax.ShapeDtypeStruct((t, HEAD_PAD), F32)] * 2,
        name="rope_tables",
    )(positions.reshape(t, 1), invf, sign)


def _mixer_in_kernel(*refs, tiles_per_seq, fused):
    if fused:
        base_ref, r_ref, gprev_ref = refs[:3]
        (shift_ref, scale_ref, win_ref, wpool_ref, pscale_ref, wsp_ref, bsp_ref, gq_ref, wuq_ref, wuqs_ref,
         gkv_ref, wk_ref, wv_ref, cos_ref, sin_ref, mix_ref, q_ref, k_ref, v_ref, x_out_ref, carry_ref) = refs[3:]
        tm = base_ref.shape[0]
        x = base_ref[...] + gprev_ref[0] * jnp.concatenate(
            [r_ref[pl.ds(j, tm, stride=ROW_TILE), :] for j in range(ROW_TILE)], axis=1)
        x_out_ref[...] = x
    else:
        (x_ref, shift_ref, scale_ref, win_ref, wpool_ref, pscale_ref, wsp_ref, bsp_ref, gq_ref, wuq_ref, wuqs_ref,
         gkv_ref, wk_ref, wv_ref, cos_ref, sin_ref, mix_ref, q_ref, k_ref, v_ref, carry_ref) = refs
        tm = x_ref.shape[0]
        x = x_ref[...]
    ti = pl.program_id(0) % tiles_per_seq
    h = _rms(x) * (1.0 + scale_ref[0]) + shift_ref[0]
    z = _dot(h.astype(BF16), win_ref[...])

    a = z[:, 0:POOL_WIDTH]

    @pl.when(ti == 0)
    def _():
        carry_ref[...] = jnp.zeros_like(carry_ref)

    ext = jnp.concatenate([carry_ref[...], a], axis=0)
    carry_ref[...] = a[tm - MAX_WINDOW:, :]
    p1 = ext + pltpu.roll(ext, 1, 0)
    p2 = p1 + pltpu.roll(p1, 2, 0)
    p3 = p2 + pltpu.roll(p2, 4, 0)
    p4 = p3 + pltpu.roll(p3, 8, 0)
    lane = lax.broadcasted_iota(jnp.int32, (tm, POOL_WIDTH), 1)
    row = lax.broadcasted_iota(jnp.int32, (tm, POOL_WIDTH), 0) + (ti * tm + 1)
    g0, g1, g2 = lane < POOL_CH, lane < 2 * POOL_CH, lane < 3 * POOL_CH
    wsum = jnp.where(g0, p1[MAX_WINDOW:], jnp.where(g1, p2[MAX_WINDOW:],
                     jnp.where(g2, p3[MAX_WINDOW:], p4[MAX_WINDOW:])))
    width = jnp.where(g0, POOL_WINDOWS[0], jnp.where(g1, POOL_WINDOWS[1],
                      jnp.where(g2, POOL_WINDOWS[2], POOL_WINDOWS[3])))
    cnt = jnp.minimum(row, width).astype(F32)
    dlt = wsum / cnt - a
    y_pool = _dot(dlt.astype(BF16), wpool_ref[...]) * pscale_ref[...]
    mix_ref[:, 0:POOL_WIDTH] = y_pool.astype(mix_ref.dtype)

    ug = _gelu_tanh(z[:, POOL_WIDTH:POOL_WIDTH + SG_WIDTH])
    vg = _gelu_tanh(z[:, POOL_WIDTH + SG_WIDTH:POOL_WIDTH + 2 * SG_WIDTH])
    mu = jnp.mean(vg, axis=-1, keepdims=True)
    vc = vg - mu
    var = jnp.mean(vc * vc, axis=-1, keepdims=True)
    vn = (vc * lax.rsqrt(var + EPS)).astype(BF16)
    r_i = lax.broadcasted_iota(jnp.int32, (SG_CHUNK, SG_CHUNK), 0)
    c_i = lax.broadcasted_iota(jnp.int32, (SG_CHUNK, SG_CHUNK), 1)
    wms = [jnp.where(c_i <= r_i, wsp_ref[hh], 0.0).astype(BF16) for hh in range(SG_HEADS)]
    lane_head = lax.broadcasted_iota(jnp.int32, (SG_CHUNK, SG_WIDTH), 1) // SG_DIM
    for cidx in range(tm // SG_CHUNK):
        rows = slice(cidx * SG_CHUNK, (cidx + 1) * SG_CHUNK)
        vchunk = vn[rows]
        mixed = bsp_ref[...]
        for hh in range(SG_HEADS):
            mixed = mixed + jnp.where(lane_head == hh, _dot(wms[hh], vchunk), 0.0)
        mix_ref[rows, POOL_WIDTH:POOL_WIDTH + SG_WIDTH] = (ug[rows] * mixed).astype(mix_ref.dtype)

    o_cq = POOL_WIDTH + 2 * SG_WIDTH
    o_ckv = o_cq + Q_LORA
    o_kpe = o_ckv + KV_LORA
    cos = cos_ref[...]
    sin = sin_ref[...]
    cqn = (_rms(z[:, o_cq:o_ckv]) * gq_ref[...]).astype(BF16)
    q = _dot(cqn, wuq_ref[...])
    qs = _dot(cqn, wuqs_ref[...])
    ckvn = (_rms(z[:, o_ckv:o_kpe]) * gkv_ref[...]).astype(BF16)
    kn = _dot(ckvn, wk_ref[...])
    kpe = z[:, o_kpe:o_kpe + HEAD_PAD] * cos + z[:, o_kpe + HEAD_PAD:o_kpe + 2 * HEAD_PAD] * sin
    for hh in range(MLA_HEADS):
        blk = slice(hh * HEAD_PAD, (hh + 1) * HEAD_PAD)
        q_ref[:, blk] = ((q[:, blk] * cos + qs[:, blk] * sin) * ATTN_SCALE).astype(q_ref.dtype)
        k_ref[:, blk] = (kn[:, blk] + kpe).astype(k_ref.dtype)
    v_ref[...] = _dot(ckvn, wv_ref[...]).astype(v_ref.dtype)


def _mixer_in(x_parts, shift, scale, cos_t, sin_t, p, seq):
    fused = len(x_parts) == 3
    t, d = x_parts[0].shape
    tm = 512
    tiles_per_seq = seq // tm
    nz = p["w_in"].shape[1]
    hq = MLA_HEADS * HEAD_PAD

    def full(shape):
        return pl.BlockSpec(shape, lambda i: (0,) * len(shape))

    def mod():
        return pl.BlockSpec((1, 1, d), lambda i: (i // tiles_per_seq, 0, 0))

    x_specs = [pl.BlockSpec((tm, d), lambda i: (i, 0))]
    extra_out_specs, extra_out_shapes = [], []
    if fused:
        x_specs += [pl.BlockSpec((tm * ROW_TILE, LANES), lambda i: (i, 0)), mod()]
        extra_out_specs = [pl.BlockSpec((tm, d), lambda i: (i, 0))]
        extra_out_shapes = [jax.ShapeDtypeStruct((t, d), F32)]
    return pl.pallas_call(
        functools.partial(_mixer_in_kernel, tiles_per_seq=tiles_per_seq, fused=fused),
        grid=(t // tm,),
        in_specs=x_specs + [mod(), mod(),
                  full((d, nz)), full((POOL_WIDTH, POOL_WIDTH)), full((1, POOL_WIDTH)),
                  full((SG_HEADS, SG_CHUNK, SG_CHUNK)), full((SG_CHUNK, SG_WIDTH)),
                  full((1, Q_LORA)), full((Q_LORA, hq)), full((Q_LORA, hq)),
                  full((1, KV_LORA)), full((KV_LORA, hq)), full((KV_LORA, MLA_HEADS * V_DIM)),
                  pl.BlockSpec((tm, HEAD_PAD), lambda i: (i, 0)),
                  pl.BlockSpec((tm, HEAD_PAD), lambda i: (i, 0))],
        out_specs=[pl.BlockSpec((tm, POOL_WIDTH + SG_WIDTH), lambda i: (i, 0)),
                   pl.BlockSpec((tm, hq), lambda i: (i, 0)),
                   pl.BlockSpec((tm, hq), lambda i: (i, 0)),
                   pl.BlockSpec((tm, MLA_HEADS * V_DIM), lambda i: (i, 0))] + extra_out_specs,
        out_shape=[jax.ShapeDtypeStruct((t, POOL_WIDTH + SG_WIDTH), BF16),
                   jax.ShapeDtypeStruct((t, hq), BF16),
                   jax.ShapeDtypeStruct((t, hq), BF16),
                   jax.ShapeDtypeStruct((t, MLA_HEADS * V_DIM), BF16)] + extra_out_shapes,
        scratch_shapes=[pltpu.VMEM((MAX_WINDOW, POOL_WIDTH), F32)],
        compiler_params=pltpu.CompilerParams(dimension_semantics=("arbitrary",),
                                             vmem_limit_bytes=VMEM_LIMIT),
        name="mixer_in",
    )(*x_parts, shift, scale, p["w_in"], p["w_pool"], p["pool_scale"], p["w_spatial"], p["b_spatial"],
      p["g_q"], p["w_uq"], p["w_uq_sw"], p["g_kv"], p["w_k"], p["w_v"], cos_t, sin_t)


ATTN_TQ = 256
ATTN_LOOKAHEAD = 2


def _attn_kernel(q_ref, k_ref, v_ref, o_ref):
    seq = q_ref.shape[0]
    tq = ATTN_TQ
    r_i = lax.broadcasted_iota(jnp.int32, (tq, tq), 0)
    c_i = lax.broadcasted_iota(jnp.int32, (tq, tq), 1)
    lane = lax.broadcasted_iota(jnp.int32, (tq, 2 * V_DIM), 1)

    def scores(i, hh):
        lo, hi = i * tq, (i + 1) * tq
        blk = slice(hh * HEAD_PAD, (hh + 1) * HEAD_PAD)
        q = q_ref[lo:hi, blk]
        s_d = jnp.where(c_i <= r_i, _dot_nt(q, k_ref[lo:hi, blk]), -jnp.inf)
        s_o = _dot_nt(q, k_ref[0:lo, blk]) if i > 0 else None
        return s_d, s_o

    def finish(i, s_d, s_o):
        lo, hi = i * tq, (i + 1) * tq
        m = jnp.max(s_d, axis=-1, keepdims=True)
        if s_o is not None:
            m = jnp.maximum(m, jnp.max(s_o, axis=-1, keepdims=True))
            p_o = jnp.exp(s_o - m)
        p_d = jnp.exp(s_d - m)
        l = jnp.sum(p_d, axis=-1, keepdims=True)
        acc = _dot(p_d.astype(BF16), v_ref[lo:hi, :])
        if s_o is not None:
            l = l + jnp.sum(p_o, axis=-1, keepdims=True)
            acc = acc + _dot(p_o.astype(BF16), v_ref[0:lo, :])
        return acc * (1.0 / l)

    chains = [(i, hh) for i in range(seq // tq) for hh in range(2)]
    pending = [scores(*chains[c]) for c in range(ATTN_LOOKAHEAD)]
    outs = []
    for c, (i, hh) in enumerate(chains):
        if c + ATTN_LOOKAHEAD < len(chains):
            pending.append(scores(*chains[c + ATTN_LOOKAHEAD]))
        outs.append(finish(i, *pending.pop(0)))
        if hh == 1:
            o_ref[i * tq:(i + 1) * tq, :] = jnp.where(lane < V_DIM, outs[-2], outs[-1]).astype(o_ref.dtype)


def _attention(q, k, v, batch, seq):
    t = q.shape[0]
    return pl.pallas_call(
        _attn_kernel,
        grid=(batch, MLA_HEADS // 2),
        in_specs=[pl.BlockSpec((seq, 2 * HEAD_PAD), lambda b, hp: (b, hp)),
                  pl.BlockSpec((seq, 2 * HEAD_PAD), lambda b, hp: (b, hp)),
                  pl.BlockSpec((seq, 2 * V_DIM), lambda b, hp: (b, hp))],
        out_specs=pl.BlockSpec((seq, 2 * V_DIM), lambda b, hp: (b, hp)),
        out_shape=jax.ShapeDtypeStruct((t, MLA_HEADS * V_DIM), BF16),
        compiler_params=pltpu.CompilerParams(
            dimension_semantics=("arbitrary", "arbitrary"), vmem_limit_bytes=VMEM_LIMIT),
        name="attention",
    )(q, k, v)


def _first_max_index(cur, idx, sentinel):
    m = jnp.max(cur, axis=0, keepdims=True)
    first = jnp.min(jnp.where(cur == m, idx, sentinel), axis=0, keepdims=True)
    return m, first


def _route(logits_t, bias_t):
    n_tok = logits_t.shape[1]
    scores = jax.nn.sigmoid(logits_t)
    sel = scores + bias_t
    neg = -jnp.inf
    sub = lax.broadcasted_iota(jnp.int32, (GROUP_SIZE, n_tok), 0).astype(F32)
    gid = lax.broadcasted_iota(jnp.int32, (N_EXPERT_GROUPS, n_tok), 0).astype(F32)
    gscore = jnp.zeros((N_EXPERT_GROUPS, n_tok), F32)
    for g in range(N_EXPERT_GROUPS):
        s = sel[g * GROUP_SIZE:(g + 1) * GROUP_SIZE]
        m1, i1 = _first_max_index(s, sub, float(GROUP_SIZE))
        m2 = jnp.max(jnp.where(sub == i1, neg, s), axis=0, keepdims=True)
        gscore = jnp.where(gid == float(g), m1 + m2, gscore)
    eid = lax.broadcasted_iota(jnp.int32, (N_EXPERTS, n_tok), 0).astype(F32)
    egroup = jnp.floor(eid * (1.0 / GROUP_SIZE))
    allowed = jnp.zeros((N_EXPERTS, n_tok), F32)
    cur = gscore
    for _ in range(TOPK_GROUPS):
        _, gi = _first_max_index(cur, gid, float(N_EXPERT_GROUPS))
        cur = jnp.where(gid == gi, neg, cur)
        allowed = jnp.where(egroup == gi, 1.0, allowed)
    cur = jnp.where(allowed > 0.0, sel, neg)
    chosen = jnp.zeros((N_EXPERTS, n_tok), F32)
    for _ in range(TOP_K):
        _, ei = _first_max_index(cur, eid, float(N_EXPERTS))
        hit = eid == ei
        cur = jnp.where(hit, neg, cur)
        chosen = jnp.where(hit, 1.0, chosen)
    w = jnp.where(chosen > 0.0, scores, 0.0)
    return w / jnp.sum(w, axis=0, keepdims=True) * ROUTED_SCALE


def _split_bf16(x):
    hi = x.astype(BF16)
    lo = (x - hi.astype(F32)).astype(BF16)
    return hi, lo


ROW_TILE = 8


def _mixer_out_kernel(mix_ref, att_ref, x_ref, gate1_ref, shift_ref, scale_ref, gate2_ref, wo_ref, wr_ref,
                      rb_ref, sg_ref, su_ref, sd_ref, base_ref, h2r_ref, cw_ref):
    tm = x_ref.shape[0]
    half = mix_ref.shape[1]
    y = _dot(mix_ref[...], wo_ref[0:half, :]) + _dot(att_ref[...], wo_ref[half:, :])
    x2 = x_ref[...] + gate1_ref[0] * y
    h2 = _rms(x2) * (1.0 + scale_ref[0]) + shift_ref[0]
    for j in range(ROW_TILE):
        h2r_ref[pl.ds(j, tm, stride=ROW_TILE), :] = h2[:, j * LANES:(j + 1) * LANES]
    hb = h2.astype(BF16)
    act = _silu(_dot(hb, sg_ref[...])) * _dot(hb, su_ref[...])
    base_ref[...] = x2 + gate2_ref[0] * _dot(act.astype(BF16), sd_ref[...])
    h_hi, h_lo = _split_bf16(h2)
    w_hi, w_lo = _split_bf16(wr_ref[...])
    logits_t = _dot_nt(w_hi, h_hi) + (_dot_nt(w_hi, h_lo) + _dot_nt(w_lo, h_hi))
    cw_ref[...] = _route(logits_t, rb_ref[...])


def _mixer_out(mix, att, x2d, gate1, shift, scale, gate2, p, seq):
    t, d = x2d.shape
    tm = 512
    tiles_per_seq = seq // tm
    ff = p["ws_gate"].shape[1]

    def full(shape):
        return pl.BlockSpec(shape, lambda i: (0,) * len(shape))

    def mod():
        return pl.BlockSpec((1, 1, d), lambda i: (i // tiles_per_seq, 0, 0))

    return pl.pallas_call(
        _mixer_out_kernel,
        grid=(t // tm,),
        in_specs=[pl.BlockSpec((tm, mix.shape[1]), lambda i: (i, 0)),
                  pl.BlockSpec((tm, att.shape[1]), lambda i: (i, 0)),
                  pl.BlockSpec((tm, d), lambda i: (i, 0)), mod(), mod(), mod(), mod(),
                  full((d, d)), full((N_EXPERTS, d)), full((N_EXPERTS, 1)),
                  full((d, ff)), full((d, ff)), full((ff, d))],
        out_specs=[pl.BlockSpec((tm, d), lambda i: (i, 0)),
                   pl.BlockSpec((tm * ROW_TILE, LANES), lambda i: (i, 0)),
                   pl.BlockSpec((N_EXPERTS, tm), lambda i: (0, i))],
        out_shape=[jax.ShapeDtypeStruct((t, d), F32),
                   jax.ShapeDtypeStruct((t * ROW_TILE, LANES), F32),
                   jax.ShapeDtypeStruct((N_EXPERTS, t), F32)],
        compiler_params=pltpu.CompilerParams(dimension_semantics=("arbitrary",),
                                             vmem_limit_bytes=VMEM_LIMIT),
        name="mixer_out",
    )(mix, att, x2d, gate1, shift, scale, gate2, p["w_out"], p["w_router_t"], p["router_bias"],
      p["ws_gate"], p["ws_up"], p["ws_down"])


EXPERT_BLOCK = 128
TOK_BITS = 12
RMW_BATCH = 16
MOE_TOKENS = 4096
MOE_GROUP = 5
GROUP_ROWS = MOE_GROUP * EXPERT_BLOCK


def _list_len(ts):
    return -(-ts // GROUP_ROWS) * GROUP_ROWS


def _plan_kernel(cw_ref, tok_ref, w_ref, cnt_ref):
    n_e, ts = cw_ref.shape
    cw = cw_ref[...]
    chosen = cw > 0.0
    cf = jnp.where(chosen, 1.0, 0.0).astype(BF16)
    r_i = lax.broadcasted_iota(jnp.int32, (LANES, LANES), 0)
    c_i = lax.broadcasted_iota(jnp.int32, (LANES, LANES), 1)
    before = jnp.where(r_i < c_i, 1.0, 0.0).astype(BF16)
    ones = jnp.ones((LANES, LANES), BF16)
    carry = jnp.zeros((n_e, LANES), F32)
    ranks = []
    for k in range(ts // LANES):
        ck = cf[:, k * LANES:(k + 1) * LANES]
        ranks.append(_dot(ck, before) + carry)
        carry = carry + _dot(ck, ones)
    rank = jnp.concatenate(ranks, axis=1).astype(jnp.int32)
    lane = lax.broadcasted_iota(jnp.int32, (n_e, ts), 1)
    packed = jnp.where(chosen, ((lane - rank) << TOK_BITS) | lane, -1)
    w = jnp.where(chosen, cw, 0.0)
    for bit in range(ts.bit_length() - 1):
        step = 1 << bit
        src_p = pltpu.roll(packed, ts - step, 1)
        src_w = pltpu.roll(w, ts - step, 1)
        take = (src_p >= 0) & (lane < ts - step) & (((src_p >> (TOK_BITS + bit)) & 1) == 1)
        keep = (packed >= 0) & (((packed >> (TOK_BITS + bit)) & 1) == 0)
        packed = jnp.where(take, src_p, jnp.where(keep, packed, -1))
        w = jnp.where(take, src_w, jnp.where(keep, w, 0.0))
    count = carry[:, 0:1].astype(jnp.int32)
    valid = packed >= 0
    tok = packed & ((1 << TOK_BITS) - 1)
    tok_last = jnp.max(jnp.where(valid, tok, 0).astype(F32), axis=1, keepdims=True).astype(jnp.int32)
    w_last = jnp.sum(jnp.where(lane == count - 1, w, 0.0), axis=1, keepdims=True)
    batch_end = (count + (RMW_BATCH - 1)) & (-RMW_BATCH)
    tok = jnp.where(valid, tok, tok_last)
    w = jnp.where(valid, w, jnp.where(lane < batch_end, w_last, 0.0))
    n_tail = tok_ref.shape[2] - ts
    if n_tail:
        tok = jnp.concatenate([tok, jnp.broadcast_to(tok_last, (n_e, n_tail))], axis=1)
        w = jnp.concatenate([w, jnp.zeros((n_e, n_tail), F32)], axis=1)
    tok_ref[0] = tok * ROW_TILE
    w_ref[0] = w
    cnt_ref[0] = jnp.broadcast_to(count, (n_e, LANES))


def _plan(cw_t, ts):
    n_e, t = cw_t.shape
    n_sup = t // ts
    n_list = _list_len(ts)
    assert ts <= (1 << TOK_BITS)
    return pl.pallas_call(
        _plan_kernel,
        grid=(n_sup,),
        in_specs=[pl.BlockSpec((n_e, ts), lambda s: (0, s))],
        out_specs=[pl.BlockSpec((1, n_e, n_list), lambda s: (s, 0, 0)),
                   pl.BlockSpec((1, n_e, n_list), lambda s: (s, 0, 0)),
                   pl.BlockSpec((1, n_e, LANES), lambda s: (s, 0, 0))],
        out_shape=[jax.ShapeDtypeStruct((n_sup, n_e, n_list), jnp.int32),
                   jax.ShapeDtypeStruct((n_sup, n_e, n_list), F32),
                   jax.ShapeDtypeStruct((n_sup, n_e, LANES), jnp.int32)],
        compiler_params=pltpu.CompilerParams(dimension_semantics=("arbitrary",),
                                             vmem_limit_bytes=VMEM_LIMIT),
        name="plan",
    )(cw_t)


XT_PITCH = GROUP_ROWS + 1


def _moe_gather(list_ref, first, h_ref, xt_ref):
    for r in range(GROUP_ROWS):
        start = pl.multiple_of(list_ref[0, 0, first + r], ROW_TILE)
        xt_ref[pl.ds(r, ROW_TILE, stride=XT_PITCH), :] = h_ref[pl.ds(start, ROW_TILE), :]


def _moe_scatter_add(list_ref, first, yt_ref, acc_ref):
    for grp in range(GROUP_ROWS // RMW_BATCH):
        rows = range(grp * RMW_BATCH, (grp + 1) * RMW_BATCH)
        starts = [pl.multiple_of(list_ref[0, 0, first + r], ROW_TILE) for r in rows]
        olds = [acc_ref[pl.ds(st, ROW_TILE), :] for st in starts]
        news = [old + yt_ref[pl.ds(r, ROW_TILE, stride=XT_PITCH), :] for r, old in zip(rows, olds)]
        for st, new in zip(starts, news):
            acc_ref[pl.ds(st, ROW_TILE), :] = new


def _moe_experts(q, xt_ref, w_ref, wg_ref, wu_ref, wd_ref, yt_ref):
    x = jnp.concatenate(
        [xt_ref[pl.ds(j * XT_PITCH, GROUP_ROWS), :] for j in range(ROW_TILE)], axis=1).astype(BF16)
    act = _silu(_dot(x, wg_ref[0, 0].astype(BF16))) * _dot(x, wu_ref[0, 0].astype(BF16))
    y = _dot(act.astype(BF16), wd_ref[0, 0].astype(BF16))
    w_col = jnp.concatenate(
        [jnp.broadcast_to(w_ref[0, q * MOE_GROUP + k], (EXPERT_BLOCK, EXPERT_BLOCK)).T
         for k in range(MOE_GROUP)], axis=0)
    for j in range(ROW_TILE):
        yt_ref[pl.ds(j * XT_PITCH, GROUP_ROWS), :] = y[:, j * LANES:(j + 1) * LANES] * w_col


def _groups(count):
    return jnp.maximum((count + (GROUP_ROWS - 1)) // GROUP_ROWS, 1)


def _moe_kernel(cnt_ref, tokp_ref, tok_ref, tokn_ref, w_ref, h_ref, wg_ref, wu_ref, wd_ref, acc_ref,
                xt0_ref, xt1_ref, yt_ref):
    s = pl.program_id(0)
    e = pl.program_id(1)
    n = _groups(cnt_ref[s, e])
    n_prev = _groups(cnt_ref[s, jnp.maximum(e - 1, 0)])
    prev_first = (n_prev - 1) * GROUP_ROWS

    @pl.when(e == 0)
    def _():
        acc_ref[...] = jnp.zeros_like(acc_ref)
        yt_ref[...] = jnp.zeros_like(yt_ref)
        _moe_gather(tok_ref, 0, h_ref, xt0_ref)

    def experts(q, xt_ref):
        _moe_experts(q, xt_ref, w_ref, wg_ref, wu_ref, wd_ref, yt_ref)

    def step(xt_cur, xt_next):
        @pl.when((n == 1) & (n_prev == 1))
        def _():
            _moe_gather(tokn_ref, 0, h_ref, xt_next)
            _moe_scatter_add(tokp_ref, 0, yt_ref, acc_ref)
            experts(0, xt_cur)

        @pl.when((n == 1) & (n_prev > 1))
        def _():
            _moe_gather(tokn_ref, 0, h_ref, xt_next)
            _moe_scatter_add(tokp_ref, prev_first, yt_ref, acc_ref)
            experts(0, xt_cur)

        @pl.when(n > 1)
        def _():
            _moe_gather(tokn_ref, 0, h_ref, xt_next)
            _moe_scatter_add(tokp_ref, prev_first, yt_ref, acc_ref)
            experts(0, xt_cur)

            def more(q, carry):
                _moe_scatter_add(tok_ref, (q - 1) * GROUP_ROWS, yt_ref, acc_ref)
                _moe_gather(tok_ref, q * GROUP_ROWS, h_ref, xt_cur)
                experts(q, xt_cur)
                return carry

            lax.fori_loop(1, n, more, 0)

    @pl.when(e % 2 == 0)
    def _():
        step(xt0_ref, xt1_ref)

    @pl.when(e % 2 == 1)
    def _():
        step(xt1_ref, xt0_ref)

    @pl.when(e == pl.num_programs(1) - 1)
    def _():
        _moe_scatter_add(tok_ref, (n - 1) * GROUP_ROWS, yt_ref, acc_ref)


def _moe(h2rows, tok, wts, counts, wg, wu, wd, layer, ts):
    rows = h2rows.shape[0]
    n_sup = rows // (ts * ROW_TILE)
    d, ff = wg.shape[2], wg.shape[3]
    n_list = tok.shape[2]
    assert d == ROW_TILE * LANES and n_list == _list_len(ts)

    assert N_EXPERTS % 2 == 0

    def list_idx(shift):
        def idx(s, e, cnt):
            return (s * N_EXPERTS + jnp.clip(e + shift, 0, N_EXPERTS - 1), 0, 0)
        return idx

    def sup_idx(s, e, cnt):
        return (s, 0)

    def w_idx(s, e, cnt):
        return (layer, e, 0, 0)

    def list_spec(shift):
        return pl.BlockSpec((1, 1, n_list), list_idx(shift), memory_space=pltpu.SMEM)

    once = pl.Buffered(1)
    tile = pltpu.VMEM((ROW_TILE * XT_PITCH, LANES), F32)
    grid_spec = pltpu.PrefetchScalarGridSpec(
        num_scalar_prefetch=1,
        grid=(n_sup, N_EXPERTS),
        in_specs=[list_spec(-1), list_spec(0), list_spec(1),
                  pl.BlockSpec((1, n_list // EXPERT_BLOCK, 1, EXPERT_BLOCK),
                               lambda s, e, cnt: (s * N_EXPERTS + e, 0, 0, 0)),
                  pl.BlockSpec((ts * ROW_TILE, LANES), sup_idx, pipeline_mode=once),
                  pl.BlockSpec((1, 1, d, ff), w_idx),
                  pl.BlockSpec((1, 1, d, ff), w_idx),
                  pl.BlockSpec((1, 1, ff, d), w_idx)],
        out_specs=pl.BlockSpec((ts * ROW_TILE, LANES), sup_idx, pipeline_mode=once),
        scratch_shapes=[tile, tile, tile],
    )
    lists = tok.reshape(n_sup * N_EXPERTS, 1, n_list)
    return pl.pallas_call(
        _moe_kernel,
        grid_spec=grid_spec,
        out_shape=jax.ShapeDtypeStruct((rows, LANES), F32),
        compiler_params=pltpu.CompilerParams(dimension_semantics=("arbitrary", "arbitrary"),
                                             vmem_limit_bytes=VMEM_LIMIT),
        name="moe",
    )(counts, lists, lists, lists,
      wts.reshape(n_sup * N_EXPERTS, n_list // EXPERT_BLOCK, 1, EXPERT_BLOCK), h2rows, wg, wu, wd)


def _routed_rows_to_tile(r_ref, tm):
    return jnp.concatenate([r_ref[pl.ds(j, tm, stride=ROW_TILE), :] for j in range(ROW_TILE)], axis=1)


def _final_kernel(base_ref, r_ref, gate_ref, fg_ref, o_ref):
    tm = base_ref.shape[0]
    x = base_ref[...] + gate_ref[0] * _routed_rows_to_tile(r_ref, tm)
    o_ref[...] = _rms(x) * fg_ref[...]


def _final(base, routed, gate, final_gain, seq):
    t, d = base.shape
    tm = 512
    tiles_per_seq = seq // tm
    return pl.pallas_call(
        _final_kernel,
        grid=(t // tm,),
        in_specs=[pl.BlockSpec((tm, d), lambda i: (i, 0)),
                  pl.BlockSpec((tm * ROW_TILE, LANES), lambda i: (i, 0)),
                  pl.BlockSpec((1, 1, d), lambda i: (i // tiles_per_seq, 0, 0)),
                  pl.BlockSpec((1, d), lambda i: (0, 0))],
        out_specs=pl.BlockSpec((tm, d), lambda i: (i, 0)),
        out_shape=jax.ShapeDtypeStruct((t, d), F32),
        compiler_params=pltpu.CompilerParams(dimension_semantics=("arbitrary",),
                                             vmem_limit_bytes=VMEM_LIMIT),
        name="final_norm",
    )(base, routed, gate, final_gain)


def _prep_layer(w_in, w_pool, pool_scale, w_spatial, b_spatial, g_q, w_uq, g_kv, w_ukv, w_out,
                w_router, router_bias, ws_gate, ws_up, ws_down):
    d = w_in.shape[0]
    o_kpe = POOL_WIDTH + 2 * SG_WIDTH + Q_LORA + KV_LORA
    x1 = w_in[:, o_kpe:o_kpe + HALF_ROPE]
    x2 = w_in[:, o_kpe + HALF_ROPE:o_kpe + QK_ROPE]
    zl = jnp.zeros((d, QK_NOPE), F32)
    zr = jnp.zeros((d, HEAD_PAD - QK_NOPE - QK_ROPE), F32)
    w_in_pad = jnp.concatenate([w_in[:, :o_kpe], zl, x1, x2, zr, zl, x2, x1, zr], axis=1)

    uq = w_uq.reshape(Q_LORA, MLA_HEADS, QK_NOPE + QK_ROPE)
    q1 = uq[..., QK_NOPE:QK_NOPE + HALF_ROPE]
    q2 = uq[..., QK_NOPE + HALF_ROPE:]
    zq = jnp.zeros((Q_LORA, MLA_HEADS, HEAD_PAD - QK_NOPE - QK_ROPE), F32)
    w_uq_pad = jnp.concatenate([uq, zq], axis=-1).reshape(Q_LORA, MLA_HEADS * HEAD_PAD)
    w_uq_sw = jnp.concatenate([jnp.zeros_like(uq[..., :QK_NOPE]), q2, q1, zq], axis=-1)
    w_uq_sw = w_uq_sw.reshape(Q_LORA, MLA_HEADS * HEAD_PAD)

    ukv = w_ukv.reshape(KV_LORA, MLA_HEADS, QK_NOPE + V_DIM)
    zk = jnp.zeros((KV_LORA, MLA_HEADS, HEAD_PAD - QK_NOPE), F32)
    w_k = jnp.concatenate([ukv[..., :QK_NOPE], zk], axis=-1).reshape(KV_LORA, MLA_HEADS * HEAD_PAD)
    w_v = ukv[..., QK_NOPE:].reshape(KV_LORA, MLA_HEADS * V_DIM)

    w_pool_bd = jax.scipy.linalg.block_diag(*[w_pool[g] for g in range(len(POOL_WINDOWS))])
    b_sp = jnp.repeat(b_spatial.T, SG_DIM, axis=1)
    return {
        "w_in": w_in_pad.astype(BF16), "w_pool": w_pool_bd.astype(BF16),
        "pool_scale": pool_scale.reshape(1, -1), "w_spatial": w_spatial, "b_spatial": b_sp,
        "g_q": g_q.reshape(1, -1), "w_uq": w_uq_pad.astype(BF16), "w_uq_sw": w_uq_sw.astype(BF16),
        "g_kv": g_kv.reshape(1, -1), "w_k": w_k.astype(BF16), "w_v": w_v.astype(BF16),
        "w_out": w_out.astype(BF16), "w_router_t": w_router.T, "router_bias": router_bias.reshape(-1, 1),
        "ws_gate": ws_gate.astype(BF16), "ws_up": ws_up.astype(BF16), "ws_down": ws_down.astype(BF16),
    }


def kernel(x, c, positions, w_ada, b_ada, w_in, w_pool, pool_scale, w_spatial, b_spatial, g_q, w_uq, g_kv, w_ukv, w_out, w_router, router_bias, w_gate, w_up, w_down, ws_gate, ws_up, ws_down, final_gain):
    batch, seq, d = x.shape
    depth = w_ada.shape[0]
    mod = _ada_mod(c, w_ada, b_ada)
    cos_t, sin_t = _rope_tables(positions)
    x_parts = (x.reshape(batch * seq, d),)
    fg = final_gain.reshape(1, d)
    wg_b, wu_b, wd_b = w_gate, w_up, w_down
    ts = min(MOE_TOKENS, batch * seq)
    for l in range(depth):
        p = _prep_layer(w_in[l], w_pool[l], pool_scale[l], w_spatial[l], b_spatial[l], g_q[l], w_uq[l],
                        g_kv[l], w_ukv[l], w_out[l], w_router[l], router_bias[l], ws_gate[l], ws_up[l],
                        ws_down[l])
        shift1, scale1, gate1, shift2, scale2, gate2 = [
            mod[l, :, k * d:(k + 1) * d].reshape(batch, 1, d) for k in range(6)]
        outs = _mixer_in(x_parts, shift1, scale1, cos_t, sin_t, p, seq)
        mix, q, k, v = outs[:4]
        xt = outs[4] if len(outs) == 5 else x_parts[0]
        att = _attention(q, k, v, batch, seq)
        base, h2rows, cw_t = _mixer_out(mix, att, xt, gate1, shift2, scale2, gate2, p, seq)
        tok, wts, cnt = _plan(cw_t, ts)
        routed = _moe(h2rows, tok, wts, cnt[:, :, 0], wg_b, wu_b, wd_b, l, ts)
        x_parts = (base, routed, gate2)
    return _final(*x_parts, fg, seq).reshape(batch, seq, d)
```

```python
import functools

import jax
import jax.numpy as jnp
from jax import lax
from jax.experimental import pallas as pl
from jax.experimental.pallas import tpu as pltpu

F32 = jnp.float32
BF16 = jnp.bfloat16

EPS = 1e-6
LANES = 128
POOL_WINDOWS = (2, 4, 8, 16)
POOL_WIDTH = 256
POOL_CH = 64
MAX_WINDOW = 16
SG_HEADS = 4
SG_WIDTH = 256
SG_DIM = 64
SG_CHUNK = 128
MLA_HEADS = 8
V_DIM = 64
QK_NOPE = 64
QK_ROPE = 32
HALF_ROPE = QK_ROPE // 2
Q_LORA = 256
KV_LORA = 128
ROPE_THETA = 10000.0
ATTN_SCALE = (QK_NOPE + QK_ROPE) ** -0.5
HEAD_PAD = 128
N_EXPERTS = 64
TOP_K = 8
N_EXPERT_GROUPS = 8
GROUP_SIZE = N_EXPERTS // N_EXPERT_GROUPS
TOPK_GROUPS = 4
ROUTED_SCALE = 2.5
VMEM_LIMIT = 52 * 1024 * 1024


def _dot(a, b):
    return jnp.dot(a, b, preferred_element_type=F32)


def _dot_nt(a, b):
    return lax.dot_general(a, b, (((1,), (1,)), ((), ())), preferred_element_type=F32)


def _rms(x):
    return x * lax.rsqrt(jnp.mean(x * x, axis=-1, keepdims=True) + EPS)


def _gelu_tanh(x):
    c = (2.0 / jnp.pi) ** 0.5
    return x * (0.5 * (1.0 + jnp.tanh(c * (x + 0.044715 * (x * x * x)))))


def _silu(x):
    return x * jax.nn.sigmoid(x)


def _ada_kernel(c_ref, w_ref, b_ref, o_ref):
    cond = _silu(c_ref[...])
    o_ref[0] = _dot(cond.astype(BF16), w_ref[0].astype(BF16)) + b_ref[0]


def _ada_mod(c, w_ada, b_ada):
    depth, d, n = w_ada.shape
    b = c.shape[0]
    nt = 1536
    return pl.pallas_call(
        _ada_kernel,
        grid=(depth, n // nt),
        in_specs=[pl.BlockSpec((b, d), lambda l, j: (0, 0)),
                  pl.BlockSpec((1, d, nt), lambda l, j: (l, 0, j)),
                  pl.BlockSpec((1, 1, nt), lambda l, j: (l, 0, j))],
        out_specs=pl.BlockSpec((1, b, nt), lambda l, j: (l, 0, j)),
        out_shape=jax.ShapeDtypeStruct((depth, b, n), F32),
        compiler_params=pltpu.CompilerParams(vmem_limit_bytes=VMEM_LIMIT),
        name="ada_mod",
    )(c, w_ada, b_ada.reshape(depth, 1, n))


def _rope_kernel(pos_ref, invf_ref, sign_ref, cos_ref, sin_ref):
    ang = pos_ref[...].astype(F32) * invf_ref[...]
    cos_ref[...] = jnp.cos(ang)
    sin_ref[...] = jnp.sin(ang) * sign_ref[...]


def _rope_tables(positions):
    t = positions.size
    tm = 2048
    inv_freq = ROPE_THETA ** (-jnp.arange(0, QK_ROPE, 2, dtype=F32) / QK_ROPE)
    invf = jnp.zeros((1, HEAD_PAD), F32)
    invf = invf.at[0, QK_NOPE:QK_NOPE + HALF_ROPE].set(inv_freq)
    invf = invf.at[0, QK_NOPE + HALF_ROPE:QK_NOPE + QK_ROPE].set(inv_freq)
    sign = jnp.zeros((1, HEAD_PAD), F32)
    sign = sign.at[0, QK_NOPE:QK_NOPE + HALF_ROPE].set(-1.0)
    sign = sign.at[0, QK_NOPE + HALF_ROPE:QK_NOPE + QK_ROPE].set(1.0)
    return pl.pallas_call(
        _rope_kernel,
        grid=(t // tm,),
        in_specs=[pl.BlockSpec((tm, 1), lambda i: (i, 0)),
                  pl.BlockSpec((1, HEAD_PAD), lambda i: (0, 0)),
                  pl.BlockSpec((1, HEAD_PAD), lambda i: (0, 0))],
        out_specs=[pl.BlockSpec((tm, HEAD_PAD), lambda i: (i, 0)),
                   pl.BlockSpec((tm, HEAD_PAD), lambda i: (i, 0))],
        out_shape=[jax.ShapeDtypeStruct((t, HEAD_PAD), F32)] * 2,
        name="rope_tables",
    )(positions.reshape(t, 1), invf, sign)


def _mixer_in_kernel(*refs, tiles_per_seq, fused):
    if fused:
        base_ref, r_ref, gprev_ref = refs[:3]
        (shift_ref, scale_ref, win_ref, wpool_ref, pscale_ref, wsp_ref, bsp_ref, gq_ref, wuq_ref, wuqs_ref,
         gkv_ref, wk_ref, wv_ref, cos_ref, sin_ref, mix_ref, q_ref, k_ref, v_ref, x_out_ref, carry_ref) = refs[3:]
        tm = base_ref.shape[0]
        x = base_ref[...] + gprev_ref[0] * jnp.concatenate(
            [r_ref[pl.ds(j, tm, stride=ROW_TILE), :] for j in range(ROW_TILE)], axis=1)
        x_out_ref[...] = x
    else:
        (x_ref, shift_ref, scale_ref, win_ref, wpool_ref, pscale_ref, wsp_ref, bsp_ref, gq_ref, wuq_ref, wuqs_ref,
         gkv_ref, wk_ref, wv_ref, cos_ref, sin_ref, mix_ref, q_ref, k_ref, v_ref, carry_ref) = refs
        tm = x_ref.shape[0]
        x = x_ref[...]
    ti = pl.program_id(0) % tiles_per_seq
    h = _rms(x) * (1.0 + scale_ref[0]) + shift_ref[0]
    z = _dot(h.astype(BF16), win_ref[...])

    a = z[:, 0:POOL_WIDTH]

    @pl.when(ti == 0)
    def _():
        carry_ref[...] = jnp.zeros_like(carry_ref)

    ext = jnp.concatenate([carry_ref[...], a], axis=0)
    carry_ref[...] = a[tm - MAX_WINDOW:, :]
    p1 = ext + pltpu.roll(ext, 1, 0)
    p2 = p1 + pltpu.roll(p1, 2, 0)
    p3 = p2 + pltpu.roll(p2, 4, 0)
    p4 = p3 + pltpu.roll(p3, 8, 0)
    lane = lax.broadcasted_iota(jnp.int32, (tm, POOL_WIDTH), 1)
    row = lax.broadcasted_iota(jnp.int32, (tm, POOL_WIDTH), 0) + (ti * tm + 1)
    g0, g1, g2 = lane < POOL_CH, lane < 2 * POOL_CH, lane < 3 * POOL_CH
    wsum = jnp.where(g0, p1[MAX_WINDOW:], jnp.where(g1, p2[MAX_WINDOW:],
                     jnp.where(g2, p3[MAX_WINDOW:], p4[MAX_WINDOW:])))
    width = jnp.where(g0, POOL_WINDOWS[0], jnp.where(g1, POOL_WINDOWS[1],
                      jnp.where(g2, POOL_WINDOWS[2], POOL_WINDOWS[3])))
    cnt = jnp.minimum(row, width).astype(F32)
    dlt = wsum / cnt - a
    y_pool = _dot(dlt.astype(BF16), wpool_ref[...]) * pscale_ref[...]
    mix_ref[:, 0:POOL_WIDTH] = y_pool.astype(mix_ref.dtype)

    ug = _gelu_tanh(z[:, POOL_WIDTH:POOL_WIDTH + SG_WIDTH])
    vg = _gelu_tanh(z[:, POOL_WIDTH + SG_WIDTH:POOL_WIDTH + 2 * SG_WIDTH])
    mu = jnp.mean(vg, axis=-1, keepdims=True)
    vc = vg - mu
    var = jnp.mean(vc * vc, axis=-1, keepdims=True)
    vn = (vc * lax.rsqrt(var + EPS)).astype(BF16)
    r_i = lax.broadcasted_iota(jnp.int32, (SG_CHUNK, SG_CHUNK), 0)
    c_i = lax.broadcasted_iota(jnp.int32, (SG_CHUNK, SG_CHUNK), 1)
    wms = [jnp.where(c_i <= r_i, wsp_ref[hh], 0.0).astype(BF16) for hh in range(SG_HEADS)]
    lane_head = lax.broadcasted_iota(jnp.int32, (SG_CHUNK, SG_WIDTH), 1) // SG_DIM
    for cidx in range(tm // SG_CHUNK):
        rows = slice(cidx * SG_CHUNK, (cidx + 1) * SG_CHUNK)
        vchunk = vn[rows]
        mixed = bsp_ref[...]
        for hh in range(SG_HEADS):
            mixed = mixed + jnp.where(lane_head == hh, _dot(wms[hh], vchunk), 0.0)
        mix_ref[rows, POOL_WIDTH:POOL_WIDTH + SG_WIDTH] = (ug[rows] * mixed).astype(mix_ref.dtype)

    o_cq = POOL_WIDTH + 2 * SG_WIDTH
    o_ckv = o_cq + Q_LORA
    o_kpe = o_ckv + KV_LORA
    cos = cos_ref[...]
    sin = sin_ref[...]
    cqn = (_rms(z[:, o_cq:o_ckv]) * gq_ref[...]).astype(BF16)
    q = _dot(cqn, wuq_ref[...])
    qs = _dot(cqn, wuqs_ref[...])
    ckvn = (_rms(z[:, o_ckv:o_kpe]) * gkv_ref[...]).astype(BF16)
    kn = _dot(ckvn, wk_ref[...])
    kpe = z[:, o_kpe:o_kpe + HEAD_PAD] * cos + z[:, o_kpe + HEAD_PAD:o_kpe + 2 * HEAD_PAD] * sin
    for hh in range(MLA_HEADS):
        blk = slice(hh * HEAD_PAD, (hh + 1) * HEAD_PAD)
        q_ref[:, blk] = ((q[:, blk] * cos + qs[:, blk] * sin) * ATTN_SCALE).astype(q_ref.dtype)
        k_ref[:, blk] = (kn[:, blk] + kpe).astype(k_ref.dtype)
    v_ref[...] = _dot(ckvn, wv_ref[...]).astype(v_ref.dtype)


def _mixer_in(x_parts, shift, scale, cos_t, sin_t, p, seq):
    fused = len(x_parts) == 3
    t, d = x_parts[0].shape
    tm = 512
    tiles_per_seq = seq // tm
    nz = p["w_in"].shape[1]
    hq = MLA_HEADS * HEAD_PAD

    def full(shape):
        return pl.BlockSpec(shape, lambda i: (0,) * len(shape))

    def mod():
        return pl.BlockSpec((1, 1, d), lambda i: (i // tiles_per_seq, 0, 0))

    x_specs = [pl.BlockSpec((tm, d), lambda i: (i, 0))]
    extra_out_specs, extra_out_shapes = [], []
    if fused:
        x_specs += [pl.BlockSpec((tm * ROW_TILE, LANES), lambda i: (i, 0)), mod()]
        extra_out_specs = [pl.BlockSpec((tm, d), lambda i: (i, 0))]
        extra_out_shapes = [jax.ShapeDtypeStruct((t, d), F32)]
    return pl.pallas_call(
        functools.partial(_mixer_in_kernel, tiles_per_seq=tiles_per_seq, fused=fused),
        grid=(t // tm,),
        in_specs=x_specs + [mod(), mod(),
                  full((d, nz)), full((POOL_WIDTH, POOL_WIDTH)), full((1, POOL_WIDTH)),
                  full((SG_HEADS, SG_CHUNK, SG_CHUNK)), full((SG_CHUNK, SG_WIDTH)),
                  full((1, Q_LORA)), full((Q_LORA, hq)), full((Q_LORA, hq)),
                  full((1, KV_LORA)), full((KV_LORA, hq)), full((KV_LORA, MLA_HEADS * V_DIM)),
                  pl.BlockSpec((tm, HEAD_PAD), lambda i: (i, 0)),
                  pl.BlockSpec((tm, HEAD_PAD), lambda i: (i, 0))],
        out_specs=[pl.BlockSpec((tm, POOL_WIDTH + SG_WIDTH), lambda i: (i, 0)),
                   pl.BlockSpec((tm, hq), lambda i: (i, 0)),
                   pl.BlockSpec((tm, hq), lambda i: (i, 0)),
                   pl.BlockSpec((tm, MLA_HEADS * V_DIM), lambda i: (i, 0))] + extra_out_specs,
        out_shape=[jax.ShapeDtypeStruct((t, POOL_WIDTH + SG_WIDTH), BF16),
                   jax.ShapeDtypeStruct((t, hq), BF16),
                   jax.ShapeDtypeStruct((t, hq), BF16),
                   jax.ShapeDtypeStruct((t, MLA_HEADS * V_DIM), BF16)] + extra_out_shapes,
        scratch_shapes=[pltpu.VMEM((MAX_WINDOW, POOL_WIDTH), F32)],
        compiler_params=pltpu.CompilerParams(dimension_semantics=("arbitrary",),
                                             vmem_limit_bytes=VMEM_LIMIT),
        name="mixer_in",
    )(*x_parts, shift, scale, p["w_in"], p["w_pool"], p["pool_scale"], p["w_spatial"], p["b_spatial"],
      p["g_q"], p["w_uq"], p["w_uq_sw"], p["g_kv"], p["w_k"], p["w_v"], cos_t, sin_t)


ATTN_TQ = 256
ATTN_LOOKAHEAD = 2


def _attn_kernel(q_ref, k_ref, v_ref, o_ref):
    seq = q_ref.shape[0]
    tq = ATTN_TQ
    r_i = lax.broadcasted_iota(jnp.int32, (tq, tq), 0)
    c_i = lax.broadcasted_iota(jnp.int32, (tq, tq), 1)
    lane = lax.broadcasted_iota(jnp.int32, (tq, 2 * V_DIM), 1)

    def scores(i, hh):
        lo, hi = i * tq, (i + 1) * tq
        blk = slice(hh * HEAD_PAD, (hh + 1) * HEAD_PAD)
        q = q_ref[lo:hi, blk]
        s_d = jnp.where(c_i <= r_i, _dot_nt(q, k_ref[lo:hi, blk]), -jnp.inf)
        s_o = _dot_nt(q, k_ref[0:lo, blk]) if i > 0 else None
        return s_d, s_o

    def finish(i, s_d, s_o):
        lo, hi = i * tq, (i + 1) * tq
        m = jnp.max(s_d, axis=-1, keepdims=True)
        if s_o is not None:
            m = jnp.maximum(m, jnp.max(s_o, axis=-1, keepdims=True))
            p_o = jnp.exp(s_o - m)
        p_d = jnp.exp(s_d - m)
        l = jnp.sum(p_d, axis=-1, keepdims=True)
        acc = _dot(p_d.astype(BF16), v_ref[lo:hi, :])
        if s_o is not None:
            l = l + jnp.sum(p_o, axis=-1, keepdims=True)
            acc = acc + _dot(p_o.astype(BF16), v_ref[0:lo, :])
        return acc * (1.0 / l)

    chains = [(i, hh) for i in range(seq // tq) for hh in range(2)]
    pending = [scores(*chains[c]) for c in range(ATTN_LOOKAHEAD)]
    outs = []
    for c, (i, hh) in enumerate(chains):
        if c + ATTN_LOOKAHEAD < len(chains):
            pending.append(scores(*chains[c + ATTN_LOOKAHEAD]))
        outs.append(finish(i, *pending.pop(0)))
        if hh == 1:
            o_ref[i * tq:(i + 1) * tq, :] = jnp.where(lane < V_DIM, outs[-2], outs[-1]).astype(o_ref.dtype)


def _attention(q, k, v, batch, seq):
    t = q.shape[0]
    return pl.pallas_call(
        _attn_kernel,
        grid=(batch, MLA_HEADS // 2),
        in_specs=[pl.BlockSpec((seq, 2 * HEAD_PAD), lambda b, hp: (b, hp)),
                  pl.BlockSpec((seq, 2 * HEAD_PAD), lambda b, hp: (b, hp)),
                  pl.BlockSpec((seq, 2 * V_DIM), lambda b, hp: (b, hp))],
        out_specs=pl.BlockSpec((seq, 2 * V_DIM), lambda b, hp: (b, hp)),
        out_shape=jax.ShapeDtypeStruct((t, MLA_HEADS * V_DIM), BF16),
        compiler_params=pltpu.CompilerParams(
            dimension_semantics=("arbitrary", "arbitrary"), vmem_limit_bytes=VMEM_LIMIT),
        name="attention",
    )(q, k, v)


def _first_max_index(cur, idx, sentinel):
    m = jnp.max(cur, axis=0, keepdims=True)
    first = jnp.min(jnp.where(cur == m, idx, sentinel), axis=0, keepdims=True)
    return m, first


def _route(logits_t, bias_t):
    n_tok = logits_t.shape[1]
    scores = jax.nn.sigmoid(logits_t)
    sel = scores + bias_t
    neg = -jnp.inf
    sub = lax.broadcasted_iota(jnp.int32, (GROUP_SIZE, n_tok), 0).astype(F32)
    gid = lax.broadcasted_iota(jnp.int32, (N_EXPERT_GROUPS, n_tok), 0).astype(F32)
    gscore = jnp.zeros((N_EXPERT_GROUPS, n_tok), F32)
    for g in range(N_EXPERT_GROUPS):
        s = sel[g * GROUP_SIZE:(g + 1) * GROUP_SIZE]
        m1, i1 = _first_max_index(s, sub, float(GROUP_SIZE))
        m2 = jnp.max(jnp.where(sub == i1, neg, s), axis=0, keepdims=True)
        gscore = jnp.where(gid == float(g), m1 + m2, gscore)
    eid = lax.broadcasted_iota(jnp.int32, (N_EXPERTS, n_tok), 0).astype(F32)
    egroup = jnp.floor(eid * (1.0 / GROUP_SIZE))
    allowed = jnp.zeros((N_EXPERTS, n_tok), F32)
    cur = gscore
    for _ in range(TOPK_GROUPS):
        _, gi = _first_max_index(cur, gid, float(N_EXPERT_GROUPS))
        cur = jnp.where(gid == gi, neg, cur)
        allowed = jnp.where(egroup == gi, 1.0, allowed)
    cur = jnp.where(allowed > 0.0, sel, neg)
    chosen = jnp.zeros((N_EXPERTS, n_tok), F32)
    for _ in range(TOP_K):
        _, ei = _first_max_index(cur, eid, float(N_EXPERTS))
        hit = eid == ei
        cur = jnp.where(hit, neg, cur)
        chosen = jnp.where(hit, 1.0, chosen)
    w = jnp.where(chosen > 0.0, scores, 0.0)
    return w / jnp.sum(w, axis=0, keepdims=True) * ROUTED_SCALE


def _split_bf16(x):
    hi = x.astype(BF16)
    lo = (x - hi.astype(F32)).astype(BF16)
    return hi, lo


ROW_TILE = 8


def _mixer_out_kernel(mix_ref, att_ref, x_ref, gate1_ref, shift_ref, scale_ref, gate2_ref, wo_ref, wr_ref,
                      rb_ref, sg_ref, su_ref, sd_ref, base_ref, h2r_ref, cw_ref):
    tm = x_ref.shape[0]
    half = mix_ref.shape[1]
    y = _dot(mix_ref[...], wo_ref[0:half, :]) + _dot(att_ref[...], wo_ref[half:, :])
    x2 = x_ref[...] + gate1_ref[0] * y
    h2 = _rms(x2) * (1.0 + scale_ref[0]) + shift_ref[0]
    for j in range(ROW_TILE):
        h2r_ref[pl.ds(j, tm, stride=ROW_TILE), :] = h2[:, j * LANES:(j + 1) * LANES]
    hb = h2.astype(BF16)
    act = _silu(_dot(hb, sg_ref[...])) * _dot(hb, su_ref[...])
    base_ref[...] = x2 + gate2_ref[0] * _dot(act.astype(BF16), sd_ref[...])
    h_hi, h_lo = _split_bf16(h2)
    w_hi, w_lo = _split_bf16(wr_ref[...])
    logits_t = _dot_nt(w_hi, h_hi) + (_dot_nt(w_hi, h_lo) + _dot_nt(w_lo, h_hi))
    cw_ref[...] = _route(logits_t, rb_ref[...])


def _mixer_out(mix, att, x2d, gate1, shift, scale, gate2, p, seq):
    t, d = x2d.shape
    tm = 512
    tiles_per_seq = seq // tm
    ff = p["ws_gate"].shape[1]

    def full(shape):
        return pl.BlockSpec(shape, lambda i: (0,) * len(shape))

    def mod():
        return pl.BlockSpec((1, 1, d), lambda i: (i // tiles_per_seq, 0, 0))

    return pl.pallas_call(
        _mixer_out_kernel,
        grid=(t // tm,),
        in_specs=[pl.BlockSpec((tm, mix.shape[1]), lambda i: (i, 0)),
                  pl.BlockSpec((tm, att.shape[1]), lambda i: (i, 0)),
                  pl.BlockSpec((tm, d), lambda i: (i, 0)), mod(), mod(), mod(), mod(),
                  full((d, d)), full((N_EXPERTS, d)), full((N_EXPERTS, 1)),
                  full((d, ff)), full((d, ff)), full((ff, d))],
        out_specs=[pl.BlockSpec((tm, d), lambda i: (i, 0)),
                   pl.BlockSpec((tm * ROW_TILE, LANES), lambda i: (i, 0)),
                   pl.BlockSpec((N_EXPERTS, tm), lambda i: (0, i))],
        out_shape=[jax.ShapeDtypeStruct((t, d), F32),
                   jax.ShapeDtypeStruct((t * ROW_TILE, LANES), F32),
                   jax.ShapeDtypeStruct((N_EXPERTS, t), F32)],
        compiler_params=pltpu.CompilerParams(dimension_semantics=("arbitrary",),
                                             vmem_limit_bytes=VMEM_LIMIT),
        name="mixer_out",
    )(mix, att, x2d, gate1, shift, scale, gate2, p["w_out"], p["w_router_t"], p["router_bias"],
      p["ws_gate"], p["ws_up"], p["ws_down"])


EXPERT_BLOCK = 128
TOK_BITS = 12
RMW_BATCH = 16
MOE_TOKENS = 4096
MOE_GROUP = 5
GROUP_ROWS = MOE_GROUP * EXPERT_BLOCK


def _list_len(ts):
    return -(-ts // GROUP_ROWS) * GROUP_ROWS


def _plan_kernel(cw_ref, tok_ref, w_ref, cnt_ref):
    n_e, ts = cw_ref.shape
    cw = cw_ref[...]
    chosen = cw > 0.0
    cf = jnp.where(chosen, 1.0, 0.0).astype(BF16)
    r_i = lax.broadcasted_iota(jnp.int32, (LANES, LANES), 0)
    c_i = lax.broadcasted_iota(jnp.int32, (LANES, LANES), 1)
    before = jnp.where(r_i < c_i, 1.0, 0.0).astype(BF16)
    ones = jnp.ones((LANES, LANES), BF16)
    carry = jnp.zeros((n_e, LANES), F32)
    ranks = []
    for k in range(ts // LANES):
        ck = cf[:, k * LANES:(k + 1) * LANES]
        ranks.append(_dot(ck, before) + carry)
        carry = carry + _dot(ck, ones)
    rank = jnp.concatenate(ranks, axis=1).astype(jnp.int32)
    lane = lax.broadcasted_iota(jnp.int32, (n_e, ts), 1)
    packed = jnp.where(chosen, ((lane - rank) << TOK_BITS) | lane, -1)
    w = jnp.where(chosen, cw, 0.0)
    for bit in range(ts.bit_length() - 1):
        step = 1 << bit
        src_p = pltpu.roll(packed, ts - step, 1)
        src_w = pltpu.roll(w, ts - step, 1)
        take = (src_p >= 0) & (lane < ts - step) & (((src_p >> (TOK_BITS + bit)) & 1) == 1)
        keep = (packed >= 0) & (((packed >> (TOK_BITS + bit)) & 1) == 0)
        packed = jnp.where(take, src_p, jnp.where(keep, packed, -1))
        w = jnp.where(take, src_w, jnp.where(keep, w, 0.0))
    count = carry[:, 0:1].astype(jnp.int32)
    valid = packed >= 0
    tok = packed & ((1 << TOK_BITS) - 1)
    tok_last = jnp.max(jnp.where(valid, tok, 0).astype(F32), axis=1, keepdims=True).astype(jnp.int32)
    w_last = jnp.sum(jnp.where(lane == count - 1, w, 0.0), axis=1, keepdims=True)
    batch_end = (count + (RMW_BATCH - 1)) & (-RMW_BATCH)
    tok = jnp.where(valid, tok, tok_last)
    w = jnp.where(valid, w, jnp.where(lane < batch_end, w_last, 0.0))
    n_tail = tok_ref.shape[2] - ts
    if n_tail:
        tok = jnp.concatenate([tok, jnp.broadcast_to(tok_last, (n_e, n_tail))], axis=1)
        w = jnp.concatenate([w, jnp.zeros((n_e, n_tail), F32)], axis=1)
    tok_ref[0] = tok * ROW_TILE
    w_ref[0] = w
    cnt_ref[0] = jnp.broadcast_to(count, (n_e, LANES))


def _plan(cw_t, ts):
    n_e, t = cw_t.shape
    n_sup = t // ts
    n_list = _list_len(ts)
    assert ts <= (1 << TOK_BITS)
    return pl.pallas_call(
        _plan_kernel,
        grid=(n_sup,),
        in_specs=[pl.BlockSpec((n_e, ts), lambda s: (0, s))],
        out_specs=[pl.BlockSpec((1, n_e, n_list), lambda s: (s, 0, 0)),
                   pl.BlockSpec((1, n_e, n_list), lambda s: (s, 0, 0)),
                   pl.BlockSpec((1, n_e, LANES), lambda s: (s, 0, 0))],
        out_shape=[jax.ShapeDtypeStruct((n_sup, n_e, n_list), jnp.int32),
                   jax.ShapeDtypeStruct((n_sup, n_e, n_list), F32),
                   jax.ShapeDtypeStruct((n_sup, n_e, LANES), jnp.int32)],
        compiler_params=pltpu.CompilerParams(dimension_semantics=("arbitrary",),
                                             vmem_limit_bytes=VMEM_LIMIT),
        name="plan",
    )(cw_t)


XT_PITCH = GROUP_ROWS + 1


SHORT_ROWS = 544


def _moe_gather(list_ref, first, h_ref, xt_ref, lo=0, hi=GROUP_ROWS):
    for r in range(lo, hi):
        start = pl.multiple_of(list_ref[0, 0, first + r], ROW_TILE)
        xt_ref[pl.ds(r, ROW_TILE, stride=XT_PITCH), :] = h_ref[pl.ds(start, ROW_TILE), :]


def _moe_scatter_add(list_ref, first, yt_ref, acc_ref, lo=0, hi=GROUP_ROWS):
    for grp in range(lo // RMW_BATCH, hi // RMW_BATCH):
        rows = range(grp * RMW_BATCH, (grp + 1) * RMW_BATCH)
        starts = [pl.multiple_of(list_ref[0, 0, first + r], ROW_TILE) for r in rows]
        olds = [acc_ref[pl.ds(st, ROW_TILE), :] for st in starts]
        news = [old + yt_ref[pl.ds(r, ROW_TILE, stride=XT_PITCH), :] for r, old in zip(rows, olds)]
        for st, new in zip(starts, news):
            acc_ref[pl.ds(st, ROW_TILE), :] = new


def _moe_experts(q, xt_ref, w_ref, wg_ref, wu_ref, wd_ref, yt_ref, rows=GROUP_ROWS):
    x = jnp.concatenate(
        [xt_ref[pl.ds(j * XT_PITCH, rows), :] for j in range(ROW_TILE)], axis=1).astype(BF16)
    act = _silu(_dot(x, wg_ref[0, 0].astype(BF16))) * _dot(x, wu_ref[0, 0].astype(BF16))
    y = _dot(act.astype(BF16), wd_ref[0, 0].astype(BF16))
    w_col = jnp.concatenate(
        [jnp.broadcast_to(w_ref[0, q * MOE_GROUP + k], (EXPERT_BLOCK, EXPERT_BLOCK)).T
         for k in range(MOE_GROUP)], axis=0)[0:rows]
    for j in range(ROW_TILE):
        yt_ref[pl.ds(j * XT_PITCH, rows), :] = y[:, j * LANES:(j + 1) * LANES] * w_col


def _groups(count):
    return jnp.maximum((count + (GROUP_ROWS - 1)) // GROUP_ROWS, 1)


def _moe_kernel(cnt_ref, tokp_ref, tok_ref, tokn_ref, w_ref, h_ref, wg_ref, wu_ref, wd_ref, acc_ref,
                xt0_ref, xt1_ref, yt_ref):
    s = pl.program_id(0)
    e = pl.program_id(1)
    last_e = pl.num_programs(1) - 1
    cnt = cnt_ref[s, e]
    cnt_prev = cnt_ref[s, jnp.maximum(e - 1, 0)]
    cnt_next = cnt_ref[s, jnp.minimum(e + 1, last_e)]
    n = _groups(cnt)
    n_prev = _groups(cnt_prev)
    prev_first = (n_prev - 1) * GROUP_ROWS
    short = cnt <= SHORT_ROWS

    @pl.when(e == 0)
    def _():
        acc_ref[...] = jnp.zeros_like(acc_ref)
        yt_ref[...] = jnp.zeros_like(yt_ref)
        _moe_gather(tok_ref, 0, h_ref, xt0_ref)

    def experts(q, xt_ref, rows=GROUP_ROWS):
        _moe_experts(q, xt_ref, w_ref, wg_ref, wu_ref, wd_ref, yt_ref, rows)

    def step(xt_cur, xt_next):
        @pl.when(cnt_prev > SHORT_ROWS)
        def _():
            _moe_scatter_add(tokp_ref, prev_first, yt_ref, acc_ref, SHORT_ROWS, GROUP_ROWS)

        @pl.when(cnt_next > SHORT_ROWS)
        def _():
            _moe_gather(tokn_ref, 0, h_ref, xt_next, SHORT_ROWS, GROUP_ROWS)

        single = (n == 1) & (n_prev == 1)

        @pl.when(single & short)
        def _():
            _moe_gather(tokn_ref, 0, h_ref, xt_next, 0, SHORT_ROWS)
            _moe_scatter_add(tokp_ref, 0, yt_ref, acc_ref, 0, SHORT_ROWS)
            experts(0, xt_cur, SHORT_ROWS)

        @pl.when(single & jnp.logical_not(short))
        def _():
            _moe_gather(tokn_ref, 0, h_ref, xt_next, 0, SHORT_ROWS)
            _moe_scatter_add(tokp_ref, 0, yt_ref, acc_ref, 0, SHORT_ROWS)
            experts(0, xt_cur)

        @pl.when(jnp.logical_not(single))
        def _():
            _moe_gather(tokn_ref, 0, h_ref, xt_next, 0, SHORT_ROWS)
            _moe_scatter_add(tokp_ref, prev_first, yt_ref, acc_ref, 0, SHORT_ROWS)
            _moe_gather(tok_ref, 0, h_ref, xt_cur, SHORT_ROWS, GROUP_ROWS)
            experts(0, xt_cur)

            def more(q, carry):
                _moe_scatter_add(tok_ref, (q - 1) * GROUP_ROWS, yt_ref, acc_ref)
                _moe_gather(tok_ref, q * GROUP_ROWS, h_ref, xt_cur)
                experts(q, xt_cur)
                return carry

            lax.fori_loop(1, n, more, 0)

    @pl.when(e % 2 == 0)
    def _():
        step(xt0_ref, xt1_ref)

    @pl.when(e % 2 == 1)
    def _():
        step(xt1_ref, xt0_ref)

    @pl.when(e == last_e)
    def _():
        _moe_scatter_add(tok_ref, (n - 1) * GROUP_ROWS, yt_ref, acc_ref, 0, SHORT_ROWS)

    @pl.when((e == last_e) & (cnt > SHORT_ROWS))
    def _():
        _moe_scatter_add(tok_ref, (n - 1) * GROUP_ROWS, yt_ref, acc_ref, SHORT_ROWS, GROUP_ROWS)


def _moe(h2rows, tok, wts, counts, wg, wu, wd, layer, ts):
    rows = h2rows.shape[0]
    n_sup = rows // (ts * ROW_TILE)
    d, ff = wg.shape[2], wg.shape[3]
    n_list = tok.shape[2]
    assert d == ROW_TILE * LANES and n_list == _list_len(ts)

    assert N_EXPERTS % 2 == 0

    def list_idx(shift):
        def idx(s, e, cnt):
            return (s * N_EXPERTS + jnp.clip(e + shift, 0, N_EXPERTS - 1), 0, 0)
        return idx

    def sup_idx(s, e, cnt):
        return (s, 0)

    def w_idx(s, e, cnt):
        return (layer, e, 0, 0)

    def list_spec(shift):
        return pl.BlockSpec((1, 1, n_list), list_idx(shift), memory_space=pltpu.SMEM)

    once = pl.Buffered(1)
    tile = pltpu.VMEM((ROW_TILE * XT_PITCH, LANES), F32)
    grid_spec = pltpu.PrefetchScalarGridSpec(
        num_scalar_prefetch=1,
        grid=(n_sup, N_EXPERTS),
        in_specs=[list_spec(-1), list_spec(0), list_spec(1),
                  pl.BlockSpec((1, n_list // EXPERT_BLOCK, 1, EXPERT_BLOCK),
                               lambda s, e, cnt: (s * N_EXPERTS + e, 0, 0, 0)),
                  pl.BlockSpec((ts * ROW_TILE, LANES), sup_idx, pipeline_mode=once),
                  pl.BlockSpec((1, 1, d, ff), w_idx),
                  pl.BlockSpec((1, 1, d, ff), w_idx),
                  pl.BlockSpec((1, 1, ff, d), w_idx)],
        out_specs=pl.BlockSpec((ts * ROW_TILE, LANES), sup_idx, pipeline_mode=once),
        scratch_shapes=[tile, tile, tile],
    )
    lists = tok.reshape(n_sup * N_EXPERTS, 1, n_list)
    return pl.pallas_call(
        _moe_kernel,
        grid_spec=grid_spec,
        out_shape=jax.ShapeDtypeStruct((rows, LANES), F32),
        compiler_params=pltpu.CompilerParams(dimension_semantics=("arbitrary", "arbitrary"),
                                             vmem_limit_bytes=VMEM_LIMIT),
        name="moe",
    )(counts, lists, lists, lists,
      wts.reshape(n_sup * N_EXPERTS, n_list // EXPERT_BLOCK, 1, EXPERT_BLOCK), h2rows, wg, wu, wd)


def _routed_rows_to_tile(r_ref, tm):
    return jnp.concatenate([r_ref[pl.ds(j, tm, stride=ROW_TILE), :] for j in range(ROW_TILE)], axis=1)


def _final_kernel(base_ref, r_ref, gate_ref, fg_ref, o_ref):
    tm = base_ref.shape[0]
    x = base_ref[...] + gate_ref[0] * _routed_rows_to_tile(r_ref, tm)
    o_ref[...] = _rms(x) * fg_ref[...]


def _final(base, routed, gate, final_gain, seq):
    t, d = base.shape
    tm = 512
    tiles_per_seq = seq // tm
    return pl.pallas_call(
        _final_kernel,
        grid=(t // tm,),
        in_specs=[pl.BlockSpec((tm, d), lambda i: (i, 0)),
                  pl.BlockSpec((tm * ROW_TILE, LANES), lambda i: (i, 0)),
                  pl.BlockSpec((1, 1, d), lambda i: (i // tiles_per_seq, 0, 0)),
                  pl.BlockSpec((1, d), lambda i: (0, 0))],
        out_specs=pl.BlockSpec((tm, d), lambda i: (i, 0)),
        out_shape=jax.ShapeDtypeStruct((t, d), F32),
        compiler_params=pltpu.CompilerParams(dimension_semantics=("arbitrary",),
                                             vmem_limit_bytes=VMEM_LIMIT),
        name="final_norm",
    )(base, routed, gate, final_gain)


def _prep_layer(w_in, w_pool, pool_scale, w_spatial, b_spatial, g_q, w_uq, g_kv, w_ukv, w_out,
                w_router, router_bias, ws_gate, ws_up, ws_down):
    d = w_in.shape[0]
    o_kpe = POOL_WIDTH + 2 * SG_WIDTH + Q_LORA + KV_LORA
    x1 = w_in[:, o_kpe:o_kpe + HALF_ROPE]
    x2 = w_in[:, o_kpe + HALF_ROPE:o_kpe + QK_ROPE]
    zl = jnp.zeros((d, QK_NOPE), F32)
    zr = jnp.zeros((d, HEAD_PAD - QK_NOPE - QK_ROPE), F32)
    w_in_pad = jnp.concatenate([w_in[:, :o_kpe], zl, x1, x2, zr, zl, x2, x1, zr], axis=1)

    uq = w_uq.reshape(Q_LORA, MLA_HEADS, QK_NOPE + QK_ROPE)
    q1 = uq[..., QK_NOPE:QK_NOPE + HALF_ROPE]
    q2 = uq[..., QK_NOPE + HALF_ROPE:]
    zq = jnp.zeros((Q_LORA, MLA_HEADS, HEAD_PAD - QK_NOPE - QK_ROPE), F32)
    w_uq_pad = jnp.concatenate([uq, zq], axis=-1).reshape(Q_LORA, MLA_HEADS * HEAD_PAD)
    w_uq_sw = jnp.concatenate([jnp.zeros_like(uq[..., :QK_NOPE]), q2, q1, zq], axis=-1)
    w_uq_sw = w_uq_sw.reshape(Q_LORA, MLA_HEADS * HEAD_PAD)

    ukv = w_ukv.reshape(KV_LORA, MLA_HEADS, QK_NOPE + V_DIM)
    zk = jnp.zeros((KV_LORA, MLA_HEADS, HEAD_PAD - QK_NOPE), F32)
    w_k = jnp.concatenate([ukv[..., :QK_NOPE], zk], axis=-1).reshape(KV_LORA, MLA_HEADS * HEAD_PAD)
    w_v = ukv[..., QK_NOPE:].reshape(KV_LORA, MLA_HEADS * V_DIM)

    w_pool_bd = jax.scipy.linalg.block_diag(*[w_pool[g] for g in range(len(POOL_WINDOWS))])
    b_sp = jnp.repeat(b_spatial.T, SG_DIM, axis=1)
    return {
        "w_in": w_in_pad.astype(BF16), "w_pool": w_pool_bd.astype(BF16),
        "pool_scale": pool_scale.reshape(1, -1), "w_spatial": w_spatial, "b_spatial": b_sp,
        "g_q": g_q.reshape(1, -1), "w_uq": w_uq_pad.astype(BF16), "w_uq_sw": w_uq_sw.astype(BF16),
        "g_kv": g_kv.reshape(1, -1), "w_k": w_k.astype(BF16), "w_v": w_v.astype(BF16),
        "w_out": w_out.astype(BF16), "w_router_t": w_router.T, "router_bias": router_bias.reshape(-1, 1),
        "ws_gate": ws_gate.astype(BF16), "ws_up": ws_up.astype(BF16), "ws_down": ws_down.astype(BF16),
    }


def kernel(x, c, positions, w_ada, b_ada, w_in, w_pool, pool_scale, w_spatial, b_spatial, g_q, w_uq, g_kv, w_ukv, w_out, w_router, router_bias, w_gate, w_up, w_down, ws_gate, ws_up, ws_down, final_gain):
    batch, seq, d = x.shape
    depth = w_ada.shape[0]
    mod = _ada_mod(c, w_ada, b_ada)
    cos_t, sin_t = _rope_tables(positions)
    x_parts = (x.reshape(batch * seq, d),)
    fg = final_gain.reshape(1, d)
    wg_b, wu_b, wd_b = w_gate, w_up, w_down
    ts = min(MOE_TOKENS, batch * seq)
    for l in range(depth):
        p = _prep_layer(w_in[l], w_pool[l], pool_scale[l], w_spatial[l], b_spatial[l], g_q[l], w_uq[l],
                        g_kv[l], w_ukv[l], w_out[l], w_router[l], router_bias[l], ws_gate[l], ws_up[l],
                        ws_down[l])
        shift1, scale1, gate1, shift2, scale2, gate2 = [
            mod[l, :, k * d:(k + 1) * d].reshape(batch, 1, d) for k in range(6)]
        outs = _mixer_in(x_parts, shift1, scale1, cos_t, sin_t, p, seq)
        mix, q, k, v = outs[:4]
        xt = outs[4] if len(outs) == 5 else x_parts[0]
        att = _attention(q, k, v, batch, seq)
        base, h2rows, cw_t = _mixer_out(mix, att, xt, gate1, shift2, scale2, gate2, p, seq)
        tok, wts, cnt = _plan(cw_t, ts)
        routed = _moe(h2rows, tok, wts, cnt[:, :, 0], wg_b, wu_b, wd_b, l, ts)
        x_parts = (base, routed, gate2)
    return _final(*x_parts, fg, seq).reshape(batch, seq, d)
```

```python
import functools

import jax
import jax.numpy as jnp
from jax import lax
from jax.experimental import pallas as pl
from jax.experimental.pallas import tpu as pltpu

F32 = jnp.float32
BF16 = jnp.bfloat16

EPS = 1e-6
LANES = 128
POOL_WINDOWS = (2, 4, 8, 16)
POOL_WIDTH = 256
POOL_CH = 64
MAX_WINDOW = 16
SG_HEADS = 4
SG_WIDTH = 256
SG_DIM = 64
SG_CHUNK = 128
MLA_HEADS = 8
V_DIM = 64
QK_NOPE = 64
QK_ROPE = 32
HALF_ROPE = QK_ROPE // 2
Q_LORA = 256
KV_LORA = 128
ROPE_THETA = 10000.0
ATTN_SCALE = (QK_NOPE + QK_ROPE) ** -0.5
HEAD_PAD = 128
N_EXPERTS = 64
TOP_K = 8
N_EXPERT_GROUPS = 8
GROUP_SIZE = N_EXPERTS // N_EXPERT_GROUPS
TOPK_GROUPS = 4
ROUTED_SCALE = 2.5
VMEM_LIMIT = 52 * 1024 * 1024


def _dot(a, b):
    return jnp.dot(a, b, preferred_element_type=F32)


def _dot_nt(a, b):
    return lax.dot_general(a, b, (((1,), (1,)), ((), ())), preferred_element_type=F32)


def _rms(x):
    return x * lax.rsqrt(jnp.mean(x * x, axis=-1, keepdims=True) + EPS)


def _gelu_tanh(x):
    c = (2.0 / jnp.pi) ** 0.5
    return x * (0.5 * (1.0 + jnp.tanh(c * (x + 0.044715 * (x * x * x)))))


def _silu(x):
    return x * jax.nn.sigmoid(x)


def _ada_kernel(c_ref, w_ref, b_ref, o_ref):
    cond = _silu(c_ref[...])
    o_ref[0] = _dot(cond.astype(BF16), w_ref[0].astype(BF16)) + b_ref[0]


def _ada_mod(c, w_ada, b_ada):
    depth, d, n = w_ada.shape
    b = c.shape[0]
    nt = 1536
    return pl.pallas_call(
        _ada_kernel,
        grid=(depth, n // nt),
        in_specs=[pl.BlockSpec((b, d), lambda l, j: (0, 0)),
                  pl.BlockSpec((1, d, nt), lambda l, j: (l, 0, j)),
                  pl.BlockSpec((1, 1, nt), lambda l, j: (l, 0, j))],
        out_specs=pl.BlockSpec((1, b, nt), lambda l, j: (l, 0, j)),
        out_shape=jax.ShapeDtypeStruct((depth, b, n), F32),
        compiler_params=pltpu.CompilerParams(vmem_limit_bytes=VMEM_LIMIT),
        name="ada_mod",
    )(c, w_ada, b_ada.reshape(depth, 1, n))


def _rope_kernel(pos_ref, invf_ref, sign_ref, cos_ref, sin_ref):
    ang = pos_ref[...].astype(F32) * invf_ref[...]
    cos_ref[...] = jnp.cos(ang)
    sin_ref[...] = jnp.sin(ang) * sign_ref[...]


def _rope_tables(positions):
    t = positions.size
    tm = 2048
    inv_freq = ROPE_THETA ** (-jnp.arange(0, QK_ROPE, 2, dtype=F32) / QK_ROPE)
    invf = jnp.zeros((1, HEAD_PAD), F32)
    invf = invf.at[0, QK_NOPE:QK_NOPE + HALF_ROPE].set(inv_freq)
    invf = invf.at[0, QK_NOPE + HALF_ROPE:QK_NOPE + QK_ROPE].set(inv_freq)
    sign = jnp.zeros((1, HEAD_PAD), F32)
    sign = sign.at[0, QK_NOPE:QK_NOPE + HALF_ROPE].set(-1.0)
    sign = sign.at[0, QK_NOPE + HALF_ROPE:QK_NOPE + QK_ROPE].set(1.0)
    return pl.pallas_call(
        _rope_kernel,
        grid=(t // tm,),
        in_specs=[pl.BlockSpec((tm, 1), lambda i: (i, 0)),
                  pl.BlockSpec((1, HEAD_PAD), lambda i: (0, 0)),
                  pl.BlockSpec((1, HEAD_PAD), lambda i: (0, 0))],
        out_specs=[pl.BlockSpec((tm, HEAD_PAD), lambda i: (i, 0)),
                   pl.BlockSpec((tm, HEAD_PAD), lambda i: (i, 0))],
        out_shape=[jax.ShapeDtypeStruct((t, HEAD_PAD), F32)] * 2,
        name="rope_tables",
    )(positions.reshape(t, 1), invf, sign)


def _mixer_in_kernel(*refs, tiles_per_seq, fused):
    if fused:
        base_ref, r_ref, gprev_ref = refs[:3]
        (shift_ref, scale_ref, win_ref, wpool_ref, pscale_ref, wsp_ref, bsp_ref, gq_ref, wuq_ref, wuqs_ref,
         gkv_ref, wk_ref, wv_ref, cos_ref, sin_ref, mix_ref, q_ref, k_ref, v_ref, x_out_ref, carry_ref) = refs[3:]
        tm = base_ref.shape[0]
        x = base_ref[...] + gprev_ref[0] * jnp.concatenate(
            [r_ref[pl.ds(j, tm, stride=ROW_TILE), :] for j in range(ROW_TILE)], axis=1)
        x_out_ref[...] = x
    else:
        (x_ref, shift_ref, scale_ref, win_ref, wpool_ref, pscale_ref, wsp_ref, bsp_ref, gq_ref, wuq_ref, wuqs_ref,
         gkv_ref, wk_ref, wv_ref, cos_ref, sin_ref, mix_ref, q_ref, k_ref, v_ref, carry_ref) = refs
        tm = x_ref.shape[0]
        x = x_ref[...]
    ti = pl.program_id(0) % tiles_per_seq
    h = _rms(x) * (1.0 + scale_ref[0]) + shift_ref[0]
    z = _dot(h.astype(BF16), win_ref[...])

    a = z[:, 0:POOL_WIDTH]

    @pl.when(ti == 0)
    def _():
        carry_ref[...] = jnp.zeros_like(carry_ref)

    ext = jnp.concatenate([carry_ref[...], a], axis=0)
    carry_ref[...] = a[tm - MAX_WINDOW:, :]
    p1 = ext + pltpu.roll(ext, 1, 0)
    p2 = p1 + pltpu.roll(p1, 2, 0)
    p3 = p2 + pltpu.roll(p2, 4, 0)
    p4 = p3 + pltpu.roll(p3, 8, 0)
    lane = lax.broadcasted_iota(jnp.int32, (tm, POOL_WIDTH), 1)
    row = lax.broadcasted_iota(jnp.int32, (tm, POOL_WIDTH), 0) + (ti * tm + 1)
    g0, g1, g2 = lane < POOL_CH, lane < 2 * POOL_CH, lane < 3 * POOL_CH
    wsum = jnp.where(g0, p1[MAX_WINDOW:], jnp.where(g1, p2[MAX_WINDOW:],
                     jnp.where(g2, p3[MAX_WINDOW:], p4[MAX_WINDOW:])))
    width = jnp.where(g0, POOL_WINDOWS[0], jnp.where(g1, POOL_WINDOWS[1],
                      jnp.where(g2, POOL_WINDOWS[2], POOL_WINDOWS[3])))
    cnt = jnp.minimum(row, width).astype(F32)
    dlt = wsum / cnt - a
    y_pool = _dot(dlt.astype(BF16), wpool_ref[...]) * pscale_ref[...]
    mix_ref[:, 0:POOL_WIDTH] = y_pool.astype(mix_ref.dtype)

    ug = _gelu_tanh(z[:, POOL_WIDTH:POOL_WIDTH + SG_WIDTH])
    vg = _gelu_tanh(z[:, POOL_WIDTH + SG_WIDTH:POOL_WIDTH + 2 * SG_WIDTH])
    mu = jnp.mean(vg, axis=-1, keepdims=True)
    vc = vg - mu
    var = jnp.mean(vc * vc, axis=-1, keepdims=True)
    vn = (vc * lax.rsqrt(var + EPS)).astype(BF16)
    r_i = lax.broadcasted_iota(jnp.int32, (SG_CHUNK, SG_CHUNK), 0)
    c_i = lax.broadcasted_iota(jnp.int32, (SG_CHUNK, SG_CHUNK), 1)
    wms = [jnp.where(c_i <= r_i, wsp_ref[hh], 0.0).astype(BF16) for hh in range(SG_HEADS)]
    lane_head = lax.broadcasted_iota(jnp.int32, (SG_CHUNK, SG_WIDTH), 1) // SG_DIM
    for cidx in range(tm // SG_CHUNK):
        rows = slice(cidx * SG_CHUNK, (cidx + 1) * SG_CHUNK)
        vchunk = vn[rows]
        mixed = bsp_ref[...]
        for hh in range(SG_HEADS):
            mixed = mixed + jnp.where(lane_head == hh, _dot(wms[hh], vchunk), 0.0)
        mix_ref[rows, POOL_WIDTH:POOL_WIDTH + SG_WIDTH] = (ug[rows] * mixed).astype(mix_ref.dtype)

    o_cq = POOL_WIDTH + 2 * SG_WIDTH
    o_ckv = o_cq + Q_LORA
    o_kpe = o_ckv + KV_LORA
    cos = cos_ref[...]
    sin = sin_ref[...]
    cqn = (_rms(z[:, o_cq:o_ckv]) * gq_ref[...]).astype(BF16)
    q = _dot(cqn, wuq_ref[...])
    qs = _dot(cqn, wuqs_ref[...])
    ckvn = (_rms(z[:, o_ckv:o_kpe]) * gkv_ref[...]).astype(BF16)
    kn = _dot(ckvn, wk_ref[...])
    kpe = z[:, o_kpe:o_kpe + HEAD_PAD] * cos + z[:, o_kpe + HEAD_PAD:o_kpe + 2 * HEAD_PAD] * sin
    for hh in range(MLA_HEADS):
        blk = slice(hh * HEAD_PAD, (hh + 1) * HEAD_PAD)
        q_ref[:, blk] = ((q[:, blk] * cos + qs[:, blk] * sin) * ATTN_SCALE).astype(q_ref.dtype)
        k_ref[:, blk] = (kn[:, blk] + kpe).astype(k_ref.dtype)
    v_ref[...] = _dot(ckvn, wv_ref[...]).astype(v_ref.dtype)


def _mixer_in(x_parts, shift, scale, cos_t, sin_t, p, seq):
    fused = len(x_parts) == 3
    t, d = x_parts[0].shape
    tm = 512
    tiles_per_seq = seq // tm
    nz = p["w_in"].shape[1]
    hq = MLA_HEADS * HEAD_PAD

    def full(shape):
        return pl.BlockSpec(shape, lambda i: (0,) * len(shape))

    def mod():
        return pl.BlockSpec((1, 1, d), lambda i: (i // tiles_per_seq, 0, 0))

    x_specs = [pl.BlockSpec((tm, d), lambda i: (i, 0))]
    extra_out_specs, extra_out_shapes = [], []
    if fused:
        x_specs += [pl.BlockSpec((tm * ROW_TILE, LANES), lambda i: (i, 0)), mod()]
        extra_out_specs = [pl.BlockSpec((tm, d), lambda i: (i, 0))]
        extra_out_shapes = [jax.ShapeDtypeStruct((t, d), F32)]
    return pl.pallas_call(
        functools.partial(_mixer_in_kernel, tiles_per_seq=tiles_per_seq, fused=fused),
        grid=(t // tm,),
        in_specs=x_specs + [mod(), mod(),
                  full((d, nz)), full((POOL_WIDTH, POOL_WIDTH)), full((1, POOL_WIDTH)),
                  full((SG_HEADS, SG_CHUNK, SG_CHUNK)), full((SG_CHUNK, SG_WIDTH)),
                  full((1, Q_LORA)), full((Q_LORA, hq)), full((Q_LORA, hq)),
                  full((1, KV_LORA)), full((KV_LORA, hq)), full((KV_LORA, MLA_HEADS * V_DIM)),
                  pl.BlockSpec((tm, HEAD_PAD), lambda i: (i, 0)),
                  pl.BlockSpec((tm, HEAD_PAD), lambda i: (i, 0))],
        out_specs=[pl.BlockSpec((tm, POOL_WIDTH + SG_WIDTH), lambda i: (i, 0)),
                   pl.BlockSpec((tm, hq), lambda i: (i, 0)),
                   pl.BlockSpec((tm, hq), lambda i: (i, 0)),
                   pl.BlockSpec((tm, MLA_HEADS * V_DIM), lambda i: (i, 0))] + extra_out_specs,
        out_shape=[jax.ShapeDtypeStruct((t, POOL_WIDTH + SG_WIDTH), BF16),
                   jax.ShapeDtypeStruct((t, hq), BF16),
                   jax.ShapeDtypeStruct((t, hq), BF16),
                   jax.ShapeDtypeStruct((t, MLA_HEADS * V_DIM), BF16)] + extra_out_shapes,
        scratch_shapes=[pltpu.VMEM((MAX_WINDOW, POOL_WIDTH), F32)],
        compiler_params=pltpu.CompilerParams(dimension_semantics=("arbitrary",),
                                             vmem_limit_bytes=VMEM_LIMIT),
        name="mixer_in",
    )(*x_parts, shift, scale, p["w_in"], p["w_pool"], p["pool_scale"], p["w_spatial"], p["b_spatial"],
      p["g_q"], p["w_uq"], p["w_uq_sw"], p["g_kv"], p["w_k"], p["w_v"], cos_t, sin_t)


ATTN_TQ = 256
ATTN_LOOKAHEAD = 2


def _attn_kernel(q_ref, k_ref, v_ref, o_ref):
    seq = q_ref.shape[0]
    tq = ATTN_TQ
    r_i = lax.broadcasted_iota(jnp.int32, (tq, tq), 0)
    c_i = lax.broadcasted_iota(jnp.int32, (tq, tq), 1)
    lane = lax.broadcasted_iota(jnp.int32, (tq, 2 * V_DIM), 1)

    def scores(i, hh):
        lo, hi = i * tq, (i + 1) * tq
        blk = slice(hh * HEAD_PAD, (hh + 1) * HEAD_PAD)
        q = q_ref[lo:hi, blk]
        s_d = jnp.where(c_i <= r_i, _dot_nt(q, k_ref[lo:hi, blk]), -jnp.inf)
        s_o = _dot_nt(q, k_ref[0:lo, blk]) if i > 0 else None
        return s_d, s_o

    def finish(i, s_d, s_o):
        lo, hi = i * tq, (i + 1) * tq
        m = jnp.max(s_d, axis=-1, keepdims=True)
        if s_o is not None:
            m = jnp.maximum(m, jnp.max(s_o, axis=-1, keepdims=True))
            p_o = jnp.exp(s_o - m)
        p_d = jnp.exp(s_d - m)
        l = jnp.sum(p_d, axis=-1, keepdims=True)
        acc = _dot(p_d.astype(BF16), v_ref[lo:hi, :])
        if s_o is not None:
            l = l + jnp.sum(p_o, axis=-1, keepdims=True)
            acc = acc + _dot(p_o.astype(BF16), v_ref[0:lo, :])
        return acc * (1.0 / l)

    chains = [(i, hh) for i in range(seq // tq) for hh in range(2)]
    pending = [scores(*chains[c]) for c in range(ATTN_LOOKAHEAD)]
    outs = []
    for c, (i, hh) in enumerate(chains):
        if c + ATTN_LOOKAHEAD < len(chains):
            pending.append(scores(*chains[c + ATTN_LOOKAHEAD]))
        outs.append(finish(i, *pending.pop(0)))
        if hh == 1:
            o_ref[i * tq:(i + 1) * tq, :] = jnp.where(lane < V_DIM, outs[-2], outs[-1]).astype(o_ref.dtype)


def _attention(q, k, v, batch, seq):
    t = q.shape[0]
    return pl.pallas_call(
        _attn_kernel,
        grid=(batch, MLA_HEADS // 2),
        in_specs=[pl.BlockSpec((seq, 2 * HEAD_PAD), lambda b, hp: (b, hp)),
                  pl.BlockSpec((seq, 2 * HEAD_PAD), lambda b, hp: (b, hp)),
                  pl.BlockSpec((seq, 2 * V_DIM), lambda b, hp: (b, hp))],
        out_specs=pl.BlockSpec((seq, 2 * V_DIM), lambda b, hp: (b, hp)),
        out_shape=jax.ShapeDtypeStruct((t, MLA_HEADS * V_DIM), BF16),
        compiler_params=pltpu.CompilerParams(
            dimension_semantics=("arbitrary", "arbitrary"), vmem_limit_bytes=VMEM_LIMIT),
        name="attention",
    )(q, k, v)


def _first_max_index(cur, idx, sentinel):
    m = jnp.max(cur, axis=0, keepdims=True)
    first = jnp.min(jnp.where(cur == m, idx, sentinel), axis=0, keepdims=True)
    return m, first


def _route(logits_t, bias_t):
    n_tok = logits_t.shape[1]
    scores = jax.nn.sigmoid(logits_t)
    sel = scores + bias_t
    neg = -jnp.inf
    sub = lax.broadcasted_iota(jnp.int32, (GROUP_SIZE, n_tok), 0).astype(F32)
    gid = lax.broadcasted_iota(jnp.int32, (N_EXPERT_GROUPS, n_tok), 0).astype(F32)
    gscore = jnp.zeros((N_EXPERT_GROUPS, n_tok), F32)
    for g in range(N_EXPERT_GROUPS):
        s = sel[g * GROUP_SIZE:(g + 1) * GROUP_SIZE]
        m1, i1 = _first_max_index(s, sub, float(GROUP_SIZE))
        m2 = jnp.max(jnp.where(sub == i1, neg, s), axis=0, keepdims=True)
        gscore = jnp.where(gid == float(g), m1 + m2, gscore)
    eid = lax.broadcasted_iota(jnp.int32, (N_EXPERTS, n_tok), 0).astype(F32)
    egroup = jnp.floor(eid * (1.0 / GROUP_SIZE))
    allowed = jnp.zeros((N_EXPERTS, n_tok), F32)
    cur = gscore
    for _ in range(TOPK_GROUPS):
        _, gi = _first_max_index(cur, gid, float(N_EXPERT_GROUPS))
        cur = jnp.where(gid == gi, neg, cur)
        allowed = jnp.where(egroup == gi, 1.0, allowed)
    cur = jnp.where(allowed > 0.0, sel, neg)
    chosen = jnp.zeros((N_EXPERTS, n_tok), F32)
    for _ in range(TOP_K):
        _, ei = _first_max_index(cur, eid, float(N_EXPERTS))
        hit = eid == ei
        cur = jnp.where(hit, neg, cur)
        chosen = jnp.where(hit, 1.0, chosen)
    w = jnp.where(chosen > 0.0, scores, 0.0)
    return w / jnp.sum(w, axis=0, keepdims=True) * ROUTED_SCALE


def _split_bf16(x):
    hi = x.astype(BF16)
    lo = (x - hi.astype(F32)).astype(BF16)
    return hi, lo


ROW_TILE = 8


def _mixer_out_kernel(mix_ref, att_ref, x_ref, gate1_ref, shift_ref, scale_ref, gate2_ref, wo_ref, wr_ref,
                      rb_ref, sg_ref, su_ref, sd_ref, base_ref, h2r_ref, cw_ref):
    tm = x_ref.shape[0]
    half = mix_ref.shape[1]
    y = _dot(mix_ref[...], wo_ref[0:half, :]) + _dot(att_ref[...], wo_ref[half:, :])
    x2 = x_ref[...] + gate1_ref[0] * y
    h2 = _rms(x2) * (1.0 + scale_ref[0]) + shift_ref[0]
    for j in range(ROW_TILE):
        h2r_ref[pl.ds(j, tm, stride=ROW_TILE), :] = h2[:, j * LANES:(j + 1) * LANES]
    hb = h2.astype(BF16)
    act = _silu(_dot(hb, sg_ref[...])) * _dot(hb, su_ref[...])
    base_ref[...] = x2 + gate2_ref[0] * _dot(act.astype(BF16), sd_ref[...])
    h_hi, h_lo = _split_bf16(h2)
    w_hi, w_lo = _split_bf16(wr_ref[...])
    logits_t = _dot_nt(w_hi, h_hi) + (_dot_nt(w_hi, h_lo) + _dot_nt(w_lo, h_hi))
    cw_ref[...] = _route(logits_t, rb_ref[...])


def _mixer_out(mix, att, x2d, gate1, shift, scale, gate2, p, seq):
    t, d = x2d.shape
    tm = 512
    tiles_per_seq = seq // tm
    ff = p["ws_gate"].shape[1]

    def full(shape):
        return pl.BlockSpec(shape, lambda i: (0,) * len(shape))

    def mod():
        return pl.BlockSpec((1, 1, d), lambda i: (i // tiles_per_seq, 0, 0))

    return pl.pallas_call(
        _mixer_out_kernel,
        grid=(t // tm,),
        in_specs=[pl.BlockSpec((tm, mix.shape[1]), lambda i: (i, 0)),
                  pl.BlockSpec((tm, att.shape[1]), lambda i: (i, 0)),
                  pl.BlockSpec((tm, d), lambda i: (i, 0)), mod(), mod(), mod(), mod(),
                  full((d, d)), full((N_EXPERTS, d)), full((N_EXPERTS, 1)),
                  full((d, ff)), full((d, ff)), full((ff, d))],
        out_specs=[pl.BlockSpec((tm, d), lambda i: (i, 0)),
                   pl.BlockSpec((tm * ROW_TILE, LANES), lambda i: (i, 0)),
                   pl.BlockSpec((N_EXPERTS, tm), lambda i: (0, i))],
        out_shape=[jax.ShapeDtypeStruct((t, d), F32),
                   jax.ShapeDtypeStruct((t * ROW_TILE, LANES), F32),
                   jax.ShapeDtypeStruct((N_EXPERTS, t), F32)],
        compiler_params=pltpu.CompilerParams(dimension_semantics=("arbitrary",),
                                             vmem_limit_bytes=VMEM_LIMIT),
        name="mixer_out",
    )(mix, att, x2d, gate1, shift, scale, gate2, p["w_out"], p["w_router_t"], p["router_bias"],
      p["ws_gate"], p["ws_up"], p["ws_down"])


EXPERT_BLOCK = 128
TOK_BITS = 12
RMW_BATCH = 16
MOE_TOKENS = 4096
MOE_GROUP = 5
GROUP_ROWS = MOE_GROUP * EXPERT_BLOCK


def _list_len(ts):
    return -(-ts // GROUP_ROWS) * GROUP_ROWS


def _plan_kernel(cw_ref, tok_ref, w_ref, cnt_ref):
    n_e, ts = cw_ref.shape
    cw = cw_ref[...]
    chosen = cw > 0.0
    cf = jnp.where(chosen, 1.0, 0.0).astype(BF16)
    r_i = lax.broadcasted_iota(jnp.int32, (LANES, LANES), 0)
    c_i = lax.broadcasted_iota(jnp.int32, (LANES, LANES), 1)
    before = jnp.where(r_i < c_i, 1.0, 0.0).astype(BF16)
    ones = jnp.ones((LANES, LANES), BF16)
    carry = jnp.zeros((n_e, LANES), F32)
    ranks = []
    for k in range(ts // LANES):
        ck = cf[:, k * LANES:(k + 1) * LANES]
        ranks.append(_dot(ck, before) + carry)
        carry = carry + _dot(ck, ones)
    rank = jnp.concatenate(ranks, axis=1).astype(jnp.int32)
    lane = lax.broadcasted_iota(jnp.int32, (n_e, ts), 1)
    packed = jnp.where(chosen, ((lane - rank) << TOK_BITS) | lane, -1)
    w = jnp.where(chosen, cw, 0.0)
    for bit in range(ts.bit_length() - 1):
        step = 1 << bit
        src_p = pltpu.roll(packed, ts - step, 1)
        src_w = pltpu.roll(w, ts - step, 1)
        take = (src_p >= 0) & (lane < ts - step) & (((src_p >> (TOK_BITS + bit)) & 1) == 1)
        keep = (packed >= 0) & (((packed >> (TOK_BITS + bit)) & 1) == 0)
        packed = jnp.where(take, src_p, jnp.where(keep, packed, -1))
        w = jnp.where(take, src_w, jnp.where(keep, w, 0.0))
    count = carry[:, 0:1].astype(jnp.int32)
    valid = packed >= 0
    tok = packed & ((1 << TOK_BITS) - 1)
    tok_last = jnp.max(jnp.where(valid, tok, 0).astype(F32), axis=1, keepdims=True).astype(jnp.int32)
    w_last = jnp.sum(jnp.where(lane == count - 1, w, 0.0), axis=1, keepdims=True)
    batch_end = (count + (RMW_BATCH - 1)) & (-RMW_BATCH)
    tok = jnp.where(valid, tok, tok_last)
    w = jnp.where(valid, w, jnp.where(lane < batch_end, w_last, 0.0))
    n_tail = tok_ref.shape[2] - ts
    if n_tail:
        tok = jnp.concatenate([tok, jnp.broadcast_to(tok_last, (n_e, n_tail))], axis=1)
        w = jnp.concatenate([w, jnp.zeros((n_e, n_tail), F32)], axis=1)
    tok_ref[0] = tok * ROW_TILE
    w_ref[0] = w
    cnt_ref[0] = jnp.broadcast_to(count, (n_e, LANES))


def _plan(cw_t, ts):
    n_e, t = cw_t.shape
    n_sup = t // ts
    n_list = _list_len(ts)
    assert ts <= (1 << TOK_BITS)
    return pl.pallas_call(
        _plan_kernel,
        grid=(n_sup,),
        in_specs=[pl.BlockSpec((n_e, ts), lambda s: (0, s))],
        out_specs=[pl.BlockSpec((1, n_e, n_list), lambda s: (s, 0, 0)),
                   pl.BlockSpec((1, n_e, n_list), lambda s: (s, 0, 0)),
                   pl.BlockSpec((1, n_e, LANES), lambda s: (s, 0, 0))],
        out_shape=[jax.ShapeDtypeStruct((n_sup, n_e, n_list), jnp.int32),
                   jax.ShapeDtypeStruct((n_sup, n_e, n_list), F32),
                   jax.ShapeDtypeStruct((n_sup, n_e, LANES), jnp.int32)],
        compiler_params=pltpu.CompilerParams(dimension_semantics=("arbitrary",),
                                             vmem_limit_bytes=VMEM_LIMIT),
        name="plan",
    )(cw_t)


XT_PITCH = GROUP_ROWS + 1


SHORT_ROWS = 544


def _entry(list_ref, first, r):
    if first is None:
        return list_ref[0, r]
    return list_ref[0, 0, first + r]


def _moe_gather(list_ref, first, h_ref, xt_ref, lo=0, hi=GROUP_ROWS):
    for r in range(lo, hi):
        start = pl.multiple_of(_entry(list_ref, first, r), ROW_TILE)
        xt_ref[pl.ds(r, ROW_TILE, stride=XT_PITCH), :] = h_ref[pl.ds(start, ROW_TILE), :]


def _moe_scatter_add(list_ref, first, yt_ref, acc_ref, lo=0, hi=GROUP_ROWS):
    for grp in range(lo // RMW_BATCH, hi // RMW_BATCH):
        rows = range(grp * RMW_BATCH, (grp + 1) * RMW_BATCH)
        starts = [pl.multiple_of(_entry(list_ref, first, r), ROW_TILE) for r in rows]
        olds = [acc_ref[pl.ds(st, ROW_TILE), :] for st in starts]
        news = [old + yt_ref[pl.ds(r, ROW_TILE, stride=XT_PITCH), :] for r, old in zip(rows, olds)]
        for st, new in zip(starts, news):
            acc_ref[pl.ds(st, ROW_TILE), :] = new


def _moe_experts(q, xt_ref, w_ref, wg_ref, wu_ref, wd_ref, yt_ref, rows=GROUP_ROWS):
    x = jnp.concatenate(
        [xt_ref[pl.ds(j * XT_PITCH, rows), :] for j in range(ROW_TILE)], axis=1).astype(BF16)
    act = _silu(_dot(x, wg_ref[0, 0].astype(BF16))) * _dot(x, wu_ref[0, 0].astype(BF16))
    y = _dot(act.astype(BF16), wd_ref[0, 0].astype(BF16))
    w_col = jnp.concatenate(
        [jnp.broadcast_to(w_ref[0, q * MOE_GROUP + k], (EXPERT_BLOCK, EXPERT_BLOCK)).T
         for k in range(MOE_GROUP)], axis=0)[0:rows]
    for j in range(ROW_TILE):
        yt_ref[pl.ds(j * XT_PITCH, rows), :] = y[:, j * LANES:(j + 1) * LANES] * w_col


def _groups(count):
    return jnp.maximum((count + (GROUP_ROWS - 1)) // GROUP_ROWS, 1)


LIST_BUFFERS = 4


def _list_copy(lists_hbm, row, buf_ref, sem):
    return pltpu.make_async_copy(lists_hbm.at[row, :, pl.ds(0, GROUP_ROWS)], buf_ref, sem)


def _moe_kernel(cnt_ref, tokp_ref, tok_ref, tokn_ref, lists_hbm, w_ref, h_ref, wg_ref, wu_ref, wd_ref, acc_ref,
                xt0_ref, xt1_ref, yt_ref, lb0_ref, lb1_ref, lb2_ref, lb3_ref, sems):
    s = pl.program_id(0)
    e = pl.program_id(1)
    last_e = pl.num_programs(1) - 1
    cnt = cnt_ref[s, e]
    cnt_prev = cnt_ref[s, jnp.maximum(e - 1, 0)]
    cnt_next = cnt_ref[s, jnp.minimum(e + 1, last_e)]
    n = _groups(cnt)
    n_prev = _groups(cnt_prev)
    prev_first = (n_prev - 1) * GROUP_ROWS
    short = cnt <= SHORT_ROWS
    bufs = (lb0_ref, lb1_ref, lb2_ref, lb3_ref)
    row0 = s * pl.num_programs(1)

    @pl.when(e == 0)
    def _():
        acc_ref[...] = jnp.zeros_like(acc_ref)
        yt_ref[...] = jnp.zeros_like(yt_ref)
        _moe_gather(tok_ref, 0, h_ref, xt0_ref)
        for b, j in ((0, 0), (LIST_BUFFERS - 1, 0), (1, 1)):
            copy = _list_copy(lists_hbm, row0 + j, bufs[b], sems.at[b])
            copy.start()
            copy.wait()

    def experts(q, xt_ref, rows=GROUP_ROWS):
        _moe_experts(q, xt_ref, w_ref, wg_ref, wu_ref, wd_ref, yt_ref, rows)

    single = (n == 1) & (n_prev == 1)

    def step(xt_cur, xt_next, list_next, list_prev):
        @pl.when(single & short)
        def _():
            _moe_gather(list_next, None, h_ref, xt_next, 0, SHORT_ROWS)
            _moe_scatter_add(list_prev, None, yt_ref, acc_ref, 0, SHORT_ROWS)
            experts(0, xt_cur, SHORT_ROWS)

        @pl.when(single & jnp.logical_not(short))
        def _():
            _moe_gather(list_next, None, h_ref, xt_next, 0, SHORT_ROWS)
            _moe_scatter_add(list_prev, None, yt_ref, acc_ref, 0, SHORT_ROWS)
            experts(0, xt_cur)

    def step_rare(xt_cur, xt_next):
        @pl.when(cnt_prev > SHORT_ROWS)
        def _():
            _moe_scatter_add(tokp_ref, prev_first, yt_ref, acc_ref, SHORT_ROWS, GROUP_ROWS)

        @pl.when(cnt_next > SHORT_ROWS)
        def _():
            _moe_gather(tokn_ref, 0, h_ref, xt_next, SHORT_ROWS, GROUP_ROWS)

        @pl.when(jnp.logical_not(single))
        def _():
            _moe_gather(tokn_ref, 0, h_ref, xt_next, 0, SHORT_ROWS)
            _moe_scatter_add(tokp_ref, prev_first, yt_ref, acc_ref, 0, SHORT_ROWS)
            _moe_gather(tok_ref, 0, h_ref, xt_cur, SHORT_ROWS, GROUP_ROWS)
            experts(0, xt_cur)

            def more(q, carry):
                _moe_scatter_add(tok_ref, (q - 1) * GROUP_ROWS, yt_ref, acc_ref)
                _moe_gather(tok_ref, q * GROUP_ROWS, h_ref, xt_cur)
                experts(q, xt_cur)
                return carry

            lax.fori_loop(1, n, more, 0)

    tiles = (xt0_ref, xt1_ref)
    for p in range(2):
        @pl.when(e % 2 == p)
        def _(p=p):
            step_rare(tiles[p], tiles[1 - p])

    for k in range(LIST_BUFFERS):
        nxt, inc, prv = (k + 1) % LIST_BUFFERS, (k + 2) % LIST_BUFFERS, (k + 3) % LIST_BUFFERS

        @pl.when(e % LIST_BUFFERS == k)
        def _(k=k, nxt=nxt, inc=inc, prv=prv):
            @pl.when((e >= 1) & (e + 1 <= last_e))
            def _():
                _list_copy(lists_hbm, row0 + e + 1, bufs[nxt], sems.at[nxt]).wait()

            @pl.when(e + 2 <= last_e)
            def _():
                _list_copy(lists_hbm, row0 + e + 2, bufs[inc], sems.at[inc]).start()

            step(tiles[k % 2], tiles[(k + 1) % 2], bufs[nxt], bufs[prv])

    @pl.when(e == last_e)
    def _():
        _moe_scatter_add(tok_ref, (n - 1) * GROUP_ROWS, yt_ref, acc_ref, 0, SHORT_ROWS)

    @pl.when((e == last_e) & (cnt > SHORT_ROWS))
    def _():
        _moe_scatter_add(tok_ref, (n - 1) * GROUP_ROWS, yt_ref, acc_ref, SHORT_ROWS, GROUP_ROWS)


def _moe(h2rows, tok, wts, counts, wg, wu, wd, layer, ts):
    rows = h2rows.shape[0]
    n_sup = rows // (ts * ROW_TILE)
    d, ff = wg.shape[2], wg.shape[3]
    n_list = tok.shape[2]
    assert d == ROW_TILE * LANES and n_list == _list_len(ts)

    assert N_EXPERTS % LIST_BUFFERS == 0

    def list_idx(shift):
        def idx(s, e, cnt):
            return (s * N_EXPERTS + jnp.clip(e + shift, 0, N_EXPERTS - 1), 0, 0)
        return idx

    def sup_idx(s, e, cnt):
        return (s, 0)

    def w_idx(s, e, cnt):
        return (layer, e, 0, 0)

    def list_spec(shift):
        return pl.BlockSpec((1, 1, n_list), list_idx(shift), memory_space=pltpu.SMEM)

    once = pl.Buffered(1)
    tile = pltpu.VMEM((ROW_TILE * XT_PITCH, LANES), F32)
    grid_spec = pltpu.PrefetchScalarGridSpec(
        num_scalar_prefetch=1,
        grid=(n_sup, N_EXPERTS),
        in_specs=[list_spec(-1), list_spec(0), list_spec(1),
                  pl.BlockSpec(memory_space=pl.ANY),
                  pl.BlockSpec((1, n_list // EXPERT_BLOCK, 1, EXPERT_BLOCK),
                               lambda s, e, cnt: (s * N_EXPERTS + e, 0, 0, 0)),
                  pl.BlockSpec((ts * ROW_TILE, LANES), sup_idx, pipeline_mode=once),
                  pl.BlockSpec((1, 1, d, ff), w_idx),
                  pl.BlockSpec((1, 1, d, ff), w_idx),
                  pl.BlockSpec((1, 1, ff, d), w_idx)],
        out_specs=pl.BlockSpec((ts * ROW_TILE, LANES), sup_idx, pipeline_mode=once),
        scratch_shapes=[tile, tile, tile] + [pltpu.SMEM((1, GROUP_ROWS), jnp.int32)] * LIST_BUFFERS
        + [pltpu.SemaphoreType.DMA((LIST_BUFFERS,))],
    )
    lists = tok.reshape(n_sup * N_EXPERTS, 1, n_list)
    return pl.pallas_call(
        _moe_kernel,
        grid_spec=grid_spec,
        out_shape=jax.ShapeDtypeStruct((rows, LANES), F32),
        compiler_params=pltpu.CompilerParams(dimension_semantics=("arbitrary", "arbitrary"),
                                             vmem_limit_bytes=VMEM_LIMIT),
        name="moe",
    )(counts, lists, lists, lists, lists,
      wts.reshape(n_sup * N_EXPERTS, n_list // EXPERT_BLOCK, 1, EXPERT_BLOCK), h2rows, wg, wu, wd)


def _routed_rows_to_tile(r_ref, tm):
    return jnp.concatenate([r_ref[pl.ds(j, tm, stride=ROW_TILE), :] for j in range(ROW_TILE)], axis=1)


def _final_kernel(base_ref, r_ref, gate_ref, fg_ref, o_ref):
    tm = base_ref.shape[0]
    x = base_ref[...] + gate_ref[0] * _routed_rows_to_tile(r_ref, tm)
    o_ref[...] = _rms(x) * fg_ref[...]


def _final(base, routed, gate, final_gain, seq):
    t, d = base.shape
    tm = 512
    tiles_per_seq = seq // tm
    return pl.pallas_call(
        _final_kernel,
        grid=(t // tm,),
        in_specs=[pl.BlockSpec((tm, d), lambda i: (i, 0)),
                  pl.BlockSpec((tm * ROW_TILE, LANES), lambda i: (i, 0)),
                  pl.BlockSpec((1, 1, d), lambda i: (i // tiles_per_seq, 0, 0)),
                  pl.BlockSpec((1, d), lambda i: (0, 0))],
        out_specs=pl.BlockSpec((tm, d), lambda i: (i, 0)),
        out_shape=jax.ShapeDtypeStruct((t, d), F32),
        compiler_params=pltpu.CompilerParams(dimension_semantics=("arbitrary",),
                                             vmem_limit_bytes=VMEM_LIMIT),
        name="final_norm",
    )(base, routed, gate, final_gain)


def _prep_layer(w_in, w_pool, pool_scale, w_spatial, b_spatial, g_q, w_uq, g_kv, w_ukv, w_out,
                w_router, router_bias, ws_gate, ws_up, ws_down):
    d = w_in.shape[0]
    o_kpe = POOL_WIDTH + 2 * SG_WIDTH + Q_LORA + KV_LORA
    x1 = w_in[:, o_kpe:o_kpe + HALF_ROPE]
    x2 = w_in[:, o_kpe + HALF_ROPE:o_kpe + QK_ROPE]
    zl = jnp.zeros((d, QK_NOPE), F32)
    zr = jnp.zeros((d, HEAD_PAD - QK_NOPE - QK_ROPE), F32)
    w_in_pad = jnp.concatenate([w_in[:, :o_kpe], zl, x1, x2, zr, zl, x2, x1, zr], axis=1)

    uq = w_uq.reshape(Q_LORA, MLA_HEADS, QK_NOPE + QK_ROPE)
    q1 = uq[..., QK_NOPE:QK_NOPE + HALF_ROPE]
    q2 = uq[..., QK_NOPE + HALF_ROPE:]
    zq = jnp.zeros((Q_LORA, MLA_HEADS, HEAD_PAD - QK_NOPE - QK_ROPE), F32)
    w_uq_pad = jnp.concatenate([uq, zq], axis=-1).reshape(Q_LORA, MLA_HEADS * HEAD_PAD)
    w_uq_sw = jnp.concatenate([jnp.zeros_like(uq[..., :QK_NOPE]), q2, q1, zq], axis=-1)
    w_uq_sw = w_uq_sw.reshape(Q_LORA, MLA_HEADS * HEAD_PAD)

    ukv = w_ukv.reshape(KV_LORA, MLA_HEADS, QK_NOPE + V_DIM)
    zk = jnp.zeros((KV_LORA, MLA_HEADS, HEAD_PAD - QK_NOPE), F32)
    w_k = jnp.concatenate([ukv[..., :QK_NOPE], zk], axis=-1).reshape(KV_LORA, MLA_HEADS * HEAD_PAD)
    w_v = ukv[..., QK_NOPE:].reshape(KV_LORA, MLA_HEADS * V_DIM)

    w_pool_bd = jax.scipy.linalg.block_diag(*[w_pool[g] for g in range(len(POOL_WINDOWS))])
    b_sp = jnp.repeat(b_spatial.T, SG_DIM, axis=1)
    return {
        "w_in": w_in_pad.astype(BF16), "w_pool": w_pool_bd.astype(BF16),
        "pool_scale": pool_scale.reshape(1, -1), "w_spatial": w_spatial, "b_spatial": b_sp,
        "g_q": g_q.reshape(1, -1), "w_uq": w_uq_pad.astype(BF16), "w_uq_sw": w_uq_sw.astype(BF16),
        "g_kv": g_kv.reshape(1, -1), "w_k": w_k.astype(BF16), "w_v": w_v.astype(BF16),
        "w_out": w_out.astype(BF16), "w_router_t": w_router.T, "router_bias": router_bias.reshape(-1, 1),
        "ws_gate": ws_gate.astype(BF16), "ws_up": ws_up.astype(BF16), "ws_down": ws_down.astype(BF16),
    }


def kernel(x, c, positions, w_ada, b_ada, w_in, w_pool, pool_scale, w_spatial, b_spatial, g_q, w_uq, g_kv, w_ukv, w_out, w_router, router_bias, w_gate, w_up, w_down, ws_gate, ws_up, ws_down, final_gain):
    batch, seq, d = x.shape
    depth = w_ada.shape[0]
    mod = _ada_mod(c, w_ada, b_ada)
    cos_t, sin_t = _rope_tables(positions)
    x_parts = (x.reshape(batch * seq, d),)
    fg = final_gain.reshape(1, d)
    wg_b, wu_b, wd_b = w_gate, w_up, w_down
    ts = min(MOE_TOKENS, batch * seq)
    for l in range(depth):
        p = _prep_layer(w_in[l], w_pool[l], pool_scale[l], w_spatial[l], b_spatial[l], g_q[l], w_uq[l],
                        g_kv[l], w_ukv[l], w_out[l], w_router[l], router_bias[l], ws_gate[l], ws_up[l],
                        ws_down[l])
        shift1, scale1, gate1, shift2, scale2, gate2 = [
            mod[l, :, k * d:(k + 1) * d].reshape(batch, 1, d) for k in range(6)]
        outs = _mixer_in(x_parts, shift1, scale1, cos_t, sin_t, p, seq)
        mix, q, k, v = outs[:4]
        xt = outs[4] if len(outs) == 5 else x_parts[0]
        att = _attention(q, k, v, batch, seq)
        base, h2rows, cw_t = _mixer_out(mix, att, xt, gate1, shift2, scale2, gate2, p, seq)
        tok, wts, cnt = _plan(cw_t, ts)
        routed = _moe(h2rows, tok, wts, cnt[:, :, 0], wg_b, wu_b, wd_b, l, ts)
        x_parts = (base, routed, gate2)
    return _final(*x_parts, fg, seq).reshape(batch, seq, d)
```

```python
import functools

import jax
import jax.numpy as jnp
from jax import lax
from jax.experimental import pallas as pl
from jax.experimental.pallas import tpu as pltpu

F32 = jnp.float32
BF16 = jnp.bfloat16

EPS = 1e-6
LANES = 128
POOL_WINDOWS = (2, 4, 8, 16)
POOL_WIDTH = 256
POOL_CH = 64
MAX_WINDOW = 16
SG_HEADS = 4
SG_WIDTH = 256
SG_DIM = 64
SG_CHUNK = 128
MLA_HEADS = 8
V_DIM = 64
QK_NOPE = 64
QK_ROPE = 32
HALF_ROPE = QK_ROPE // 2
Q_LORA = 256
KV_LORA = 128
ROPE_THETA = 10000.0
ATTN_SCALE = (QK_NOPE + QK_ROPE) ** -0.5
HEAD_PAD = 128
N_EXPERTS = 64
TOP_K = 8
N_EXPERT_GROUPS = 8
GROUP_SIZE = N_EXPERTS // N_EXPERT_GROUPS
TOPK_GROUPS = 4
ROUTED_SCALE = 2.5
VMEM_LIMIT = 52 * 1024 * 1024


def _dot(a, b):
    return jnp.dot(a, b, preferred_element_type=F32)


def _dot_nt(a, b):
    return lax.dot_general(a, b, (((1,), (1,)), ((), ())), preferred_element_type=F32)


def _rms(x):
    return x * lax.rsqrt(jnp.mean(x * x, axis=-1, keepdims=True) + EPS)


def _gelu_tanh(x):
    c = (2.0 / jnp.pi) ** 0.5
    return x * (0.5 * (1.0 + jnp.tanh(c * (x + 0.044715 * (x * x * x)))))


def _silu(x):
    return x * jax.nn.sigmoid(x)


def _ada_kernel(c_ref, w_ref, b_ref, o_ref):
    cond = _silu(c_ref[...])
    o_ref[0] = _dot(cond.astype(BF16), w_ref[0].astype(BF16)) + b_ref[0]


def _ada_mod(c, w_ada, b_ada):
    depth, d, n = w_ada.shape
    b = c.shape[0]
    nt = 1536
    return pl.pallas_call(
        _ada_kernel,
        grid=(depth, n // nt),
        in_specs=[pl.BlockSpec((b, d), lambda l, j: (0, 0)),
                  pl.BlockSpec((1, d, nt), lambda l, j: (l, 0, j)),
                  pl.BlockSpec((1, 1, nt), lambda l, j: (l, 0, j))],
        out_specs=pl.BlockSpec((1, b, nt), lambda l, j: (l, 0, j)),
        out_shape=jax.ShapeDtypeStruct((depth, b, n), F32),
        compiler_params=pltpu.CompilerParams(vmem_limit_bytes=VMEM_LIMIT),
        name="ada_mod",
    )(c, w_ada, b_ada.reshape(depth, 1, n))


def _rope_kernel(pos_ref, invf_ref, sign_ref, cos_ref, sin_ref):
    ang = pos_ref[...].astype(F32) * invf_ref[...]
    cos_ref[...] = jnp.cos(ang)
    sin_ref[...] = jnp.sin(ang) * sign_ref[...]


def _rope_tables(positions):
    t = positions.size
    tm = 2048
    inv_freq = ROPE_THETA ** (-jnp.arange(0, QK_ROPE, 2, dtype=F32) / QK_ROPE)
    invf = jnp.zeros((1, HEAD_PAD), F32)
    invf = invf.at[0, QK_NOPE:QK_NOPE + HALF_ROPE].set(inv_freq)
    invf = invf.at[0, QK_NOPE + HALF_ROPE:QK_NOPE + QK_ROPE].set(inv_freq)
    sign = jnp.zeros((1, HEAD_PAD), F32)
    sign = sign.at[0, QK_NOPE:QK_NOPE + HALF_ROPE].set(-1.0)
    sign = sign.at[0, QK_NOPE + HALF_ROPE:QK_NOPE + QK_ROPE].set(1.0)
    return pl.pallas_call(
        _rope_kernel,
        grid=(t // tm,),
        in_specs=[pl.BlockSpec((tm, 1), lambda i: (i, 0)),
                  pl.BlockSpec((1, HEAD_PAD), lambda i: (0, 0)),
                  pl.BlockSpec((1, HEAD_PAD), lambda i: (0, 0))],
        out_specs=[pl.BlockSpec((tm, HEAD_PAD), lambda i: (i, 0)),
                   pl.BlockSpec((tm, HEAD_PAD), lambda i: (i, 0))],
        out_shape=[jax.ShapeDtypeStruct((t, HEAD_PAD), F32)] * 2,
        name="rope_tables",
    )(positions.reshape(t, 1), invf, sign)


def _mixer_in_kernel(*refs, tiles_per_seq, fused):
    if fused:
        base_ref, r_ref, gprev_ref = refs[:3]
        (shift_ref, scale_ref, win_ref, wpool_ref, pscale_ref, wsp_ref, bsp_ref, gq_ref, wuq_ref, wuqs_ref,
         gkv_ref, wk_ref, wv_ref, cos_ref, sin_ref, mix_ref, q_ref, k_ref, v_ref, x_out_ref, carry_ref) = refs[3:]
        tm = base_ref.shape[0]
        x = base_ref[...] + gprev_ref[0] * jnp.concatenate(
            [r_ref[pl.ds(j, tm, stride=ROW_TILE), :] for j in range(ROW_TILE)], axis=1)
        x_out_ref[...] = x
    else:
        (x_ref, shift_ref, scale_ref, win_ref, wpool_ref, pscale_ref, wsp_ref, bsp_ref, gq_ref, wuq_ref, wuqs_ref,
         gkv_ref, wk_ref, wv_ref, cos_ref, sin_ref, mix_ref, q_ref, k_ref, v_ref, carry_ref) = refs
        tm = x_ref.shape[0]
        x = x_ref[...]
    ti = pl.program_id(0) % tiles_per_seq
    h = _rms(x) * (1.0 + scale_ref[0]) + shift_ref[0]
    z = _dot(h.astype(BF16), win_ref[...])

    a = z[:, 0:POOL_WIDTH]

    @pl.when(ti == 0)
    def _():
        carry_ref[...] = jnp.zeros_like(carry_ref)

    ext = jnp.concatenate([carry_ref[...], a], axis=0)
    carry_ref[...] = a[tm - MAX_WINDOW:, :]
    p1 = ext + pltpu.roll(ext, 1, 0)
    p2 = p1 + pltpu.roll(p1, 2, 0)
    p3 = p2 + pltpu.roll(p2, 4, 0)
    p4 = p3 + pltpu.roll(p3, 8, 0)
    lane = lax.broadcasted_iota(jnp.int32, (tm, POOL_WIDTH), 1)
    row = lax.broadcasted_iota(jnp.int32, (tm, POOL_WIDTH), 0) + (ti * tm + 1)
    g0, g1, g2 = lane < POOL_CH, lane < 2 * POOL_CH, lane < 3 * POOL_CH
    wsum = jnp.where(g0, p1[MAX_WINDOW:], jnp.where(g1, p2[MAX_WINDOW:],
                     jnp.where(g2, p3[MAX_WINDOW:], p4[MAX_WINDOW:])))
    width = jnp.where(g0, POOL_WINDOWS[0], jnp.where(g1, POOL_WINDOWS[1],
                      jnp.where(g2, POOL_WINDOWS[2], POOL_WINDOWS[3])))
    cnt = jnp.minimum(row, width).astype(F32)
    dlt = wsum / cnt - a
    y_pool = _dot(dlt.astype(BF16), wpool_ref[...]) * pscale_ref[...]
    mix_ref[:, 0:POOL_WIDTH] = y_pool.astype(mix_ref.dtype)

    ug = _gelu_tanh(z[:, POOL_WIDTH:POOL_WIDTH + SG_WIDTH])
    vg = _gelu_tanh(z[:, POOL_WIDTH + SG_WIDTH:POOL_WIDTH + 2 * SG_WIDTH])
    mu = jnp.mean(vg, axis=-1, keepdims=True)
    vc = vg - mu
    var = jnp.mean(vc * vc, axis=-1, keepdims=True)
    vn = (vc * lax.rsqrt(var + EPS)).astype(BF16)
    r_i = lax.broadcasted_iota(jnp.int32, (SG_CHUNK, SG_CHUNK), 0)
    c_i = lax.broadcasted_iota(jnp.int32, (SG_CHUNK, SG_CHUNK), 1)
    wms = [jnp.where(c_i <= r_i, wsp_ref[hh], 0.0).astype(BF16) for hh in range(SG_HEADS)]
    lane_head = lax.broadcasted_iota(jnp.int32, (SG_CHUNK, SG_WIDTH), 1) // SG_DIM
    for cidx in range(tm // SG_CHUNK):
        rows = slice(cidx * SG_CHUNK, (cidx + 1) * SG_CHUNK)
        vchunk = vn[rows]
        mixed = bsp_ref[...]
        for hh in range(SG_HEADS):
            mixed = mixed + jnp.where(lane_head == hh, _dot(wms[hh], vchunk), 0.0)
        mix_ref[rows, POOL_WIDTH:POOL_WIDTH + SG_WIDTH] = (ug[rows] * mixed).astype(mix_ref.dtype)

    o_cq = POOL_WIDTH + 2 * SG_WIDTH
    o_ckv = o_cq + Q_LORA
    o_kpe = o_ckv + KV_LORA
    cos = cos_ref[...]
    sin = sin_ref[...]
    cqn = (_rms(z[:, o_cq:o_ckv]) * gq_ref[...]).astype(BF16)
    q = _dot(cqn, wuq_ref[...])
    qs = _dot(cqn, wuqs_ref[...])
    ckvn = (_rms(z[:, o_ckv:o_kpe]) * gkv_ref[...]).astype(BF16)
    kn = _dot(ckvn, wk_ref[...])
    kpe = z[:, o_kpe:o_kpe + HEAD_PAD] * cos + z[:, o_kpe + HEAD_PAD:o_kpe + 2 * HEAD_PAD] * sin
    for hh in range(MLA_HEADS):
        blk = slice(hh * HEAD_PAD, (hh + 1) * HEAD_PAD)
        q_ref[:, blk] = ((q[:, blk] * cos + qs[:, blk] * sin) * ATTN_SCALE).astype(q_ref.dtype)
        k_ref[:, blk] = (kn[:, blk] + kpe).astype(k_ref.dtype)
    v_ref[...] = _dot(ckvn, wv_ref[...]).astype(v_ref.dtype)


def _mixer_in(x_parts, shift, scale, cos_t, sin_t, p, seq):
    fused = len(x_parts) == 3
    t, d = x_parts[0].shape
    tm = 512
    tiles_per_seq = seq // tm
    nz = p["w_in"].shape[1]
    hq = MLA_HEADS * HEAD_PAD

    def full(shape):
        return pl.BlockSpec(shape, lambda i: (0,) * len(shape))

    def mod():
        return pl.BlockSpec((1, 1, d), lambda i: (i // tiles_per_seq, 0, 0))

    x_specs = [pl.BlockSpec((tm, d), lambda i: (i, 0))]
    extra_out_specs, extra_out_shapes = [], []
    if fused:
        x_specs += [pl.BlockSpec((tm * ROW_TILE, LANES), lambda i: (i, 0)), mod()]
        extra_out_specs = [pl.BlockSpec((tm, d), lambda i: (i, 0))]
        extra_out_shapes = [jax.ShapeDtypeStruct((t, d), F32)]
    return pl.pallas_call(
        functools.partial(_mixer_in_kernel, tiles_per_seq=tiles_per_seq, fused=fused),
        grid=(t // tm,),
        in_specs=x_specs + [mod(), mod(),
                  full((d, nz)), full((POOL_WIDTH, POOL_WIDTH)), full((1, POOL_WIDTH)),
                  full((SG_HEADS, SG_CHUNK, SG_CHUNK)), full((SG_CHUNK, SG_WIDTH)),
                  full((1, Q_LORA)), full((Q_LORA, hq)), full((Q_LORA, hq)),
                  full((1, KV_LORA)), full((KV_LORA, hq)), full((KV_LORA, MLA_HEADS * V_DIM)),
                  pl.BlockSpec((tm, HEAD_PAD), lambda i: (i, 0)),
                  pl.BlockSpec((tm, HEAD_PAD), lambda i: (i, 0))],
        out_specs=[pl.BlockSpec((tm, POOL_WIDTH + SG_WIDTH), lambda i: (i, 0)),
                   pl.BlockSpec((tm, hq), lambda i: (i, 0)),
                   pl.BlockSpec((tm, hq), lambda i: (i, 0)),
                   pl.BlockSpec((tm, MLA_HEADS * V_DIM), lambda i: (i, 0))] + extra_out_specs,
        out_shape=[jax.ShapeDtypeStruct((t, POOL_WIDTH + SG_WIDTH), BF16),
                   jax.ShapeDtypeStruct((t, hq), BF16),
                   jax.ShapeDtypeStruct((t, hq), BF16),
                   jax.ShapeDtypeStruct((t, MLA_HEADS * V_DIM), BF16)] + extra_out_shapes,
        scratch_shapes=[pltpu.VMEM((MAX_WINDOW, POOL_WIDTH), F32)],
        compiler_params=pltpu.CompilerParams(dimension_semantics=("arbitrary",),
                                             vmem_limit_bytes=VMEM_LIMIT),
        name="mixer_in",
    )(*x_parts, shift, scale, p["w_in"], p["w_pool"], p["pool_scale"], p["w_spatial"], p["b_spatial"],
      p["g_q"], p["w_uq"], p["w_uq_sw"], p["g_kv"], p["w_k"], p["w_v"], cos_t, sin_t)


ATTN_TQ = 256
ATTN_LOOKAHEAD = 2


def _attn_kernel(q_ref, k_ref, v_ref, o_ref):
    seq = q_ref.shape[0]
    tq = ATTN_TQ
    r_i = lax.broadcasted_iota(jnp.int32, (tq, tq), 0)
    c_i = lax.broadcasted_iota(jnp.int32, (tq, tq), 1)
    lane = lax.broadcasted_iota(jnp.int32, (tq, 2 * V_DIM), 1)

    def scores(i, hh):
        lo, hi = i * tq, (i + 1) * tq
        blk = slice(hh * HEAD_PAD, (hh + 1) * HEAD_PAD)
        q = q_ref[lo:hi, blk]
        s_d = jnp.where(c_i <= r_i, _dot_nt(q, k_ref[lo:hi, blk]), -jnp.inf)
        s_o = _dot_nt(q, k_ref[0:lo, blk]) if i > 0 else None
        return s_d, s_o

    def finish(i, s_d, s_o):
        lo, hi = i * tq, (i + 1) * tq
        m = jnp.max(s_d, axis=-1, keepdims=True)
        if s_o is not None:
            m = jnp.maximum(m, jnp.max(s_o, axis=-1, keepdims=True))
            p_o = jnp.exp(s_o - m)
        p_d = jnp.exp(s_d - m)
        l = jnp.sum(p_d, axis=-1, keepdims=True)
        acc = _dot(p_d.astype(BF16), v_ref[lo:hi, :])
        if s_o is not None:
            l = l + jnp.sum(p_o, axis=-1, keepdims=True)
            acc = acc + _dot(p_o.astype(BF16), v_ref[0:lo, :])
        return acc * (1.0 / l)

    chains = [(i, hh) for i in range(seq // tq) for hh in range(2)]
    pending = [scores(*chains[c]) for c in range(ATTN_LOOKAHEAD)]
    outs = []
    for c, (i, hh) in enumerate(chains):
        if c + ATTN_LOOKAHEAD < len(chains):
            pending.append(scores(*chains[c + ATTN_LOOKAHEAD]))
        outs.append(finish(i, *pending.pop(0)))
        if hh == 1:
            o_ref[i * tq:(i + 1) * tq, :] = jnp.where(lane < V_DIM, outs[-2], outs[-1]).astype(o_ref.dtype)


def _attention(q, k, v, batch, seq):
    t = q.shape[0]
    return pl.pallas_call(
        _attn_kernel,
        grid=(batch, MLA_HEADS // 2),
        in_specs=[pl.BlockSpec((seq, 2 * HEAD_PAD), lambda b, hp: (b, hp)),
                  pl.BlockSpec((seq, 2 * HEAD_PAD), lambda b, hp: (b, hp)),
                  pl.BlockSpec((seq, 2 * V_DIM), lambda b, hp: (b, hp))],
        out_specs=pl.BlockSpec((seq, 2 * V_DIM), lambda b, hp: (b, hp)),
        out_shape=jax.ShapeDtypeStruct((t, MLA_HEADS * V_DIM), BF16),
        compiler_params=pltpu.CompilerParams(
            dimension_semantics=("arbitrary", "arbitrary"), vmem_limit_bytes=VMEM_LIMIT),
        name="attention",
    )(q, k, v)


def _first_max_index(cur, idx, sentinel):
    m = jnp.max(cur, axis=0, keepdims=True)
    first = jnp.min(jnp.where(cur == m, idx, sentinel), axis=0, keepdims=True)
    return m, first


def _route(logits_t, bias_t):
    n_tok = logits_t.shape[1]
    scores = jax.nn.sigmoid(logits_t)
    sel = scores + bias_t
    neg = -jnp.inf
    sub = lax.broadcasted_iota(jnp.int32, (GROUP_SIZE, n_tok), 0).astype(F32)
    gid = lax.broadcasted_iota(jnp.int32, (N_EXPERT_GROUPS, n_tok), 0).astype(F32)
    gscore = jnp.zeros((N_EXPERT_GROUPS, n_tok), F32)
    for g in range(N_EXPERT_GROUPS):
        s = sel[g * GROUP_SIZE:(g + 1) * GROUP_SIZE]
        m1, i1 = _first_max_index(s, sub, float(GROUP_SIZE))
        m2 = jnp.max(jnp.where(sub == i1, neg, s), axis=0, keepdims=True)
        gscore = jnp.where(gid == float(g), m1 + m2, gscore)
    eid = lax.broadcasted_iota(jnp.int32, (N_EXPERTS, n_tok), 0).astype(F32)
    egroup = jnp.floor(eid * (1.0 / GROUP_SIZE))
    allowed = jnp.zeros((N_EXPERTS, n_tok), F32)
    cur = gscore
    for _ in range(TOPK_GROUPS):
        _, gi = _first_max_index(cur, gid, float(N_EXPERT_GROUPS))
        cur = jnp.where(gid == gi, neg, cur)
        allowed = jnp.where(egroup == gi, 1.0, allowed)
    cur = jnp.where(allowed > 0.0, sel, neg)
    chosen = jnp.zeros((N_EXPERTS, n_tok), F32)
    for _ in range(TOP_K):
        _, ei = _first_max_index(cur, eid, float(N_EXPERTS))
        hit = eid == ei
        cur = jnp.where(hit, neg, cur)
        chosen = jnp.where(hit, 1.0, chosen)
    w = jnp.where(chosen > 0.0, scores, 0.0)
    return w / jnp.sum(w, axis=0, keepdims=True) * ROUTED_SCALE


def _split_bf16(x):
    hi = x.astype(BF16)
    lo = (x - hi.astype(F32)).astype(BF16)
    return hi, lo


ROW_TILE = 8


def _mixer_out_kernel(mix_ref, att_ref, x_ref, gate1_ref, shift_ref, scale_ref, gate2_ref, wo_ref, wr_ref,
                      rb_ref, sg_ref, su_ref, sd_ref, base_ref, h2r_ref, cw_ref):
    tm = x_ref.shape[0]
    half = mix_ref.shape[1]
    y = _dot(mix_ref[...], wo_ref[0:half, :]) + _dot(att_ref[...], wo_ref[half:, :])
    x2 = x_ref[...] + gate1_ref[0] * y
    h2 = _rms(x2) * (1.0 + scale_ref[0]) + shift_ref[0]
    for j in range(ROW_TILE):
        h2r_ref[pl.ds(j, tm, stride=ROW_TILE), :] = h2[:, j * LANES:(j + 1) * LANES]
    hb = h2.astype(BF16)
    act = _silu(_dot(hb, sg_ref[...])) * _dot(hb, su_ref[...])
    base_ref[...] = x2 + gate2_ref[0] * _dot(act.astype(BF16), sd_ref[...])
    h_hi, h_lo = _split_bf16(h2)
    w_hi, w_lo = _split_bf16(wr_ref[...])
    logits_t = _dot_nt(w_hi, h_hi) + (_dot_nt(w_hi, h_lo) + _dot_nt(w_lo, h_hi))
    cw_ref[...] = _route(logits_t, rb_ref[...])


def _mixer_out(mix, att, x2d, gate1, shift, scale, gate2, p, seq):
    t, d = x2d.shape
    tm = 512
    tiles_per_seq = seq // tm
    ff = p["ws_gate"].shape[1]

    def full(shape):
        return pl.BlockSpec(shape, lambda i: (0,) * len(shape))

    def mod():
        return pl.BlockSpec((1, 1, d), lambda i: (i // tiles_per_seq, 0, 0))

    return pl.pallas_call(
        _mixer_out_kernel,
        grid=(t // tm,),
        in_specs=[pl.BlockSpec((tm, mix.shape[1]), lambda i: (i, 0)),
                  pl.BlockSpec((tm, att.shape[1]), lambda i: (i, 0)),
                  pl.BlockSpec((tm, d), lambda i: (i, 0)), mod(), mod(), mod(), mod(),
                  full((d, d)), full((N_EXPERTS, d)), full((N_EXPERTS, 1)),
                  full((d, ff)), full((d, ff)), full((ff, d))],
        out_specs=[pl.BlockSpec((tm, d), lambda i: (i, 0)),
                   pl.BlockSpec((tm * ROW_TILE, LANES), lambda i: (i, 0)),
                   pl.BlockSpec((N_EXPERTS, tm), lambda i: (0, i))],
        out_shape=[jax.ShapeDtypeStruct((t, d), F32),
                   jax.ShapeDtypeStruct((t * ROW_TILE, LANES), F32),
                   jax.ShapeDtypeStruct((N_EXPERTS, t), F32)],
        compiler_params=pltpu.CompilerParams(dimension_semantics=("arbitrary",),
                                             vmem_limit_bytes=VMEM_LIMIT),
        name="mixer_out",
    )(mix, att, x2d, gate1, shift, scale, gate2, p["w_out"], p["w_router_t"], p["router_bias"],
      p["ws_gate"], p["ws_up"], p["ws_down"])


EXPERT_BLOCK = 128
TOK_BITS = 12
RMW_BATCH = 8
MOE_TOKENS = 4096
MOE_GROUP = 5
GROUP_ROWS = MOE_GROUP * EXPERT_BLOCK


def _list_len(ts):
    return -(-ts // GROUP_ROWS) * GROUP_ROWS


def _plan_kernel(cw_ref, tok_ref, w_ref, cnt_ref):
    n_e, ts = cw_ref.shape
    cw = cw_ref[...]
    chosen = cw > 0.0
    cf = jnp.where(chosen, 1.0, 0.0).astype(BF16)
    r_i = lax.broadcasted_iota(jnp.int32, (LANES, LANES), 0)
    c_i = lax.broadcasted_iota(jnp.int32, (LANES, LANES), 1)
    before = jnp.where(r_i < c_i, 1.0, 0.0).astype(BF16)
    ones = jnp.ones((LANES, LANES), BF16)
    carry = jnp.zeros((n_e, LANES), F32)
    ranks = []
    for k in range(ts // LANES):
        ck = cf[:, k * LANES:(k + 1) * LANES]
        ranks.append(_dot(ck, before) + carry)
        carry = carry + _dot(ck, ones)
    rank = jnp.concatenate(ranks, axis=1).astype(jnp.int32)
    lane = lax.broadcasted_iota(jnp.int32, (n_e, ts), 1)
    packed = jnp.where(chosen, ((lane - rank) << TOK_BITS) | lane, -1)
    w = jnp.where(chosen, cw, 0.0)
    for bit in range(ts.bit_length() - 1):
        step = 1 << bit
        src_p = pltpu.roll(packed, ts - step, 1)
        src_w = pltpu.roll(w, ts - step, 1)
        take = (src_p >= 0) & (lane < ts - step) & (((src_p >> (TOK_BITS + bit)) & 1) == 1)
        keep = (packed >= 0) & (((packed >> (TOK_BITS + bit)) & 1) == 0)
        packed = jnp.where(take, src_p, jnp.where(keep, packed, -1))
        w = jnp.where(take, src_w, jnp.where(keep, w, 0.0))
    count = carry[:, 0:1].astype(jnp.int32)
    valid = packed >= 0
    tok = packed & ((1 << TOK_BITS) - 1)
    tok_last = jnp.max(jnp.where(valid, tok, 0).astype(F32), axis=1, keepdims=True).astype(jnp.int32)
    w_last = jnp.sum(jnp.where(lane == count - 1, w, 0.0), axis=1, keepdims=True)
    batch_end = (count + (RMW_BATCH - 1)) & (-RMW_BATCH)
    tok = jnp.where(valid, tok, tok_last)
    w = jnp.where(valid, w, jnp.where(lane < batch_end, w_last, 0.0))
    n_tail = tok_ref.shape[2] - ts
    if n_tail:
        tok = jnp.concatenate([tok, jnp.broadcast_to(tok_last, (n_e, n_tail))], axis=1)
        w = jnp.concatenate([w, jnp.zeros((n_e, n_tail), F32)], axis=1)
    tok_ref[0] = tok * ROW_TILE
    w_ref[0] = w
    cnt_ref[0] = jnp.broadcast_to(count, (n_e, LANES))


def _plan(cw_t, ts):
    n_e, t = cw_t.shape
    n_sup = t // ts
    n_list = _list_len(ts)
    assert ts <= (1 << TOK_BITS)
    return pl.pallas_call(
        _plan_kernel,
        grid=(n_sup,),
        in_specs=[pl.BlockSpec((n_e, ts), lambda s: (0, s))],
        out_specs=[pl.BlockSpec((1, n_e, n_list), lambda s: (s, 0, 0)),
                   pl.BlockSpec((1, n_e, n_list), lambda s: (s, 0, 0)),
                   pl.BlockSpec((1, n_e, LANES), lambda s: (s, 0, 0))],
        out_shape=[jax.ShapeDtypeStruct((n_sup, n_e, n_list), jnp.int32),
                   jax.ShapeDtypeStruct((n_sup, n_e, n_list), F32),
                   jax.ShapeDtypeStruct((n_sup, n_e, LANES), jnp.int32)],
        compiler_params=pltpu.CompilerParams(dimension_semantics=("arbitrary",),
                                             vmem_limit_bytes=VMEM_LIMIT),
        name="plan",
    )(cw_t)


XT_PITCH = GROUP_ROWS + 1


SHORT_ROWS = 544


def _moe_gather(list_ref, first, h_ref, xt_ref, lo=0, hi=GROUP_ROWS):
    for r in range(lo, hi):
        start = pl.multiple_of(list_ref[0, 0, first + r], ROW_TILE)
        xt_ref[pl.ds(r, ROW_TILE, stride=XT_PITCH), :] = h_ref[pl.ds(start, ROW_TILE), :]


def _moe_scatter_add(list_ref, first, yt_ref, acc_ref, lo=0, hi=GROUP_ROWS):
    for grp in range(lo // RMW_BATCH, hi // RMW_BATCH):
        rows = range(grp * RMW_BATCH, (grp + 1) * RMW_BATCH)
        starts = [pl.multiple_of(list_ref[0, 0, first + r], ROW_TILE) for r in rows]
        olds = [acc_ref[pl.ds(st, ROW_TILE), :] for st in starts]
        news = [old + yt_ref[pl.ds(r, ROW_TILE, stride=XT_PITCH), :] for r, old in zip(rows, olds)]
        for st, new in zip(starts, news):
            acc_ref[pl.ds(st, ROW_TILE), :] = new


def _moe_experts(q, xt_ref, w_ref, wg_ref, wu_ref, wd_ref, yt_ref, rows=GROUP_ROWS):
    x = jnp.concatenate(
        [xt_ref[pl.ds(j * XT_PITCH, rows), :] for j in range(ROW_TILE)], axis=1).astype(BF16)
    act = _silu(_dot(x, wg_ref[0, 0].astype(BF16))) * _dot(x, wu_ref[0, 0].astype(BF16))
    y = _dot(act.astype(BF16), wd_ref[0, 0].astype(BF16))
    w_col = jnp.concatenate(
        [jnp.broadcast_to(w_ref[0, q * MOE_GROUP + k], (EXPERT_BLOCK, EXPERT_BLOCK)).T
         for k in range(MOE_GROUP)], axis=0)[0:rows]
    for j in range(ROW_TILE):
        yt_ref[pl.ds(j * XT_PITCH, rows), :] = y[:, j * LANES:(j + 1) * LANES] * w_col


def _groups(count):
    return jnp.maximum((count + (GROUP_ROWS - 1)) // GROUP_ROWS, 1)


def _moe_kernel(cnt_ref, tokp_ref, tok_ref, tokn_ref, w_ref, h_ref, wg_ref, wu_ref, wd_ref, acc_ref,
                xt0_ref, xt1_ref, yt_ref):
    s = pl.program_id(0)
    e = pl.program_id(1)
    last_e = pl.num_programs(1) - 1
    cnt = cnt_ref[s, e]
    cnt_prev = cnt_ref[s, jnp.maximum(e - 1, 0)]
    cnt_next = cnt_ref[s, jnp.minimum(e + 1, last_e)]
    n = _groups(cnt)
    n_prev = _groups(cnt_prev)
    prev_first = (n_prev - 1) * GROUP_ROWS
    short = cnt <= SHORT_ROWS

    @pl.when(e == 0)
    def _():
        acc_ref[...] = jnp.zeros_like(acc_ref)
        yt_ref[...] = jnp.zeros_like(yt_ref)
        _moe_gather(tok_ref, 0, h_ref, xt0_ref)

    def experts(q, xt_ref, rows=GROUP_ROWS):
        _moe_experts(q, xt_ref, w_ref, wg_ref, wu_ref, wd_ref, yt_ref, rows)

    def step(xt_cur, xt_next):
        @pl.when(cnt_prev > SHORT_ROWS)
        def _():
            _moe_scatter_add(tokp_ref, prev_first, yt_ref, acc_ref, SHORT_ROWS, GROUP_ROWS)

        @pl.when(cnt_next > SHORT_ROWS)
        def _():
            _moe_gather(tokn_ref, 0, h_ref, xt_next, SHORT_ROWS, GROUP_ROWS)

        single = (n == 1) & (n_prev == 1)

        @pl.when(single & short)
        def _():
            _moe_gather(tokn_ref, 0, h_ref, xt_next, 0, SHORT_ROWS)
            _moe_scatter_add(tokp_ref, 0, yt_ref, acc_ref, 0, SHORT_ROWS)
            experts(0, xt_cur, SHORT_ROWS)

        @pl.when(single & jnp.logical_not(short))
        def _():
            _moe_gather(tokn_ref, 0, h_ref, xt_next, 0, SHORT_ROWS)
            _moe_scatter_add(tokp_ref, 0, yt_ref, acc_ref, 0, SHORT_ROWS)
            experts(0, xt_cur)

        @pl.when(jnp.logical_not(single))
        def _():
            _moe_gather(tokn_ref, 0, h_ref, xt_next, 0, SHORT_ROWS)
            _moe_scatter_add(tokp_ref, prev_first, yt_ref, acc_ref, 0, SHORT_ROWS)
            _moe_gather(tok_ref, 0, h_ref, xt_cur, SHORT_ROWS, GROUP_ROWS)
            experts(0, xt_cur)

            def more(q, carry):
                _moe_scatter_add(tok_ref, (q - 1) * GROUP_ROWS, yt_ref, acc_ref)
                _moe_gather(tok_ref, q * GROUP_ROWS, h_ref, xt_cur)
                experts(q, xt_cur)
                return carry

            lax.fori_loop(1, n, more, 0)

    @pl.when(e % 2 == 0)
    def _():
        step(xt0_ref, xt1_ref)

    @pl.when(e % 2 == 1)
    def _():
        step(xt1_ref, xt0_ref)

    @pl.when(e == last_e)
    def _():
        _moe_scatter_add(tok_ref, (n - 1) * GROUP_ROWS, yt_ref, acc_ref, 0, SHORT_ROWS)

    @pl.when((e == last_e) & (cnt > SHORT_ROWS))
    def _():
        _moe_scatter_add(tok_ref, (n - 1) * GROUP_ROWS, yt_ref, acc_ref, SHORT_ROWS, GROUP_ROWS)


def _moe(h2rows, tok, wts, counts, wg, wu, wd, layer, ts):
    rows = h2rows.shape[0]
    n_sup = rows // (ts * ROW_TILE)
    d, ff = wg.shape[2], wg.shape[3]
    n_list = tok.shape[2]
    assert d == ROW_TILE * LANES and n_list == _list_len(ts)

    assert N_EXPERTS % 2 == 0

    def list_idx(shift):
        def idx(s, e, cnt):
            return (s * N_EXPERTS + jnp.clip(e + shift, 0, N_EXPERTS - 1), 0, 0)
        return idx

    def sup_idx(s, e, cnt):
        return (s, 0)

    def w_idx(s, e, cnt):
        return (layer, e, 0, 0)

    def list_spec(shift):
        return pl.BlockSpec((1, 1, n_list), list_idx(shift), memory_space=pltpu.SMEM)

    once = pl.Buffered(1)
    tile = pltpu.VMEM((ROW_TILE * XT_PITCH, LANES), F32)
    grid_spec = pltpu.PrefetchScalarGridSpec(
        num_scalar_prefetch=1,
        grid=(n_sup, N_EXPERTS),
        in_specs=[list_spec(-1), list_spec(0), list_spec(1),
                  pl.BlockSpec((1, n_list // EXPERT_BLOCK, 1, EXPERT_BLOCK),
                               lambda s, e, cnt: (s * N_EXPERTS + e, 0, 0, 0)),
                  pl.BlockSpec((ts * ROW_TILE, LANES), sup_idx, pipeline_mode=once),
                  pl.BlockSpec((1, 1, d, ff), w_idx),
                  pl.BlockSpec((1, 1, d, ff), w_idx),
                  pl.BlockSpec((1, 1, ff, d), w_idx)],
        out_specs=pl.BlockSpec((ts * ROW_TILE, LANES), sup_idx, pipeline_mode=once),
        scratch_shapes=[tile, tile, tile],
    )
    lists = tok.reshape(n_sup * N_EXPERTS, 1, n_list)
    return pl.pallas_call(
        _moe_kernel,
        grid_spec=grid_spec,
        out_shape=jax.ShapeDtypeStruct((rows, LANES), F32),
        compiler_params=pltpu.CompilerParams(dimension_semantics=("arbitrary", "arbitrary"),
                                             vmem_limit_bytes=VMEM_LIMIT),
        name="moe",
    )(counts, lists, lists, lists,
      wts.reshape(n_sup * N_EXPERTS, n_list // EXPERT_BLOCK, 1, EXPERT_BLOCK), h2rows, wg, wu, wd)


def _routed_rows_to_tile(r_ref, tm):
    return jnp.concatenate([r_ref[pl.ds(j, tm, stride=ROW_TILE), :] for j in range(ROW_TILE)], axis=1)


def _final_kernel(base_ref, r_ref, gate_ref, fg_ref, o_ref):
    tm = base_ref.shape[0]
    x = base_ref[...] + gate_ref[0] * _routed_rows_to_tile(r_ref, tm)
    o_ref[...] = _rms(x) * fg_ref[...]


def _final(base, routed, gate, final_gain, seq):
    t, d = base.shape
    tm = 512
    tiles_per_seq = seq // tm
    return pl.pallas_call(
        _final_kernel,
        grid=(t // tm,),
        in_specs=[pl.BlockSpec((tm, d), lambda i: (i, 0)),
                  pl.BlockSpec((tm * ROW_TILE, LANES), lambda i: (i, 0)),
                  pl.BlockSpec((1, 1, d), lambda i: (i // tiles_per_seq, 0, 0)),
                  pl.BlockSpec((1, d), lambda i: (0, 0))],
        out_specs=pl.BlockSpec((tm, d), lambda i: (i, 0)),
        out_shape=jax.ShapeDtypeStruct((t, d), F32),
        compiler_params=pltpu.CompilerParams(dimension_semantics=("arbitrary",),
                                             vmem_limit_bytes=VMEM_LIMIT),
        name="final_norm",
    )(base, routed, gate, final_gain)


def _prep_layer(w_in, w_pool, pool_scale, w_spatial, b_spatial, g_q, w_uq, g_kv, w_ukv, w_out,
                w_router, router_bias, ws_gate, ws_up, ws_down):
    d = w_in.shape[0]
    o_kpe = POOL_WIDTH + 2 * SG_WIDTH + Q_LORA + KV_LORA
    x1 = w_in[:, o_kpe:o_kpe + HALF_ROPE]
    x2 = w_in[:, o_kpe + HALF_ROPE:o_kpe + QK_ROPE]
    zl = jnp.zeros((d, QK_NOPE), F32)
    zr = jnp.zeros((d, HEAD_PAD - QK_NOPE - QK_ROPE), F32)
    w_in_pad = jnp.concatenate([w_in[:, :o_kpe], zl, x1, x2, zr, zl, x2, x1, zr], axis=1)

    uq = w_uq.reshape(Q_LORA, MLA_HEADS, QK_NOPE + QK_ROPE)
    q1 = uq[..., QK_NOPE:QK_NOPE + HALF_ROPE]
    q2 = uq[..., QK_NOPE + HALF_ROPE:]
    zq = jnp.zeros((Q_LORA, MLA_HEADS, HEAD_PAD - QK_NOPE - QK_ROPE), F32)
    w_uq_pad = jnp.concatenate([uq, zq], axis=-1).reshape(Q_LORA, MLA_HEADS * HEAD_PAD)
    w_uq_sw = jnp.concatenate([jnp.zeros_like(uq[..., :QK_NOPE]), q2, q1, zq], axis=-1)
    w_uq_sw = w_uq_sw.reshape(Q_LORA, MLA_HEADS * HEAD_PAD)

    ukv = w_ukv.reshape(KV_LORA, MLA_HEADS, QK_NOPE + V_DIM)
    zk = jnp.zeros((KV_LORA, MLA_HEADS, HEAD_PAD - QK_NOPE), F32)
    w_k = jnp.concatenate([ukv[..., :QK_NOPE], zk], axis=-1).reshape(KV_LORA, MLA_HEADS * HEAD_PAD)
    w_v = ukv[..., QK_NOPE:].reshape(KV_LORA, MLA_HEADS * V_DIM)

    w_pool_bd = jax.scipy.linalg.block_diag(*[w_pool[g] for g in range(len(POOL_WINDOWS))])
    b_sp = jnp.repeat(b_spatial.T, SG_DIM, axis=1)
    return {
        "w_in": w_in_pad.astype(BF16), "w_pool": w_pool_bd.astype(BF16),
        "pool_scale": pool_scale.reshape(1, -1), "w_spatial": w_spatial, "b_spatial": b_sp,
        "g_q": g_q.reshape(1, -1), "w_uq": w_uq_pad.astype(BF16), "w_uq_sw": w_uq_sw.astype(BF16),
        "g_kv": g_kv.reshape(1, -1), "w_k": w_k.astype(BF16), "w_v": w_v.astype(BF16),
        "w_out": w_out.astype(BF16), "w_router_t": w_router.T, "router_bias": router_bias.reshape(-1, 1),
        "ws_gate": ws_gate.astype(BF16), "ws_up": ws_up.astype(BF16), "ws_down": ws_down.astype(BF16),
    }


def kernel(x, c, positions, w_ada, b_ada, w_in, w_pool, pool_scale, w_spatial, b_spatial, g_q, w_uq, g_kv, w_ukv, w_out, w_router, router_bias, w_gate, w_up, w_down, ws_gate, ws_up, ws_down, final_gain):
    batch, seq, d = x.shape
    depth = w_ada.shape[0]
    mod = _ada_mod(c, w_ada, b_ada)
    cos_t, sin_t = _rope_tables(positions)
    x_parts = (x.reshape(batch * seq, d),)
    fg = final_gain.reshape(1, d)
    wg_b, wu_b, wd_b = w_gate, w_up, w_down
    ts = min(MOE_TOKENS, batch * seq)
    for l in range(depth):
        p = _prep_layer(w_in[l], w_pool[l], pool_scale[l], w_spatial[l], b_spatial[l], g_q[l], w_uq[l],
                        g_kv[l], w_ukv[l], w_out[l], w_router[l], router_bias[l], ws_gate[l], ws_up[l],
                        ws_down[l])
        shift1, scale1, gate1, shift2, scale2, gate2 = [
            mod[l, :, k * d:(k + 1) * d].reshape(batch, 1, d) for k in range(6)]
        outs = _mixer_in(x_parts, shift1, scale1, cos_t, sin_t, p, seq)
        mix, q, k, v = outs[:4]
        xt = outs[4] if len(outs) == 5 else x_parts[0]
        att = _attention(q, k, v, batch, seq)
        base, h2rows, cw_t = _mixer_out(mix, att, xt, gate1, shift2, scale2, gate2, p, seq)
        tok, wts, cnt = _plan(cw_t, ts)
        routed = _moe(h2rows, tok, wts, cnt[:, :, 0], wg_b, wu_b, wd_b, l, ts)
        x_parts = (base, routed, gate2)
    return _final(*x_parts, fg, seq).reshape(batch, seq, d)
```

```python
import functools

import jax
import jax.numpy as jnp
from jax import lax
from jax.experimental import pallas as pl
from jax.experimental.pallas import tpu as pltpu

F32 = jnp.float32
BF16 = jnp.bfloat16

EPS = 1e-6
LANES = 128
POOL_WINDOWS = (2, 4, 8, 16)
POOL_WIDTH = 256
POOL_CH = 64
MAX_WINDOW = 16
SG_HEADS = 4
SG_WIDTH = 256
SG_DIM = 64
SG_CHUNK = 128
MLA_HEADS = 8
V_DIM = 64
QK_NOPE = 64
QK_ROPE = 32
HALF_ROPE = QK_ROPE // 2
Q_LORA = 256
KV_LORA = 128
ROPE_THETA = 10000.0
ATTN_SCALE = (QK_NOPE + QK_ROPE) ** -0.5
HEAD_PAD = 128
N_EXPERTS = 64
TOP_K = 8
N_EXPERT_GROUPS = 8
GROUP_SIZE = N_EXPERTS // N_EXPERT_GROUPS
TOPK_GROUPS = 4
ROUTED_SCALE = 2.5
VMEM_LIMIT = 52 * 1024 * 1024
TOKEN_TILE = 1024


def _dot(a, b):
    return jnp.dot(a, b, preferred_element_type=F32)


def _dot_nt(a, b):
    return lax.dot_general(a, b, (((1,), (1,)), ((), ())), preferred_element_type=F32)


def _rms(x):
    return x * lax.rsqrt(jnp.mean(x * x, axis=-1, keepdims=True) + EPS)


def _gelu_tanh(x):
    c = (2.0 / jnp.pi) ** 0.5
    return x * (0.5 * (1.0 + jnp.tanh(c * (x + 0.044715 * (x * x * x)))))


def _silu(x):
    return x * jax.nn.sigmoid(x)


def _ada_kernel(c_ref, w_ref, b_ref, o_ref):
    cond = _silu(c_ref[...])
    o_ref[0] = _dot(cond.astype(BF16), w_ref[0].astype(BF16)) + b_ref[0]


def _ada_mod(c, w_ada, b_ada):
    depth, d, n = w_ada.shape
    b = c.shape[0]
    nt = 1536
    return pl.pallas_call(
        _ada_kernel,
        grid=(depth, n // nt),
        in_specs=[pl.BlockSpec((b, d), lambda l, j: (0, 0)),
                  pl.BlockSpec((1, d, nt), lambda l, j: (l, 0, j)),
                  pl.BlockSpec((1, 1, nt), lambda l, j: (l, 0, j))],
        out_specs=pl.BlockSpec((1, b, nt), lambda l, j: (l, 0, j)),
        out_shape=jax.ShapeDtypeStruct((depth, b, n), F32),
        compiler_params=pltpu.CompilerParams(vmem_limit_bytes=VMEM_LIMIT),
        name="ada_mod",
    )(c, w_ada, b_ada.reshape(depth, 1, n))


def _rope_kernel(pos_ref, invf_ref, sign_ref, cos_ref, sin_ref):
    ang = pos_ref[...].astype(F32) * invf_ref[...]
    cos_ref[...] = jnp.cos(ang)
    sin_ref[...] = jnp.sin(ang) * sign_ref[...]


def _rope_tables(positions):
    t = positions.size
    tm = 2048
    inv_freq = ROPE_THETA ** (-jnp.arange(0, QK_ROPE, 2, dtype=F32) / QK_ROPE)
    invf = jnp.zeros((1, HEAD_PAD), F32)
    invf = invf.at[0, QK_NOPE:QK_NOPE + HALF_ROPE].set(inv_freq)
    invf = invf.at[0, QK_NOPE + HALF_ROPE:QK_NOPE + QK_ROPE].set(inv_freq)
    sign = jnp.zeros((1, HEAD_PAD), F32)
    sign = sign.at[0, QK_NOPE:QK_NOPE + HALF_ROPE].set(-1.0)
    sign = sign.at[0, QK_NOPE + HALF_ROPE:QK_NOPE + QK_ROPE].set(1.0)
    return pl.pallas_call(
        _rope_kernel,
        grid=(t // tm,),
        in_specs=[pl.BlockSpec((tm, 1), lambda i: (i, 0)),
                  pl.BlockSpec((1, HEAD_PAD), lambda i: (0, 0)),
                  pl.BlockSpec((1, HEAD_PAD), lambda i: (0, 0))],
        out_specs=[pl.BlockSpec((tm, HEAD_PAD), lambda i: (i, 0)),
                   pl.BlockSpec((tm, HEAD_PAD), lambda i: (i, 0))],
        out_shape=[jax.ShapeDtypeStruct((t, HEAD_PAD), F32)] * 2,
        name="rope_tables",
    )(positions.reshape(t, 1), invf, sign)


def _mixer_in_kernel(*refs, tiles_per_seq, fused):
    if fused:
        base_ref, r_ref, gprev_ref = refs[:3]
        (shift_ref, scale_ref, win_ref, wpool_ref, pscale_ref, wsp_ref, bsp_ref, gq_ref, wuq_ref, wuqs_ref,
         gkv_ref, wk_ref, wv_ref, cos_ref, sin_ref, mix_ref, q_ref, k_ref, v_ref, x_out_ref, carry_ref) = refs[3:]
        tm = base_ref.shape[0]
        x = base_ref[...] + gprev_ref[0] * jnp.concatenate(
            [r_ref[pl.ds(j, tm, stride=ROW_TILE), :] for j in range(ROW_TILE)], axis=1)
        x_out_ref[...] = x
    else:
        (x_ref, shift_ref, scale_ref, win_ref, wpool_ref, pscale_ref, wsp_ref, bsp_ref, gq_ref, wuq_ref, wuqs_ref,
         gkv_ref, wk_ref, wv_ref, cos_ref, sin_ref, mix_ref, q_ref, k_ref, v_ref, carry_ref) = refs
        tm = x_ref.shape[0]
        x = x_ref[...]
    ti = pl.program_id(0) % tiles_per_seq
    h = _rms(x) * (1.0 + scale_ref[0]) + shift_ref[0]
    z = _dot(h.astype(BF16), win_ref[...])

    a = z[:, 0:POOL_WIDTH]

    @pl.when(ti == 0)
    def _():
        carry_ref[...] = jnp.zeros_like(carry_ref)

    ext = jnp.concatenate([carry_ref[...], a], axis=0)
    carry_ref[...] = a[tm - MAX_WINDOW:, :]
    p1 = ext + pltpu.roll(ext, 1, 0)
    p2 = p1 + pltpu.roll(p1, 2, 0)
    p3 = p2 + pltpu.roll(p2, 4, 0)
    p4 = p3 + pltpu.roll(p3, 8, 0)
    lane = lax.broadcasted_iota(jnp.int32, (tm, POOL_WIDTH), 1)
    row = lax.broadcasted_iota(jnp.int32, (tm, POOL_WIDTH), 0) + (ti * tm + 1)
    g0, g1, g2 = lane < POOL_CH, lane < 2 * POOL_CH, lane < 3 * POOL_CH
    wsum = jnp.where(g0, p1[MAX_WINDOW:], jnp.where(g1, p2[MAX_WINDOW:],
                     jnp.where(g2, p3[MAX_WINDOW:], p4[MAX_WINDOW:])))
    width = jnp.where(g0, POOL_WINDOWS[0], jnp.where(g1, POOL_WINDOWS[1],
                      jnp.where(g2, POOL_WINDOWS[2], POOL_WINDOWS[3])))
    cnt = jnp.minimum(row, width).astype(F32)
    dlt = wsum / cnt - a
    y_pool = _dot(dlt.astype(BF16), wpool_ref[...]) * pscale_ref[...]
    mix_ref[:, 0:POOL_WIDTH] = y_pool.astype(mix_ref.dtype)

    ug = _gelu_tanh(z[:, POOL_WIDTH:POOL_WIDTH + SG_WIDTH])
    vg = _gelu_tanh(z[:, POOL_WIDTH + SG_WIDTH:POOL_WIDTH + 2 * SG_WIDTH])
    mu = jnp.mean(vg, axis=-1, keepdims=True)
    vc = vg - mu
    var = jnp.mean(vc * vc, axis=-1, keepdims=True)
    vn = (vc * lax.rsqrt(var + EPS)).astype(BF16)
    r_i = lax.broadcasted_iota(jnp.int32, (SG_CHUNK, SG_CHUNK), 0)
    c_i = lax.broadcasted_iota(jnp.int32, (SG_CHUNK, SG_CHUNK), 1)
    wms = [jnp.where(c_i <= r_i, wsp_ref[hh], 0.0).astype(BF16) for hh in range(SG_HEADS)]
    lane_head = lax.broadcasted_iota(jnp.int32, (SG_CHUNK, SG_WIDTH), 1) // SG_DIM
    for cidx in range(tm // SG_CHUNK):
        rows = slice(cidx * SG_CHUNK, (cidx + 1) * SG_CHUNK)
        vchunk = vn[rows]
        mixed = bsp_ref[...]
        for hh in range(SG_HEADS):
            mixed = mixed + jnp.where(lane_head == hh, _dot(wms[hh], vchunk), 0.0)
        mix_ref[rows, POOL_WIDTH:POOL_WIDTH + SG_WIDTH] = (ug[rows] * mixed).astype(mix_ref.dtype)

    o_cq = POOL_WIDTH + 2 * SG_WIDTH
    o_ckv = o_cq + Q_LORA
    o_kpe = o_ckv + KV_LORA
    cos = cos_ref[...]
    sin = sin_ref[...]
    cqn = (_rms(z[:, o_cq:o_ckv]) * gq_ref[...]).astype(BF16)
    q = _dot(cqn, wuq_ref[...])
    qs = _dot(cqn, wuqs_ref[...])
    ckvn = (_rms(z[:, o_ckv:o_kpe]) * gkv_ref[...]).astype(BF16)
    kn = _dot(ckvn, wk_ref[...])
    kpe = z[:, o_kpe:o_kpe + HEAD_PAD] * cos + z[:, o_kpe + HEAD_PAD:o_kpe + 2 * HEAD_PAD] * sin
    for hh in range(MLA_HEADS):
        blk = slice(hh * HEAD_PAD, (hh + 1) * HEAD_PAD)
        q_ref[:, blk] = ((q[:, blk] * cos + qs[:, blk] * sin) * ATTN_SCALE).astype(q_ref.dtype)
        k_ref[:, blk] = (kn[:, blk] + kpe).astype(k_ref.dtype)
    v_ref[...] = _dot(ckvn, wv_ref[...]).astype(v_ref.dtype)


def _mixer_in(x_parts, shift, scale, cos_t, sin_t, p, seq):
    fused = len(x_parts) == 3
    t, d = x_parts[0].shape
    tm = TOKEN_TILE
    tiles_per_seq = seq // tm
    nz = p["w_in"].shape[1]
    hq = MLA_HEADS * HEAD_PAD

    def full(shape):
        return pl.BlockSpec(shape, lambda i: (0,) * len(shape))

    def mod():
        return pl.BlockSpec((1, 1, d), lambda i: (i // tiles_per_seq, 0, 0))

    x_specs = [pl.BlockSpec((tm, d), lambda i: (i, 0))]
    extra_out_specs, extra_out_shapes = [], []
    if fused:
        x_specs += [pl.BlockSpec((tm * ROW_TILE, LANES), lambda i: (i, 0)), mod()]
        extra_out_specs = [pl.BlockSpec((tm, d), lambda i: (i, 0))]
        extra_out_shapes = [jax.ShapeDtypeStruct((t, d), F32)]
    return pl.pallas_call(
        functools.partial(_mixer_in_kernel, tiles_per_seq=tiles_per_seq, fused=fused),
        grid=(t // tm,),
        in_specs=x_specs + [mod(), mod(),
                  full((d, nz)), full((POOL_WIDTH, POOL_WIDTH)), full((1, POOL_WIDTH)),
                  full((SG_HEADS, SG_CHUNK, SG_CHUNK)), full((SG_CHUNK, SG_WIDTH)),
                  full((1, Q_LORA)), full((Q_LORA, hq)), full((Q_LORA, hq)),
                  full((1, KV_LORA)), full((KV_LORA, hq)), full((KV_LORA, MLA_HEADS * V_DIM)),
                  pl.BlockSpec((tm, HEAD_PAD), lambda i: (i, 0)),
                  pl.BlockSpec((tm, HEAD_PAD), lambda i: (i, 0))],
        out_specs=[pl.BlockSpec((tm, POOL_WIDTH + SG_WIDTH), lambda i: (i, 0)),
                   pl.BlockSpec((tm, hq), lambda i: (i, 0)),
                   pl.BlockSpec((tm, hq), lambda i: (i, 0)),
                   pl.BlockSpec((tm, MLA_HEADS * V_DIM), lambda i: (i, 0))] + extra_out_specs,
        out_shape=[jax.ShapeDtypeStruct((t, POOL_WIDTH + SG_WIDTH), BF16),
                   jax.ShapeDtypeStruct((t, hq), BF16),
                   jax.ShapeDtypeStruct((t, hq), BF16),
                   jax.ShapeDtypeStruct((t, MLA_HEADS * V_DIM), BF16)] + extra_out_shapes,
        scratch_shapes=[pltpu.VMEM((MAX_WINDOW, POOL_WIDTH), F32)],
        compiler_params=pltpu.CompilerParams(dimension_semantics=("arbitrary",),
                                             vmem_limit_bytes=VMEM_LIMIT),
        name="mixer_in",
    )(*x_parts, shift, scale, p["w_in"], p["w_pool"], p["pool_scale"], p["w_spatial"], p["b_spatial"],
      p["g_q"], p["w_uq"], p["w_uq_sw"], p["g_kv"], p["w_k"], p["w_v"], cos_t, sin_t)


ATTN_TQ = 256
ATTN_LOOKAHEAD = 2


def _attn_kernel(q_ref, k_ref, v_ref, o_ref):
    seq = q_ref.shape[0]
    tq = ATTN_TQ
    r_i = lax.broadcasted_iota(jnp.int32, (tq, tq), 0)
    c_i = lax.broadcasted_iota(jnp.int32, (tq, tq), 1)
    lane = lax.broadcasted_iota(jnp.int32, (tq, 2 * V_DIM), 1)

    def scores(i, hh):
        lo, hi = i * tq, (i + 1) * tq
        blk = slice(hh * HEAD_PAD, (hh + 1) * HEAD_PAD)
        q = q_ref[lo:hi, blk]
        s_d = jnp.where(c_i <= r_i, _dot_nt(q, k_ref[lo:hi, blk]), -jnp.inf)
        s_o = _dot_nt(q, k_ref[0:lo, blk]) if i > 0 else None
        return s_d, s_o

    def finish(i, s_d, s_o):
        lo, hi = i * tq, (i + 1) * tq
        m = jnp.max(s_d, axis=-1, keepdims=True)
        if s_o is not None:
            m = jnp.maximum(m, jnp.max(s_o, axis=-1, keepdims=True))
            p_o = jnp.exp(s_o - m)
        p_d = jnp.exp(s_d - m)
        l = jnp.sum(p_d, axis=-1, keepdims=True)
        acc = _dot(p_d.astype(BF16), v_ref[lo:hi, :])
        if s_o is not None:
            l = l + jnp.sum(p_o, axis=-1, keepdims=True)
            acc = acc + _dot(p_o.astype(BF16), v_ref[0:lo, :])
        return acc * (1.0 / l)

    chains = [(i, hh) for i in range(seq // tq) for hh in range(2)]
    pending = [scores(*chains[c]) for c in range(ATTN_LOOKAHEAD)]
    outs = []
    for c, (i, hh) in enumerate(chains):
        if c + ATTN_LOOKAHEAD < len(chains):
            pending.append(scores(*chains[c + ATTN_LOOKAHEAD]))
        outs.append(finish(i, *pending.pop(0)))
        if hh == 1:
            o_ref[i * tq:(i + 1) * tq, :] = jnp.where(lane < V_DIM, outs[-2], outs[-1]).astype(o_ref.dtype)


def _attention(q, k, v, batch, seq):
    t = q.shape[0]
    return pl.pallas_call(
        _attn_kernel,
        grid=(batch, MLA_HEADS // 2),
        in_specs=[pl.BlockSpec((seq, 2 * HEAD_PAD), lambda b, hp: (b, hp)),
                  pl.BlockSpec((seq, 2 * HEAD_PAD), lambda b, hp: (b, hp)),
                  pl.BlockSpec((seq, 2 * V_DIM), lambda b, hp: (b, hp))],
        out_specs=pl.BlockSpec((seq, 2 * V_DIM), lambda b, hp: (b, hp)),
        out_shape=jax.ShapeDtypeStruct((t, MLA_HEADS * V_DIM), BF16),
        compiler_params=pltpu.CompilerParams(
            dimension_semantics=("arbitrary", "arbitrary"), vmem_limit_bytes=VMEM_LIMIT),
        name="attention",
    )(q, k, v)


def _first_max_index(cur, idx, sentinel):
    m = jnp.max(cur, axis=0, keepdims=True)
    first = jnp.min(jnp.where(cur == m, idx, sentinel), axis=0, keepdims=True)
    return m, first


def _route(logits_t, bias_t):
    n_tok = logits_t.shape[1]
    scores = jax.nn.sigmoid(logits_t)
    sel = scores + bias_t
    neg = -jnp.inf
    sub = lax.broadcasted_iota(jnp.int32, (GROUP_SIZE, n_tok), 0).astype(F32)
    gid = lax.broadcasted_iota(jnp.int32, (N_EXPERT_GROUPS, n_tok), 0).astype(F32)
    gscore = jnp.zeros((N_EXPERT_GROUPS, n_tok), F32)
    for g in range(N_EXPERT_GROUPS):
        s = sel[g * GROUP_SIZE:(g + 1) * GROUP_SIZE]
        m1, i1 = _first_max_index(s, sub, float(GROUP_SIZE))
        m2 = jnp.max(jnp.where(sub == i1, neg, s), axis=0, keepdims=True)
        gscore = jnp.where(gid == float(g), m1 + m2, gscore)
    eid = lax.broadcasted_iota(jnp.int32, (N_EXPERTS, n_tok), 0).astype(F32)
    egroup = jnp.floor(eid * (1.0 / GROUP_SIZE))
    allowed = jnp.zeros((N_EXPERTS, n_tok), F32)
    cur = gscore
    for _ in range(TOPK_GROUPS):
        _, gi = _first_max_index(cur, gid, float(N_EXPERT_GROUPS))
        cur = jnp.where(gid == gi, neg, cur)
        allowed = jnp.where(egroup == gi, 1.0, allowed)
    cur = jnp.where(allowed > 0.0, sel, neg)
    chosen = jnp.zeros((N_EXPERTS, n_tok), F32)
    for _ in range(TOP_K):
        _, ei = _first_max_index(cur, eid, float(N_EXPERTS))
        hit = eid == ei
        cur = jnp.where(hit, neg, cur)
        chosen = jnp.where(hit, 1.0, chosen)
    w = jnp.where(chosen > 0.0, scores, 0.0)
    return w / jnp.sum(w, axis=0, keepdims=True) * ROUTED_SCALE


def _split_bf16(x):
    hi = x.astype(BF16)
    lo = (x - hi.astype(F32)).astype(BF16)
    return hi, lo


ROW_TILE = 8


def _mixer_out_kernel(mix_ref, att_ref, x_ref, gate1_ref, shift_ref, scale_ref, gate2_ref, wo_ref, wr_ref,
                      rb_ref, sg_ref, su_ref, sd_ref, base_ref, h2r_ref, cw_ref):
    tm = x_ref.shape[0]
    half = mix_ref.shape[1]
    y = _dot(mix_ref[...], wo_ref[0:half, :]) + _dot(att_ref[...], wo_ref[half:, :])
    x2 = x_ref[...] + gate1_ref[0] * y
    h2 = _rms(x2) * (1.0 + scale_ref[0]) + shift_ref[0]
    for j in range(ROW_TILE):
        h2r_ref[pl.ds(j, tm, stride=ROW_TILE), :] = h2[:, j * LANES:(j + 1) * LANES]
    hb = h2.astype(BF16)
    act = _silu(_dot(hb, sg_ref[...])) * _dot(hb, su_ref[...])
    base_ref[...] = x2 + gate2_ref[0] * _dot(act.astype(BF16), sd_ref[...])
    h_hi, h_lo = _split_bf16(h2)
    w_hi, w_lo = _split_bf16(wr_ref[...])
    logits_t = _dot_nt(w_hi, h_hi) + (_dot_nt(w_hi, h_lo) + _dot_nt(w_lo, h_hi))
    cw_ref[...] = _route(logits_t, rb_ref[...])


def _mixer_out(mix, att, x2d, gate1, shift, scale, gate2, p, seq):
    t, d = x2d.shape
    tm = TOKEN_TILE
    tiles_per_seq = seq // tm
    ff = p["ws_gate"].shape[1]

    def full(shape):
        return pl.BlockSpec(shape, lambda i: (0,) * len(shape))

    def mod():
        return pl.BlockSpec((1, 1, d), lambda i: (i // tiles_per_seq, 0, 0))

    return pl.pallas_call(
        _mixer_out_kernel,
        grid=(t // tm,),
        in_specs=[pl.BlockSpec((tm, mix.shape[1]), lambda i: (i, 0)),
                  pl.BlockSpec((tm, att.shape[1]), lambda i: (i, 0)),
                  pl.BlockSpec((tm, d), lambda i: (i, 0)), mod(), mod(), mod(), mod(),
                  full((d, d)), full((N_EXPERTS, d)), full((N_EXPERTS, 1)),
                  full((d, ff)), full((d, ff)), full((ff, d))],
        out_specs=[pl.BlockSpec((tm, d), lambda i: (i, 0)),
                   pl.BlockSpec((tm * ROW_TILE, LANES), lambda i: (i, 0)),
                   pl.BlockSpec((N_EXPERTS, tm), lambda i: (0, i))],
        out_shape=[jax.ShapeDtypeStruct((t, d), F32),
                   jax.ShapeDtypeStruct((t * ROW_TILE, LANES), F32),
                   jax.ShapeDtypeStruct((N_EXPERTS, t), F32)],
        compiler_params=pltpu.CompilerParams(dimension_semantics=("arbitrary",),
                                             vmem_limit_bytes=VMEM_LIMIT),
        name="mixer_out",
    )(mix, att, x2d, gate1, shift, scale, gate2, p["w_out"], p["w_router_t"], p["router_bias"],
      p["ws_gate"], p["ws_up"], p["ws_down"])


EXPERT_BLOCK = 128
TOK_BITS = 12
RMW_BATCH = 8
MOE_TOKENS = 4096
MOE_GROUP = 5
GROUP_ROWS = MOE_GROUP * EXPERT_BLOCK


def _list_len(ts):
    return -(-ts // GROUP_ROWS) * GROUP_ROWS


def _plan_kernel(cw_ref, tok_ref, w_ref, cnt_ref):
    n_e, ts = cw_ref.shape
    cw = cw_ref[...]
    chosen = cw > 0.0
    cf = jnp.where(chosen, 1.0, 0.0).astype(BF16)
    r_i = lax.broadcasted_iota(jnp.int32, (LANES, LANES), 0)
    c_i = lax.broadcasted_iota(jnp.int32, (LANES, LANES), 1)
    before = jnp.where(r_i < c_i, 1.0, 0.0).astype(BF16)
    ones = jnp.ones((LANES, LANES), BF16)
    carry = jnp.zeros((n_e, LANES), F32)
    ranks = []
    for k in range(ts // LANES):
        ck = cf[:, k * LANES:(k + 1) * LANES]
        ranks.append(_dot(ck, before) + carry)
        carry = carry + _dot(ck, ones)
    rank = jnp.concatenate(ranks, axis=1).astype(jnp.int32)
    lane = lax.broadcasted_iota(jnp.int32, (n_e, ts), 1)
    packed = jnp.where(chosen, ((lane - rank) << TOK_BITS) | lane, -1)
    w = jnp.where(chosen, cw, 0.0)
    for bit in range(ts.bit_length() - 1):
        step = 1 << bit
        src_p = pltpu.roll(packed, ts - step, 1)
        src_w = pltpu.roll(w, ts - step, 1)
        take = (src_p >= 0) & (lane < ts - step) & (((src_p >> (TOK_BITS + bit)) & 1) == 1)
        keep = (packed >= 0) & (((packed >> (TOK_BITS + bit)) & 1) == 0)
        packed = jnp.where(take, src_p, jnp.where(keep, packed, -1))
        w = jnp.where(take, src_w, jnp.where(keep, w, 0.0))
    count = carry[:, 0:1].astype(jnp.int32)
    valid = packed >= 0
    tok = packed & ((1 << TOK_BITS) - 1)
    tok_last = jnp.max(jnp.where(valid, tok, 0).astype(F32), axis=1, keepdims=True).astype(jnp.int32)
    w_last = jnp.sum(jnp.where(lane == count - 1, w, 0.0), axis=1, keepdims=True)
    batch_end = (count + (RMW_BATCH - 1)) & (-RMW_BATCH)
    tok = jnp.where(valid, tok, tok_last)
    w = jnp.where(valid, w, jnp.where(lane < batch_end, w_last, 0.0))
    n_tail = tok_ref.shape[2] - ts
    if n_tail:
        tok = jnp.concatenate([tok, jnp.broadcast_to(tok_last, (n_e, n_tail))], axis=1)
        w = jnp.concatenate([w, jnp.zeros((n_e, n_tail), F32)], axis=1)
    tok_ref[0] = tok * ROW_TILE
    w_ref[0] = w
    cnt_ref[0] = jnp.broadcast_to(count, (n_e, LANES))


def _plan(cw_t, ts):
    n_e, t = cw_t.shape
    n_sup = t // ts
    n_list = _list_len(ts)
    assert ts <= (1 << TOK_BITS)
    return pl.pallas_call(
        _plan_kernel,
        grid=(n_sup,),
        in_specs=[pl.BlockSpec((n_e, ts), lambda s: (0, s))],
        out_specs=[pl.BlockSpec((1, n_e, n_list), lambda s: (s, 0, 0)),
                   pl.BlockSpec((1, n_e, n_list), lambda s: (s, 0, 0)),
                   pl.BlockSpec((1, n_e, LANES), lambda s: (s, 0, 0))],
        out_shape=[jax.ShapeDtypeStruct((n_sup, n_e, n_list), jnp.int32),
                   jax.ShapeDtypeStruct((n_sup, n_e, n_list), F32),
                   jax.ShapeDtypeStruct((n_sup, n_e, LANES), jnp.int32)],
        compiler_params=pltpu.CompilerParams(dimension_semantics=("arbitrary",),
                                             vmem_limit_bytes=VMEM_LIMIT),
        name="plan",
    )(cw_t)


XT_PITCH = GROUP_ROWS + 1


SHORT_ROWS = 528


def _moe_gather(list_ref, first, h_ref, xt_ref, lo=0, hi=GROUP_ROWS):
    for r in range(lo, hi):
        start = pl.multiple_of(list_ref[0, 0, first + r], ROW_TILE)
        xt_ref[pl.ds(r, ROW_TILE, stride=XT_PITCH), :] = h_ref[pl.ds(start, ROW_TILE), :]


def _moe_scatter_add(list_ref, first, yt_ref, acc_ref, lo=0, hi=GROUP_ROWS):
    for grp in range(lo // RMW_BATCH, hi // RMW_BATCH):
        rows = range(grp * RMW_BATCH, (grp + 1) * RMW_BATCH)
        starts = [pl.multiple_of(list_ref[0, 0, first + r], ROW_TILE) for r in rows]
        olds = [acc_ref[pl.ds(st, ROW_TILE), :] for st in starts]
        news = [old + yt_ref[pl.ds(r, ROW_TILE, stride=XT_PITCH), :] for r, old in zip(rows, olds)]
        for st, new in zip(starts, news):
            acc_ref[pl.ds(st, ROW_TILE), :] = new


def _moe_experts(q, xt_ref, w_ref, wg_ref, wu_ref, wd_ref, yt_ref, rows=GROUP_ROWS):
    x = jnp.concatenate(
        [xt_ref[pl.ds(j * XT_PITCH, rows), :] for j in range(ROW_TILE)], axis=1).astype(BF16)
    act = _silu(_dot(x, wg_ref[0, 0].astype(BF16))) * _dot(x, wu_ref[0, 0].astype(BF16))
    y = _dot(act.astype(BF16), wd_ref[0, 0].astype(BF16))
    w_col = jnp.concatenate(
        [jnp.broadcast_to(w_ref[0, q * MOE_GROUP + k], (EXPERT_BLOCK, EXPERT_BLOCK)).T
         for k in range(MOE_GROUP)], axis=0)[0:rows]
    for j in range(ROW_TILE):
        yt_ref[pl.ds(j * XT_PITCH, rows), :] = y[:, j * LANES:(j + 1) * LANES] * w_col


def _groups(count):
    return jnp.maximum((count + (GROUP_ROWS - 1)) // GROUP_ROWS, 1)


def _moe_kernel(cnt_ref, tokp_ref, tok_ref, tokn_ref, w_ref, h_ref, wg_ref, wu_ref, wd_ref, acc_ref,
                xt0_ref, xt1_ref, yt_ref):
    s = pl.program_id(0)
    e = pl.program_id(1)
    last_e = pl.num_programs(1) - 1
    cnt = cnt_ref[s, e]
    cnt_prev = cnt_ref[s, jnp.maximum(e - 1, 0)]
    cnt_next = cnt_ref[s, jnp.minimum(e + 1, last_e)]
    n = _groups(cnt)
    n_prev = _groups(cnt_prev)
    prev_first = (n_prev - 1) * GROUP_ROWS
    short = cnt <= SHORT_ROWS

    @pl.when(e == 0)
    def _():
        acc_ref[...] = jnp.zeros_like(acc_ref)
        yt_ref[...] = jnp.zeros_like(yt_ref)
        _moe_gather(tok_ref, 0, h_ref, xt0_ref)

    def experts(q, xt_ref, rows=GROUP_ROWS):
        _moe_experts(q, xt_ref, w_ref, wg_ref, wu_ref, wd_ref, yt_ref, rows)

    def step(xt_cur, xt_next):
        @pl.when(cnt_prev > SHORT_ROWS)
        def _():
            _moe_scatter_add(tokp_ref, prev_first, yt_ref, acc_ref, SHORT_ROWS, GROUP_ROWS)

        @pl.when(cnt_next > SHORT_ROWS)
        def _():
            _moe_gather(tokn_ref, 0, h_ref, xt_next, SHORT_ROWS, GROUP_ROWS)

        single = (n == 1) & (n_prev == 1)

        @pl.when(single & short)
        def _():
            _moe_gather(tokn_ref, 0, h_ref, xt_next, 0, SHORT_ROWS)
            _moe_scatter_add(tokp_ref, 0, yt_ref, acc_ref, 0, SHORT_ROWS)
            experts(0, xt_cur, SHORT_ROWS)

        @pl.when(single & jnp.logical_not(short))
        def _():
            _moe_gather(tokn_ref, 0, h_ref, xt_next, 0, SHORT_ROWS)
            _moe_scatter_add(tokp_ref, 0, yt_ref, acc_ref, 0, SHORT_ROWS)
            experts(0, xt_cur)

        @pl.when(jnp.logical_not(single))
        def _():
            _moe_gather(tokn_ref, 0, h_ref, xt_next, 0, SHORT_ROWS)
            _moe_scatter_add(tokp_ref, prev_first, yt_ref, acc_ref, 0, SHORT_ROWS)
            _moe_gather(tok_ref, 0, h_ref, xt_cur, SHORT_ROWS, GROUP_ROWS)
            experts(0, xt_cur)

            def more(q, carry):
                _moe_scatter_add(tok_ref, (q - 1) * GROUP_ROWS, yt_ref, acc_ref)
                _moe_gather(tok_ref, q * GROUP_ROWS, h_ref, xt_cur)
                experts(q, xt_cur)
                return carry

            lax.fori_loop(1, n, more, 0)

    @pl.when(e % 2 == 0)
    def _():
        step(xt0_ref, xt1_ref)

    @pl.when(e % 2 == 1)
    def _():
        step(xt1_ref, xt0_ref)

    @pl.when(e == last_e)
    def _():
        _moe_scatter_add(tok_ref, (n - 1) * GROUP_ROWS, yt_ref, acc_ref, 0, SHORT_ROWS)

    @pl.when((e == last_e) & (cnt > SHORT_ROWS))
    def _():
        _moe_scatter_add(tok_ref, (n - 1) * GROUP_ROWS, yt_ref, acc_ref, SHORT_ROWS, GROUP_ROWS)


def _moe(h2rows, tok, wts, counts, wg, wu, wd, layer, ts):
    rows = h2rows.shape[0]
    n_sup = rows // (ts * ROW_TILE)
    d, ff = wg.shape[2], wg.shape[3]
    n_list = tok.shape[2]
    assert d == ROW_TILE * LANES and n_list == _list_len(ts)

    assert N_EXPERTS % 2 == 0

    def list_idx(shift):
        def idx(s, e, cnt):
            return (s * N_EXPERTS + jnp.clip(e + shift, 0, N_EXPERTS - 1), 0, 0)
        return idx

    def sup_idx(s, e, cnt):
        return (s, 0)

    def w_idx(s, e, cnt):
        return (layer, e, 0, 0)

    def list_spec(shift):
        return pl.BlockSpec((1, 1, n_list), list_idx(shift), memory_space=pltpu.SMEM)

    once = pl.Buffered(1)
    tile = pltpu.VMEM((ROW_TILE * XT_PITCH, LANES), F32)
    grid_spec = pltpu.PrefetchScalarGridSpec(
        num_scalar_prefetch=1,
        grid=(n_sup, N_EXPERTS),
        in_specs=[list_spec(-1), list_spec(0), list_spec(1),
                  pl.BlockSpec((1, n_list // EXPERT_BLOCK, 1, EXPERT_BLOCK),
                               lambda s, e, cnt: (s * N_EXPERTS + e, 0, 0, 0)),
                  pl.BlockSpec((ts * ROW_TILE, LANES), sup_idx, pipeline_mode=once),
                  pl.BlockSpec((1, 1, d, ff), w_idx),
                  pl.BlockSpec((1, 1, d, ff), w_idx),
                  pl.BlockSpec((1, 1, ff, d), w_idx)],
        out_specs=pl.BlockSpec((ts * ROW_TILE, LANES), sup_idx, pipeline_mode=once),
        scratch_shapes=[tile, tile, tile],
    )
    lists = tok.reshape(n_sup * N_EXPERTS, 1, n_list)
    return pl.pallas_call(
        _moe_kernel,
        grid_spec=grid_spec,
        out_shape=jax.ShapeDtypeStruct((rows, LANES), F32),
        compiler_params=pltpu.CompilerParams(dimension_semantics=("arbitrary", "arbitrary"),
                                             vmem_limit_bytes=VMEM_LIMIT),
        name="moe",
    )(counts, lists, lists, lists,
      wts.reshape(n_sup * N_EXPERTS, n_list // EXPERT_BLOCK, 1, EXPERT_BLOCK), h2rows, wg, wu, wd)


def _routed_rows_to_tile(r_ref, tm):
    return jnp.concatenate([r_ref[pl.ds(j, tm, stride=ROW_TILE), :] for j in range(ROW_TILE)], axis=1)


def _final_kernel(base_ref, r_ref, gate_ref, fg_ref, o_ref):
    tm = base_ref.shape[0]
    x = base_ref[...] + gate_ref[0] * _routed_rows_to_tile(r_ref, tm)
    o_ref[...] = _rms(x) * fg_ref[...]


def _final(base, routed, gate, final_gain, seq):
    t, d = base.shape
    tm = TOKEN_TILE
    tiles_per_seq = seq // tm
    return pl.pallas_call(
        _final_kernel,
        grid=(t // tm,),
        in_specs=[pl.BlockSpec((tm, d), lambda i: (i, 0)),
                  pl.BlockSpec((tm * ROW_TILE, LANES), lambda i: (i, 0)),
                  pl.BlockSpec((1, 1, d), lambda i: (i // tiles_per_seq, 0, 0)),
                  pl.BlockSpec((1, d), lambda i: (0, 0))],
        out_specs=pl.BlockSpec((tm, d), lambda i: (i, 0)),
        out_shape=jax.ShapeDtypeStruct((t, d), F32),
        compiler_params=pltpu.CompilerParams(dimension_semantics=("arbitrary",),
                                             vmem_limit_bytes=VMEM_LIMIT),
        name="final_norm",
    )(base, routed, gate, final_gain)


def _prep_layer(w_in, w_pool, pool_scale, w_spatial, b_spatial, g_q, w_uq, g_kv, w_ukv, w_out,
                w_router, router_bias, ws_gate, ws_up, ws_down):
    d = w_in.shape[0]
    o_kpe = POOL_WIDTH + 2 * SG_WIDTH + Q_LORA + KV_LORA
    x1 = w_in[:, o_kpe:o_kpe + HALF_ROPE]
    x2 = w_in[:, o_kpe + HALF_ROPE:o_kpe + QK_ROPE]
    zl = jnp.zeros((d, QK_NOPE), F32)
    zr = jnp.zeros((d, HEAD_PAD - QK_NOPE - QK_ROPE), F32)
    w_in_pad = jnp.concatenate([w_in[:, :o_kpe], zl, x1, x2, zr, zl, x2, x1, zr], axis=1)

    uq = w_uq.reshape(Q_LORA, MLA_HEADS, QK_NOPE + QK_ROPE)
    q1 = uq[..., QK_NOPE:QK_NOPE + HALF_ROPE]
    q2 = uq[..., QK_NOPE + HALF_ROPE:]
    zq = jnp.zeros((Q_LORA, MLA_HEADS, HEAD_PAD - QK_NOPE - QK_ROPE), F32)
    w_uq_pad = jnp.concatenate([uq, zq], axis=-1).reshape(Q_LORA, MLA_HEADS * HEAD_PAD)
    w_uq_sw = jnp.concatenate([jnp.zeros_like(uq[..., :QK_NOPE]), q2, q1, zq], axis=-1)
    w_uq_sw = w_uq_sw.reshape(Q_LORA, MLA_HEADS * HEAD_PAD)

    ukv = w_ukv.reshape(KV_LORA, MLA_HEADS, QK_NOPE + V_DIM)
    zk = jnp.zeros((KV_LORA, MLA_HEADS, HEAD_PAD - QK_NOPE), F32)
    w_k = jnp.concatenate([ukv[..., :QK_NOPE], zk], axis=-1).reshape(KV_LORA, MLA_HEADS * HEAD_PAD)
    w_v = ukv[..., QK_NOPE:].reshape(KV_LORA, MLA_HEADS * V_DIM)

    w_pool_bd = jax.scipy.linalg.block_diag(*[w_pool[g] for g in range(len(POOL_WINDOWS))])
    b_sp = jnp.repeat(b_spatial.T, SG_DIM, axis=1)
    return {
        "w_in": w_in_pad.astype(BF16), "w_pool": w_pool_bd.astype(BF16),
        "pool_scale": pool_scale.reshape(1, -1), "w_spatial": w_spatial, "b_spatial": b_sp,
        "g_q": g_q.reshape(1, -1), "w_uq": w_uq_pad.astype(BF16), "w_uq_sw": w_uq_sw.astype(BF16),
        "g_kv": g_kv.reshape(1, -1), "w_k": w_k.astype(BF16), "w_v": w_v.astype(BF16),
        "w_out": w_out.astype(BF16), "w_router_t": w_router.T, "router_bias": router_bias.reshape(-1, 1),
        "ws_gate": ws_gate.astype(BF16), "ws_up": ws_up.astype(BF16), "ws_down": ws_down.astype(BF16),
    }


def kernel(x, c, positions, w_ada, b_ada, w_in, w_pool, pool_scale, w_spatial, b_spatial, g_q, w_uq, g_kv, w_ukv, w_out, w_router, router_bias, w_gate, w_up, w_down, ws_gate, ws_up, ws_down, final_gain):
    batch, seq, d = x.shape
    depth = w_ada.shape[0]
    mod = _ada_mod(c, w_ada, b_ada)
    cos_t, sin_t = _rope_tables(positions)
    x_parts = (x.reshape(batch * seq, d),)
    fg = final_gain.reshape(1, d)
    wg_b, wu_b, wd_b = w_gate, w_up, w_down
    ts = min(MOE_TOKENS, batch * seq)
    for l in range(depth):
        p = _prep_layer(w_in[l], w_pool[l], pool_scale[l], w_spatial[l], b_spatial[l], g_q[l], w_uq[l],
                        g_kv[l], w_ukv[l], w_out[l], w_router[l], router_bias[l], ws_gate[l], ws_up[l],
                        ws_down[l])
        shift1, scale1, gate1, shift2, scale2, gate2 = [
            mod[l, :, k * d:(k + 1) * d].reshape(batch, 1, d) for k in range(6)]
        outs = _mixer_in(x_parts, shift1, scale1, cos_t, sin_t, p, seq)
        mix, q, k, v = outs[:4]
        xt = outs[4] if len(outs) == 5 else x_parts[0]
        att = _attention(q, k, v, batch, seq)
        base, h2rows, cw_t = _mixer_out(mix, att, xt, gate1, shift2, scale2, gate2, p, seq)
        tok, wts, cnt = _plan(cw_t, ts)
        routed = _moe(h2rows, tok, wts, cnt[:, :, 0], wg_b, wu_b, wd_b, l, ts)
        x_parts = (base, routed, gate2)
    return _final(*x_parts, fg, seq).reshape(batch, seq, d)
```

```python
import functools

import jax
import jax.numpy as jnp
from jax import lax
from jax.experimental import pallas as pl
from jax.experimental.pallas import tpu as pltpu

F32 = jnp.float32
BF16 = jnp.bfloat16

EPS = 1e-6
LANES = 128
POOL_WINDOWS = (2, 4, 8, 16)
POOL_WIDTH = 256
POOL_CH = 64
MAX_WINDOW = 16
SG_HEADS = 4
SG_WIDTH = 256
SG_DIM = 64
SG_CHUNK = 128
MLA_HEADS = 8
V_DIM = 64
QK_NOPE = 64
QK_ROPE = 32
HALF_ROPE = QK_ROPE // 2
Q_LORA = 256
KV_LORA = 128
ROPE_THETA = 10000.0
ATTN_SCALE = (QK_NOPE + QK_ROPE) ** -0.5
HEAD_PAD = 128
N_EXPERTS = 64
TOP_K = 8
N_EXPERT_GROUPS = 8
GROUP_SIZE = N_EXPERTS // N_EXPERT_GROUPS
TOPK_GROUPS = 4
ROUTED_SCALE = 2.5
VMEM_LIMIT = 52 * 1024 * 1024
TOKEN_TILE = 1024


def _dot(a, b):
    return jnp.dot(a, b, preferred_element_type=F32)


def _dot_nt(a, b):
    return lax.dot_general(a, b, (((1,), (1,)), ((), ())), preferred_element_type=F32)


def _rms(x):
    return x * lax.rsqrt(jnp.mean(x * x, axis=-1, keepdims=True) + EPS)


def _gelu_tanh(x):
    c = (2.0 / jnp.pi) ** 0.5
    return x * (0.5 * (1.0 + jnp.tanh(c * (x + 0.044715 * (x * x * x)))))


def _silu(x):
    return x * jax.nn.sigmoid(x)


def _ada_kernel(c_ref, w_ref, b_ref, o_ref):
    cond = _silu(c_ref[...])
    o_ref[0] = _dot(cond.astype(BF16), w_ref[0].astype(BF16)) + b_ref[0]


def _ada_mod(c, w_ada, b_ada):
    depth, d, n = w_ada.shape
    b = c.shape[0]
    nt = 1536
    return pl.pallas_call(
        _ada_kernel,
        grid=(depth, n // nt),
        in_specs=[pl.BlockSpec((b, d), lambda l, j: (0, 0)),
                  pl.BlockSpec((1, d, nt), lambda l, j: (l, 0, j)),
                  pl.BlockSpec((1, 1, nt), lambda l, j: (l, 0, j))],
        out_specs=pl.BlockSpec((1, b, nt), lambda l, j: (l, 0, j)),
        out_shape=jax.ShapeDtypeStruct((depth, b, n), F32),
        compiler_params=pltpu.CompilerParams(vmem_limit_bytes=VMEM_LIMIT),
        name="ada_mod",
    )(c, w_ada, b_ada.reshape(depth, 1, n))


def _rope_kernel(pos_ref, invf_ref, sign_ref, cos_ref, sin_ref):
    ang = pos_ref[...].astype(F32) * invf_ref[...]
    cos_ref[...] = jnp.cos(ang)
    sin_ref[...] = jnp.sin(ang) * sign_ref[...]


def _rope_tables(positions):
    t = positions.size
    tm = 2048
    inv_freq = ROPE_THETA ** (-jnp.arange(0, QK_ROPE, 2, dtype=F32) / QK_ROPE)
    invf = jnp.zeros((1, HEAD_PAD), F32)
    invf = invf.at[0, QK_NOPE:QK_NOPE + HALF_ROPE].set(inv_freq)
    invf = invf.at[0, QK_NOPE + HALF_ROPE:QK_NOPE + QK_ROPE].set(inv_freq)
    sign = jnp.zeros((1, HEAD_PAD), F32)
    sign = sign.at[0, QK_NOPE:QK_NOPE + HALF_ROPE].set(-1.0)
    sign = sign.at[0, QK_NOPE + HALF_ROPE:QK_NOPE + QK_ROPE].set(1.0)
    return pl.pallas_call(
        _rope_kernel,
        grid=(t // tm,),
        in_specs=[pl.BlockSpec((tm, 1), lambda i: (i, 0)),
                  pl.BlockSpec((1, HEAD_PAD), lambda i: (0, 0)),
                  pl.BlockSpec((1, HEAD_PAD), lambda i: (0, 0))],
        out_specs=[pl.BlockSpec((tm, HEAD_PAD), lambda i: (i, 0)),
                   pl.BlockSpec((tm, HEAD_PAD), lambda i: (i, 0))],
        out_shape=[jax.ShapeDtypeStruct((t, HEAD_PAD), F32)] * 2,
        name="rope_tables",
    )(positions.reshape(t, 1), invf, sign)


def _mixer_in_kernel(*refs, tiles_per_seq, fused):
    ti = pl.program_id(0) % tiles_per_seq
    carry_ref = refs[-1]

    @pl.when(ti == 0)
    def _():
        carry_ref[...] = jnp.zeros_like(carry_ref)

    if fused:
        base_ref, r_ref, gprev_ref = refs[:3]
        (shift_ref, scale_ref, win_ref, wpool_ref, pscale_ref, wsp_ref, bsp_ref, gq_ref, wuq_ref, wuqs_ref,
         gkv_ref, wk_ref, wv_ref, cos_ref, sin_ref, mix_ref, q_ref, k_ref, v_ref, x_out_ref, carry_ref) = refs[3:]
        tm = base_ref.shape[0]
        x = base_ref[...] + gprev_ref[0] * jnp.concatenate(
            [r_ref[pl.ds(j, tm, stride=ROW_TILE), :] for j in range(ROW_TILE)], axis=1)
        x_out_ref[...] = x
    else:
        (x_ref, shift_ref, scale_ref, win_ref, wpool_ref, pscale_ref, wsp_ref, bsp_ref, gq_ref, wuq_ref, wuqs_ref,
         gkv_ref, wk_ref, wv_ref, cos_ref, sin_ref, mix_ref, q_ref, k_ref, v_ref, carry_ref) = refs
        tm = x_ref.shape[0]
        x = x_ref[...]
    h = _rms(x) * (1.0 + scale_ref[0]) + shift_ref[0]
    z = _dot(h.astype(BF16), win_ref[...])

    a = z[:, 0:POOL_WIDTH]
    ext = jnp.concatenate([carry_ref[...], a], axis=0)
    carry_ref[...] = a[tm - MAX_WINDOW:, :]
    p1 = ext + pltpu.roll(ext, 1, 0)
    p2 = p1 + pltpu.roll(p1, 2, 0)
    p3 = p2 + pltpu.roll(p2, 4, 0)
    p4 = p3 + pltpu.roll(p3, 8, 0)
    lane = lax.broadcasted_iota(jnp.int32, (tm, POOL_WIDTH), 1)
    row = lax.broadcasted_iota(jnp.int32, (tm, POOL_WIDTH), 0) + (ti * tm + 1)
    g0, g1, g2 = lane < POOL_CH, lane < 2 * POOL_CH, lane < 3 * POOL_CH
    wsum = jnp.where(g0, p1[MAX_WINDOW:], jnp.where(g1, p2[MAX_WINDOW:],
                     jnp.where(g2, p3[MAX_WINDOW:], p4[MAX_WINDOW:])))
    width = jnp.where(g0, POOL_WINDOWS[0], jnp.where(g1, POOL_WINDOWS[1],
                      jnp.where(g2, POOL_WINDOWS[2], POOL_WINDOWS[3])))
    cnt = jnp.minimum(row, width).astype(F32)
    dlt = wsum / cnt - a
    y_pool = _dot(dlt.astype(BF16), wpool_ref[...]) * pscale_ref[...]
    mix_ref[:, 0:POOL_WIDTH] = y_pool.astype(mix_ref.dtype)

    ug = _gelu_tanh(z[:, POOL_WIDTH:POOL_WIDTH + SG_WIDTH])
    vg = _gelu_tanh(z[:, POOL_WIDTH + SG_WIDTH:POOL_WIDTH + 2 * SG_WIDTH])
    mu = jnp.mean(vg, axis=-1, keepdims=True)
    vc = vg - mu
    var = jnp.mean(vc * vc, axis=-1, keepdims=True)
    vn = (vc * lax.rsqrt(var + EPS)).astype(BF16)
    r_i = lax.broadcasted_iota(jnp.int32, (SG_CHUNK, SG_CHUNK), 0)
    c_i = lax.broadcasted_iota(jnp.int32, (SG_CHUNK, SG_CHUNK), 1)
    wms = [jnp.where(c_i <= r_i, wsp_ref[hh], 0.0).astype(BF16) for hh in range(SG_HEADS)]
    lane_head = lax.broadcasted_iota(jnp.int32, (SG_CHUNK, SG_WIDTH), 1) // SG_DIM
    for cidx in range(tm // SG_CHUNK):
        rows = slice(cidx * SG_CHUNK, (cidx + 1) * SG_CHUNK)
        vchunk = vn[rows]
        mixed = bsp_ref[...]
        for hh in range(SG_HEADS):
            mixed = mixed + jnp.where(lane_head == hh, _dot(wms[hh], vchunk), 0.0)
        mix_ref[rows, POOL_WIDTH:POOL_WIDTH + SG_WIDTH] = (ug[rows] * mixed).astype(mix_ref.dtype)

    o_cq = POOL_WIDTH + 2 * SG_WIDTH
    o_ckv = o_cq + Q_LORA
    o_kpe = o_ckv + KV_LORA
    cos = cos_ref[...]
    sin = sin_ref[...]
    cqn = (_rms(z[:, o_cq:o_ckv]) * gq_ref[...]).astype(BF16)
    q = _dot(cqn, wuq_ref[...])
    qs = _dot(cqn, wuqs_ref[...])
    ckvn = (_rms(z[:, o_ckv:o_kpe]) * gkv_ref[...]).astype(BF16)
    kn = _dot(ckvn, wk_ref[...])
    kpe = z[:, o_kpe:o_kpe + HEAD_PAD] * cos + z[:, o_kpe + HEAD_PAD:o_kpe + 2 * HEAD_PAD] * sin
    for hh in range(MLA_HEADS):
        blk = slice(hh * HEAD_PAD, (hh + 1) * HEAD_PAD)
        q_ref[:, blk] = ((q[:, blk] * cos + qs[:, blk] * sin) * ATTN_SCALE).astype(q_ref.dtype)
        k_ref[:, blk] = (kn[:, blk] + kpe).astype(k_ref.dtype)
    v_ref[...] = _dot(ckvn, wv_ref[...]).astype(v_ref.dtype)


def _mixer_in(x_parts, shift, scale, cos_t, sin_t, p, seq):
    fused = len(x_parts) == 3
    t, d = x_parts[0].shape
    tm = TOKEN_TILE
    tiles_per_seq = seq // tm
    nz = p["w_in"].shape[1]
    hq = MLA_HEADS * HEAD_PAD

    def full(shape):
        return pl.BlockSpec(shape, lambda i: (0,) * len(shape))

    def mod():
        return pl.BlockSpec((1, 1, d), lambda i: (i // tiles_per_seq, 0, 0))

    x_specs = [pl.BlockSpec((tm, d), lambda i: (i, 0))]
    extra_out_specs, extra_out_shapes = [], []
    if fused:
        x_specs += [pl.BlockSpec((tm * ROW_TILE, LANES), lambda i: (i, 0)), mod()]
        extra_out_specs = [pl.BlockSpec((tm, d), lambda i: (i, 0))]
        extra_out_shapes = [jax.ShapeDtypeStruct((t, d), F32)]
    return pl.pallas_call(
        functools.partial(_mixer_in_kernel, tiles_per_seq=tiles_per_seq, fused=fused),
        grid=(t // tm,),
        in_specs=x_specs + [mod(), mod(),
                  full((d, nz)), full((POOL_WIDTH, POOL_WIDTH)), full((1, POOL_WIDTH)),
                  full((SG_HEADS, SG_CHUNK, SG_CHUNK)), full((SG_CHUNK, SG_WIDTH)),
                  full((1, Q_LORA)), full((Q_LORA, hq)), full((Q_LORA, hq)),
                  full((1, KV_LORA)), full((KV_LORA, hq)), full((KV_LORA, MLA_HEADS * V_DIM)),
                  pl.BlockSpec((tm, HEAD_PAD), lambda i: (i, 0)),
                  pl.BlockSpec((tm, HEAD_PAD), lambda i: (i, 0))],
        out_specs=[pl.BlockSpec((tm, POOL_WIDTH + SG_WIDTH), lambda i: (i, 0)),
                   pl.BlockSpec((tm, hq), lambda i: (i, 0)),
                   pl.BlockSpec((tm, hq), lambda i: (i, 0)),
                   pl.BlockSpec((tm, MLA_HEADS * V_DIM), lambda i: (i, 0))] + extra_out_specs,
        out_shape=[jax.ShapeDtypeStruct((t, POOL_WIDTH + SG_WIDTH), BF16),
                   jax.ShapeDtypeStruct((t, hq), BF16),
                   jax.ShapeDtypeStruct((t, hq), BF16),
                   jax.ShapeDtypeStruct((t, MLA_HEADS * V_DIM), BF16)] + extra_out_shapes,
        scratch_shapes=[pltpu.VMEM((MAX_WINDOW, POOL_WIDTH), F32)],
        compiler_params=pltpu.CompilerParams(dimension_semantics=("arbitrary",),
                                             vmem_limit_bytes=VMEM_LIMIT),
        name="mixer_in",
    )(*x_parts, shift, scale, p["w_in"], p["w_pool"], p["pool_scale"], p["w_spatial"], p["b_spatial"],
      p["g_q"], p["w_uq"], p["w_uq_sw"], p["g_kv"], p["w_k"], p["w_v"], cos_t, sin_t)


ATTN_TQ = 256
ATTN_LOOKAHEAD = 2


def _attn_kernel(q_ref, k_ref, v_ref, o_ref):
    seq = q_ref.shape[0]
    tq = ATTN_TQ
    r_i = lax.broadcasted_iota(jnp.int32, (tq, tq), 0)
    c_i = lax.broadcasted_iota(jnp.int32, (tq, tq), 1)
    lane = lax.broadcasted_iota(jnp.int32, (tq, 2 * V_DIM), 1)

    def scores(i, hh):
        lo, hi = i * tq, (i + 1) * tq
        blk = slice(hh * HEAD_PAD, (hh + 1) * HEAD_PAD)
        q = q_ref[lo:hi, blk]
        s_d = jnp.where(c_i <= r_i, _dot_nt(q, k_ref[lo:hi, blk]), -jnp.inf)
        s_o = _dot_nt(q, k_ref[0:lo, blk]) if i > 0 else None
        return s_d, s_o

    def finish(i, s_d, s_o):
        lo, hi = i * tq, (i + 1) * tq
        m = jnp.max(s_d, axis=-1, keepdims=True)
        if s_o is not None:
            m = jnp.maximum(m, jnp.max(s_o, axis=-1, keepdims=True))
            p_o = jnp.exp(s_o - m)
        p_d = jnp.exp(s_d - m)
        l = jnp.sum(p_d, axis=-1, keepdims=True)
        acc = _dot(p_d.astype(BF16), v_ref[lo:hi, :])
        if s_o is not None:
            l = l + jnp.sum(p_o, axis=-1, keepdims=True)
            acc = acc + _dot(p_o.astype(BF16), v_ref[0:lo, :])
        return acc * (1.0 / l)

    chains = [(i, hh) for i in reversed(range(seq // tq)) for hh in range(2)]
    pending = [scores(*chains[c]) for c in range(ATTN_LOOKAHEAD)]
    outs = []
    for c, (i, hh) in enumerate(chains):
        if c + ATTN_LOOKAHEAD < len(chains):
            pending.append(scores(*chains[c + ATTN_LOOKAHEAD]))
        outs.append(finish(i, *pending.pop(0)))
        if hh == 1:
            o_ref[i * tq:(i + 1) * tq, :] = jnp.where(lane < V_DIM, outs[-2], outs[-1]).astype(o_ref.dtype)


def _attention(q, k, v, batch, seq):
    t = q.shape[0]
    return pl.pallas_call(
        _attn_kernel,
        grid=(batch, MLA_HEADS // 2),
        in_specs=[pl.BlockSpec((seq, 2 * HEAD_PAD), lambda b, hp: (b, hp)),
                  pl.BlockSpec((seq, 2 * HEAD_PAD), lambda b, hp: (b, hp)),
                  pl.BlockSpec((seq, 2 * V_DIM), lambda b, hp: (b, hp))],
        out_specs=pl.BlockSpec((seq, 2 * V_DIM), lambda b, hp: (b, hp)),
        out_shape=jax.ShapeDtypeStruct((t, MLA_HEADS * V_DIM), BF16),
        compiler_params=pltpu.CompilerParams(
            dimension_semantics=("arbitrary", "arbitrary"), vmem_limit_bytes=VMEM_LIMIT),
        name="attention",
    )(q, k, v)


def _first_max_index(cur, idx, sentinel):
    m = jnp.max(cur, axis=0, keepdims=True)
    first = jnp.min(jnp.where(cur == m, idx, sentinel), axis=0, keepdims=True)
    return m, first


def _route(logits_t, bias_t):
    n_tok = logits_t.shape[1]
    scores = jax.nn.sigmoid(logits_t)
    sel = scores + bias_t
    neg = -jnp.inf
    sub = lax.broadcasted_iota(jnp.int32, (GROUP_SIZE, n_tok), 0).astype(F32)
    gid = lax.broadcasted_iota(jnp.int32, (N_EXPERT_GROUPS, n_tok), 0).astype(F32)
    gscore = jnp.zeros((N_EXPERT_GROUPS, n_tok), F32)
    for g in range(N_EXPERT_GROUPS):
        s = sel[g * GROUP_SIZE:(g + 1) * GROUP_SIZE]
        m1, i1 = _first_max_index(s, sub, float(GROUP_SIZE))
        m2 = jnp.max(jnp.where(sub == i1, neg, s), axis=0, keepdims=True)
        gscore = jnp.where(gid == float(g), m1 + m2, gscore)
    eid = lax.broadcasted_iota(jnp.int32, (N_EXPERTS, n_tok), 0).astype(F32)
    egroup = jnp.floor(eid * (1.0 / GROUP_SIZE))
    allowed = jnp.zeros((N_EXPERTS, n_tok), F32)
    cur = gscore
    for _ in range(TOPK_GROUPS):
        _, gi = _first_max_index(cur, gid, float(N_EXPERT_GROUPS))
        cur = jnp.where(gid == gi, neg, cur)
        allowed = jnp.where(egroup == gi, 1.0, allowed)
    cur = jnp.where(allowed > 0.0, sel, neg)
    chosen = jnp.zeros((N_EXPERTS, n_tok), F32)
    for _ in range(TOP_K):
        _, ei = _first_max_index(cur, eid, float(N_EXPERTS))
        hit = eid == ei
        cur = jnp.where(hit, neg, cur)
        chosen = jnp.where(hit, 1.0, chosen)
    w = jnp.where(chosen > 0.0, scores, 0.0)
    return w / jnp.sum(w, axis=0, keepdims=True) * ROUTED_SCALE


def _split_bf16(x):
    hi = x.astype(BF16)
    lo = (x - hi.astype(F32)).astype(BF16)
    return hi, lo


ROW_TILE = 8


def _mixer_out_kernel(mix_ref, att_ref, x_ref, gate1_ref, shift_ref, scale_ref, gate2_ref, wo_ref, wr_ref,
                      rb_ref, sg_ref, su_ref, sd_ref, base_ref, h2r_ref, cw_ref):
    tm = x_ref.shape[0]
    half = mix_ref.shape[1]
    y = _dot(mix_ref[...], wo_ref[0:half, :]) + _dot(att_ref[...], wo_ref[half:, :])
    x2 = x_ref[...] + gate1_ref[0] * y
    h2 = _rms(x2) * (1.0 + scale_ref[0]) + shift_ref[0]
    for j in range(ROW_TILE):
        h2r_ref[pl.ds(j, tm, stride=ROW_TILE), :] = h2[:, j * LANES:(j + 1) * LANES]
    h_hi, h_lo = _split_bf16(h2)
    w_hi, w_lo = _split_bf16(wr_ref[...])
    logits_t = _dot_nt(w_hi, h_hi) + (_dot_nt(w_hi, h_lo) + _dot_nt(w_lo, h_hi))
    cw_ref[...] = _route(logits_t, rb_ref[...])
    act = _silu(_dot(h_hi, sg_ref[...])) * _dot(h_hi, su_ref[...])
    base_ref[...] = x2 + gate2_ref[0] * _dot(act.astype(BF16), sd_ref[...])


def _mixer_out(mix, att, x2d, gate1, shift, scale, gate2, p, seq):
    t, d = x2d.shape
    tm = TOKEN_TILE
    tiles_per_seq = seq // tm
    ff = p["ws_gate"].shape[1]

    def full(shape):
        return pl.BlockSpec(shape, lambda i: (0,) * len(shape))

    def mod():
        return pl.BlockSpec((1, 1, d), lambda i: (i // tiles_per_seq, 0, 0))

    return pl.pallas_call(
        _mixer_out_kernel,
        grid=(t // tm,),
        in_specs=[pl.BlockSpec((tm, mix.shape[1]), lambda i: (i, 0)),
                  pl.BlockSpec((tm, att.shape[1]), lambda i: (i, 0)),
                  pl.BlockSpec((tm, d), lambda i: (i, 0)), mod(), mod(), mod(), mod(),
                  full((d, d)), full((N_EXPERTS, d)), full((N_EXPERTS, 1)),
                  full((d, ff)), full((d, ff)), full((ff, d))],
        out_specs=[pl.BlockSpec((tm, d), lambda i: (i, 0)),
                   pl.BlockSpec((tm * ROW_TILE, LANES), lambda i: (i, 0)),
                   pl.BlockSpec((N_EXPERTS, tm), lambda i: (0, i))],
        out_shape=[jax.ShapeDtypeStruct((t, d), F32),
                   jax.ShapeDtypeStruct((t * ROW_TILE, LANES), F32),
                   jax.ShapeDtypeStruct((N_EXPERTS, t), F32)],
        compiler_params=pltpu.CompilerParams(dimension_semantics=("arbitrary",),
                                             vmem_limit_bytes=VMEM_LIMIT),
        name="mixer_out",
    )(mix, att, x2d, gate1, shift, scale, gate2, p["w_out"], p["w_router_t"], p["router_bias"],
      p["ws_gate"], p["ws_up"], p["ws_down"])


EXPERT_BLOCK = 128
TOK_BITS = 12
RMW_BATCH = 8
MOE_TOKENS = 4096
MOE_GROUP = 5
GROUP_ROWS = MOE_GROUP * EXPERT_BLOCK


def _list_len(ts):
    return -(-ts // GROUP_ROWS) * GROUP_ROWS


def _plan_kernel(cw_ref, tok_ref, w_ref, cnt_ref):
    n_e, ts = cw_ref.shape
    cw = cw_ref[...]
    chosen = cw > 0.0
    cf = jnp.where(chosen, 1.0, 0.0).astype(BF16)
    r_i = lax.broadcasted_iota(jnp.int32, (LANES, LANES), 0)
    c_i = lax.broadcasted_iota(jnp.int32, (LANES, LANES), 1)
    before = jnp.where(r_i < c_i, 1.0, 0.0).astype(BF16)
    ones = jnp.ones((LANES, LANES), BF16)
    carry = jnp.zeros((n_e, LANES), F32)
    ranks = []
    for k in range(ts // LANES):
        ck = cf[:, k * LANES:(k + 1) * LANES]
        ranks.append(_dot(ck, before) + carry)
        carry = carry + _dot(ck, ones)
    rank = jnp.concatenate(ranks, axis=1).astype(jnp.int32)
    lane = lax.broadcasted_iota(jnp.int32, (n_e, ts), 1)
    packed = jnp.where(chosen, ((lane - rank) << TOK_BITS) | lane, -1)
    w = jnp.where(chosen, cw, 0.0)
    for bit in range(ts.bit_length() - 1):
        step = 1 << bit
        src_p = pltpu.roll(packed, ts - step, 1)
        src_w = pltpu.roll(w, ts - step, 1)
        take = (src_p >= 0) & (lane < ts - step) & (((src_p >> (TOK_BITS + bit)) & 1) == 1)
        keep = (packed >= 0) & (((packed >> (TOK_BITS + bit)) & 1) == 0)
        packed = jnp.where(take, src_p, jnp.where(keep, packed, -1))
        w = jnp.where(take, src_w, jnp.where(keep, w, 0.0))
    count = carry[:, 0:1].astype(jnp.int32)
    valid = packed >= 0
    tok = packed & ((1 << TOK_BITS) - 1)
    tok_last = jnp.max(jnp.where(valid, tok, 0).astype(F32), axis=1, keepdims=True).astype(jnp.int32)
    w_last = jnp.sum(jnp.where(lane == count - 1, w, 0.0), axis=1, keepdims=True)
    batch_end = (count + (RMW_BATCH - 1)) & (-RMW_BATCH)
    tok = jnp.where(valid, tok, tok_last)
    w = jnp.where(valid, w, jnp.where(lane < batch_end, w_last, 0.0))
    n_tail = tok_ref.shape[2] - ts
    if n_tail:
        tok = jnp.concatenate([tok, jnp.broadcast_to(tok_last, (n_e, n_tail))], axis=1)
        w = jnp.concatenate([w, jnp.zeros((n_e, n_tail), F32)], axis=1)
    tok_ref[0] = tok * ROW_TILE
    w_ref[0] = w
    cnt_ref[0] = jnp.broadcast_to(count, (n_e, LANES))


def _plan(cw_t, ts):
    n_e, t = cw_t.shape
    n_sup = t // ts
    n_list = _list_len(ts)
    assert ts <= (1 << TOK_BITS)
    return pl.pallas_call(
        _plan_kernel,
        grid=(n_sup,),
        in_specs=[pl.BlockSpec((n_e, ts), lambda s: (0, s))],
        out_specs=[pl.BlockSpec((1, n_e, n_list), lambda s: (s, 0, 0)),
                   pl.BlockSpec((1, n_e, n_list), lambda s: (s, 0, 0)),
                   pl.BlockSpec((1, n_e, LANES), lambda s: (s, 0, 0))],
        out_shape=[jax.ShapeDtypeStruct((n_sup, n_e, n_list), jnp.int32),
                   jax.ShapeDtypeStruct((n_sup, n_e, n_list), F32),
                   jax.ShapeDtypeStruct((n_sup, n_e, LANES), jnp.int32)],
        compiler_params=pltpu.CompilerParams(dimension_semantics=("arbitrary",),
                                             vmem_limit_bytes=VMEM_LIMIT),
        name="plan",
    )(cw_t)


XT_PITCH = GROUP_ROWS + 1


SHORT_ROWS = 528


def _moe_gather(list_ref, first, h_ref, xt_ref, lo=0, hi=GROUP_ROWS):
    for r in range(lo, hi):
        start = pl.multiple_of(list_ref[0, 0, first + r], ROW_TILE)
        xt_ref[pl.ds(r, ROW_TILE, stride=XT_PITCH), :] = h_ref[pl.ds(start, ROW_TILE), :]


def _moe_scatter_add(list_ref, first, yt_ref, acc_ref, lo=0, hi=GROUP_ROWS):
    for grp in range(lo // RMW_BATCH, hi // RMW_BATCH):
        rows = range(grp * RMW_BATCH, (grp + 1) * RMW_BATCH)
        starts = [pl.multiple_of(list_ref[0, 0, first + r], ROW_TILE) for r in rows]
        olds = [acc_ref[pl.ds(st, ROW_TILE), :] for st in starts]
        news = [old + yt_ref[pl.ds(r, ROW_TILE, stride=XT_PITCH), :] for r, old in zip(rows, olds)]
        for st, new in zip(starts, news):
            acc_ref[pl.ds(st, ROW_TILE), :] = new


def _moe_experts(q, xt_ref, w_ref, wg_ref, wu_ref, wd_ref, yt_ref, rows=GROUP_ROWS):
    x = jnp.concatenate(
        [xt_ref[pl.ds(j * XT_PITCH, rows), :] for j in range(ROW_TILE)], axis=1).astype(BF16)
    act = _silu(_dot(x, wg_ref[0, 0].astype(BF16))) * _dot(x, wu_ref[0, 0].astype(BF16))
    y = _dot(act.astype(BF16), wd_ref[0, 0].astype(BF16))
    w_col = jnp.concatenate(
        [jnp.broadcast_to(w_ref[0, q * MOE_GROUP + k], (EXPERT_BLOCK, EXPERT_BLOCK)).T
         for k in range(MOE_GROUP)], axis=0)[0:rows]
    for j in range(ROW_TILE):
        yt_ref[pl.ds(j * XT_PITCH, rows), :] = y[:, j * LANES:(j + 1) * LANES] * w_col


def _groups(count):
    return jnp.maximum((count + (GROUP_ROWS - 1)) // GROUP_ROWS, 1)


def _moe_kernel(cnt_ref, tokp_ref, tok_ref, tokn_ref, w_ref, h_ref, wg_ref, wu_ref, wd_ref, acc_ref,
                xt0_ref, xt1_ref, yt_ref):
    s = pl.program_id(0)
    e = pl.program_id(1)
    last_e = pl.num_programs(1) - 1
    cnt = cnt_ref[s, e]
    cnt_prev = cnt_ref[s, jnp.maximum(e - 1, 0)]
    cnt_next = cnt_ref[s, jnp.minimum(e + 1, last_e)]
    n = _groups(cnt)
    n_prev = _groups(cnt_prev)
    prev_first = (n_prev - 1) * GROUP_ROWS
    short = cnt <= SHORT_ROWS

    @pl.when(e == 0)
    def _():
        acc_ref[...] = jnp.zeros_like(acc_ref)
        yt_ref[...] = jnp.zeros_like(yt_ref)
        _moe_gather(tok_ref, 0, h_ref, xt0_ref)

    def experts(q, xt_ref, rows=GROUP_ROWS):
        _moe_experts(q, xt_ref, w_ref, wg_ref, wu_ref, wd_ref, yt_ref, rows)

    def step(xt_cur, xt_next):
        @pl.when(cnt_prev > SHORT_ROWS)
        def _():
            _moe_scatter_add(tokp_ref, prev_first, yt_ref, acc_ref, SHORT_ROWS, GROUP_ROWS)

        @pl.when(cnt_next > SHORT_ROWS)
        def _():
            _moe_gather(tokn_ref, 0, h_ref, xt_next, SHORT_ROWS, GROUP_ROWS)

        single = (n == 1) & (n_prev == 1)

        @pl.when(single & short)
        def _():
            _moe_gather(tokn_ref, 0, h_ref, xt_next, 0, SHORT_ROWS)
            _moe_scatter_add(tokp_ref, 0, yt_ref, acc_ref, 0, SHORT_ROWS)
            experts(0, xt_cur, SHORT_ROWS)

        @pl.when(single & jnp.logical_not(short))
        def _():
            _moe_gather(tokn_ref, 0, h_ref, xt_next, 0, SHORT_ROWS)
            _moe_scatter_add(tokp_ref, 0, yt_ref, acc_ref, 0, SHORT_ROWS)
            experts(0, xt_cur)

        @pl.when(jnp.logical_not(single))
        def _():
            _moe_gather(tokn_ref, 0, h_ref, xt_next, 0, SHORT_ROWS)
            _moe_scatter_add(tokp_ref, prev_first, yt_ref, acc_ref, 0, SHORT_ROWS)
            _moe_gather(tok_ref, 0, h_ref, xt_cur, SHORT_ROWS, GROUP_ROWS)
            experts(0, xt_cur)

            def more(q, carry):
                _moe_scatter_add(tok_ref, (q - 1) * GROUP_ROWS, yt_ref, acc_ref)
                _moe_gather(tok_ref, q * GROUP_ROWS, h_ref, xt_cur)
                experts(q, xt_cur)
                return carry

            lax.fori_loop(1, n, more, 0)

    @pl.when(e % 2 == 0)
    def _():
        step(xt0_ref, xt1_ref)

    @pl.when(e % 2 == 1)
    def _():
        step(xt1_ref, xt0_ref)

    @pl.when(e == last_e)
    def _():
        _moe_scatter_add(tok_ref, (n - 1) * GROUP_ROWS, yt_ref, acc_ref, 0, SHORT_ROWS)

    @pl.when((e == last_e) & (cnt > SHORT_ROWS))
    def _():
        _moe_scatter_add(tok_ref, (n - 1) * GROUP_ROWS, yt_ref, acc_ref, SHORT_ROWS, GROUP_ROWS)


def _moe(h2rows, tok, wts, counts, wg, wu, wd, layer, ts):
    rows = h2rows.shape[0]
    n_sup = rows // (ts * ROW_TILE)
    d, ff = wg.shape[2], wg.shape[3]
    n_list = tok.shape[2]
    assert d == ROW_TILE * LANES and n_list == _list_len(ts)

    assert N_EXPERTS % 2 == 0

    def list_idx(shift):
        def idx(s, e, cnt):
            return (s * N_EXPERTS + jnp.clip(e + shift, 0, N_EXPERTS - 1), 0, 0)
        return idx

    def sup_idx(s, e, cnt):
        return (s, 0)

    def w_idx(s, e, cnt):
        return (layer, e, 0, 0)

    def list_spec(shift):
        return pl.BlockSpec((1, 1, n_list), list_idx(shift), memory_space=pltpu.SMEM)

    once = pl.Buffered(1)
    tile = pltpu.VMEM((ROW_TILE * XT_PITCH, LANES), F32)
    grid_spec = pltpu.PrefetchScalarGridSpec(
        num_scalar_prefetch=1,
        grid=(n_sup, N_EXPERTS),
        in_specs=[list_spec(-1), list_spec(0), list_spec(1),
                  pl.BlockSpec((1, n_list // EXPERT_BLOCK, 1, EXPERT_BLOCK),
                               lambda s, e, cnt: (s * N_EXPERTS + e, 0, 0, 0)),
                  pl.BlockSpec((ts * ROW_TILE, LANES), sup_idx, pipeline_mode=once),
                  pl.BlockSpec((1, 1, d, ff), w_idx),
                  pl.BlockSpec((1, 1, d, ff), w_idx),
                  pl.BlockSpec((1, 1, ff, d), w_idx)],
        out_specs=pl.BlockSpec((ts * ROW_TILE, LANES), sup_idx, pipeline_mode=once),
        scratch_shapes=[tile, tile, tile],
    )
    lists = tok.reshape(n_sup * N_EXPERTS, 1, n_list)
    return pl.pallas_call(
        _moe_kernel,
        grid_spec=grid_spec,
        out_shape=jax.ShapeDtypeStruct((rows, LANES), F32),
        compiler_params=pltpu.CompilerParams(dimension_semantics=("arbitrary", "arbitrary"),
                                             vmem_limit_bytes=VMEM_LIMIT),
        name="moe",
    )(counts, lists, lists, lists,
      wts.reshape(n_sup * N_EXPERTS, n_list // EXPERT_BLOCK, 1, EXPERT_BLOCK), h2rows, wg, wu, wd)


def _routed_rows_to_tile(r_ref, tm):
    return jnp.concatenate([r_ref[pl.ds(j, tm, stride=ROW_TILE), :] for j in range(ROW_TILE)], axis=1)


def _final_kernel(base_ref, r_ref, gate_ref, fg_ref, o_ref):
    tm = base_ref.shape[0]
    x = base_ref[...] + gate_ref[0] * _routed_rows_to_tile(r_ref, tm)
    o_ref[...] = _rms(x) * fg_ref[...]


def _final(base, routed, gate, final_gain, seq):
    t, d = base.shape
    tm = TOKEN_TILE
    tiles_per_seq = seq // tm
    return pl.pallas_call(
        _final_kernel,
        grid=(t // tm,),
        in_specs=[pl.BlockSpec((tm, d), lambda i: (i, 0)),
                  pl.BlockSpec((tm * ROW_TILE, LANES), lambda i: (i, 0)),
                  pl.BlockSpec((1, 1, d), lambda i: (i // tiles_per_seq, 0, 0)),
                  pl.BlockSpec((1, d), lambda i: (0, 0))],
        out_specs=pl.BlockSpec((tm, d), lambda i: (i, 0)),
        out_shape=jax.ShapeDtypeStruct((t, d), F32),
        compiler_params=pltpu.CompilerParams(dimension_semantics=("arbitrary",),
                                             vmem_limit_bytes=VMEM_LIMIT),
        name="final_norm",
    )(base, routed, gate, final_gain)


def _prep_layer(w_in, w_pool, pool_scale, w_spatial, b_spatial, g_q, w_uq, g_kv, w_ukv, w_out,
                w_router, router_bias, ws_gate, ws_up, ws_down):
    d = w_in.shape[0]
    o_kpe = POOL_WIDTH + 2 * SG_WIDTH + Q_LORA + KV_LORA
    x1 = w_in[:, o_kpe:o_kpe + HALF_ROPE]
    x2 = w_in[:, o_kpe + HALF_ROPE:o_kpe + QK_ROPE]
    zl = jnp.zeros((d, QK_NOPE), F32)
    zr = jnp.zeros((d, HEAD_PAD - QK_NOPE - QK_ROPE), F32)
    w_in_pad = jnp.concatenate([w_in[:, :o_kpe], zl, x1, x2, zr, zl, x2, x1, zr], axis=1)

    uq = w_uq.reshape(Q_LORA, MLA_HEADS, QK_NOPE + QK_ROPE)
    q1 = uq[..., QK_NOPE:QK_NOPE + HALF_ROPE]
    q2 = uq[..., QK_NOPE + HALF_ROPE:]
    zq = jnp.zeros((Q_LORA, MLA_HEADS, HEAD_PAD - QK_NOPE - QK_ROPE), F32)
    w_uq_pad = jnp.concatenate([uq, zq], axis=-1).reshape(Q_LORA, MLA_HEADS * HEAD_PAD)
    w_uq_sw = jnp.concatenate([jnp.zeros_like(uq[..., :QK_NOPE]), q2, q1, zq], axis=-1)
    w_uq_sw = w_uq_sw.reshape(Q_LORA, MLA_HEADS * HEAD_PAD)

    ukv = w_ukv.reshape(KV_LORA, MLA_HEADS, QK_NOPE + V_DIM)
    zk = jnp.zeros((KV_LORA, MLA_HEADS, HEAD_PAD - QK_NOPE), F32)
    w_k = jnp.concatenate([ukv[..., :QK_NOPE], zk], axis=-1).reshape(KV_LORA, MLA_HEADS * HEAD_PAD)
    w_v = ukv[..., QK_NOPE:].reshape(KV_LORA, MLA_HEADS * V_DIM)

    w_pool_bd = jax.scipy.linalg.block_diag(*[w_pool[g] for g in range(len(POOL_WINDOWS))])
    b_sp = jnp.repeat(b_spatial.T, SG_DIM, axis=1)
    return {
        "w_in": w_in_pad.astype(BF16), "w_pool": w_pool_bd.astype(BF16),
        "pool_scale": pool_scale.reshape(1, -1), "w_spatial": w_spatial, "b_spatial": b_sp,
        "g_q": g_q.reshape(1, -1), "w_uq": w_uq_pad.astype(BF16), "w_uq_sw": w_uq_sw.astype(BF16),
        "g_kv": g_kv.reshape(1, -1), "w_k": w_k.astype(BF16), "w_v": w_v.astype(BF16),
        "w_out": w_out.astype(BF16), "w_router_t": w_router.T, "router_bias": router_bias.reshape(-1, 1),
        "ws_gate": ws_gate.astype(BF16), "ws_up": ws_up.astype(BF16), "ws_down": ws_down.astype(BF16),
    }


def kernel(x, c, positions, w_ada, b_ada, w_in, w_pool, pool_scale, w_spatial, b_spatial, g_q, w_uq, g_kv, w_ukv, w_out, w_router, router_bias, w_gate, w_up, w_down, ws_gate, ws_up, ws_down, final_gain):
    batch, seq, d = x.shape
    depth = w_ada.shape[0]
    mod = _ada_mod(c, w_ada, b_ada)
    cos_t, sin_t = _rope_tables(positions)
    x_parts = (x.reshape(batch * seq, d),)
    fg = final_gain.reshape(1, d)
    wg_b, wu_b, wd_b = w_gate, w_up, w_down
    ts = min(MOE_TOKENS, batch * seq)
    for l in range(depth):
        p = _prep_layer(w_in[l], w_pool[l], pool_scale[l], w_spatial[l], b_spatial[l], g_q[l], w_uq[l],
                        g_kv[l], w_ukv[l], w_out[l], w_router[l], router_bias[l], ws_gate[l], ws_up[l],
                        ws_down[l])
        shift1, scale1, gate1, shift2, scale2, gate2 = [
            mod[l, :, k * d:(k + 1) * d].reshape(batch, 1, d) for k in range(6)]
        outs = _mixer_in(x_parts, shift1, scale1, cos_t, sin_t, p, seq)
        mix, q, k, v = outs[:4]
        xt = outs[4] if len(outs) == 5 else x_parts[0]
        att = _attention(q, k, v, batch, seq)
        base, h2rows, cw_t = _mixer_out(mix, att, xt, gate1, shift2, scale2, gate2, p, seq)
        tok, wts, cnt = _plan(cw_t, ts)
        routed = _moe(h2rows, tok, wts, cnt[:, :, 0], wg_b, wu_b, wd_b, l, ts)
        x_parts = (base, routed, gate2)
    return _final(*x_parts, fg, seq).reshape(batch, seq, d)
```

```python
import functools

import jax
import jax.numpy as jnp
from jax import lax
from jax.experimental import pallas as pl
from jax.experimental.pallas import tpu as pltpu

F32 = jnp.float32
BF16 = jnp.bfloat16

EPS = 1e-6
LANES = 128
POOL_WINDOWS = (2, 4, 8, 16)
POOL_WIDTH = 256
POOL_CH = 64
MAX_WINDOW = 16
SG_HEADS = 4
SG_WIDTH = 256
SG_DIM = 64
SG_CHUNK = 128
MLA_HEADS = 8
V_DIM = 64
QK_NOPE = 64
QK_ROPE = 32
HALF_ROPE = QK_ROPE // 2
Q_LORA = 256
KV_LORA = 128
ROPE_THETA = 10000.0
ATTN_SCALE = (QK_NOPE + QK_ROPE) ** -0.5
HEAD_PAD = 128
N_EXPERTS = 64
TOP_K = 8
N_EXPERT_GROUPS = 8
GROUP_SIZE = N_EXPERTS // N_EXPERT_GROUPS
TOPK_GROUPS = 4
ROUTED_SCALE = 2.5
VMEM_LIMIT = 52 * 1024 * 1024
TOKEN_TILE = 1024


def _dot(a, b):
    return jnp.dot(a, b, preferred_element_type=F32)


def _dot_nt(a, b):
    return lax.dot_general(a, b, (((1,), (1,)), ((), ())), preferred_element_type=F32)


def _rms(x):
    return x * lax.rsqrt(jnp.mean(x * x, axis=-1, keepdims=True) + EPS)


def _gelu_tanh(x):
    c = (2.0 / jnp.pi) ** 0.5
    return x * (0.5 * (1.0 + jnp.tanh(c * (x + 0.044715 * (x * x * x)))))


def _silu(x):
    return x * jax.nn.sigmoid(x)


def _ada_kernel(c_ref, w_ref, b_ref, o_ref):
    cond = _silu(c_ref[...])
    o_ref[0] = _dot(cond.astype(BF16), w_ref[0].astype(BF16)) + b_ref[0]


def _ada_mod(c, w_ada, b_ada):
    depth, d, n = w_ada.shape
    b = c.shape[0]
    nt = 1536
    return pl.pallas_call(
        _ada_kernel,
        grid=(depth, n // nt),
        in_specs=[pl.BlockSpec((b, d), lambda l, j: (0, 0)),
                  pl.BlockSpec((1, d, nt), lambda l, j: (l, 0, j)),
                  pl.BlockSpec((1, 1, nt), lambda l, j: (l, 0, j))],
        out_specs=pl.BlockSpec((1, b, nt), lambda l, j: (l, 0, j)),
        out_shape=jax.ShapeDtypeStruct((depth, b, n), F32),
        compiler_params=pltpu.CompilerParams(vmem_limit_bytes=VMEM_LIMIT),
        name="ada_mod",
    )(c, w_ada, b_ada.reshape(depth, 1, n))


def _rope_kernel(pos_ref, invf_ref, sign_ref, cos_ref, sin_ref):
    ang = pos_ref[...].astype(F32) * invf_ref[...]
    cos_ref[...] = jnp.cos(ang)
    sin_ref[...] = jnp.sin(ang) * sign_ref[...]


def _rope_tables(positions):
    t = positions.size
    tm = 2048
    inv_freq = ROPE_THETA ** (-jnp.arange(0, QK_ROPE, 2, dtype=F32) / QK_ROPE)
    invf = jnp.zeros((1, HEAD_PAD), F32)
    invf = invf.at[0, QK_NOPE:QK_NOPE + HALF_ROPE].set(inv_freq)
    invf = invf.at[0, QK_NOPE + HALF_ROPE:QK_NOPE + QK_ROPE].set(inv_freq)
    sign = jnp.zeros((1, HEAD_PAD), F32)
    sign = sign.at[0, QK_NOPE:QK_NOPE + HALF_ROPE].set(-1.0)
    sign = sign.at[0, QK_NOPE + HALF_ROPE:QK_NOPE + QK_ROPE].set(1.0)
    return pl.pallas_call(
        _rope_kernel,
        grid=(t // tm,),
        in_specs=[pl.BlockSpec((tm, 1), lambda i: (i, 0)),
                  pl.BlockSpec((1, HEAD_PAD), lambda i: (0, 0)),
                  pl.BlockSpec((1, HEAD_PAD), lambda i: (0, 0))],
        out_specs=[pl.BlockSpec((tm, HEAD_PAD), lambda i: (i, 0)),
                   pl.BlockSpec((tm, HEAD_PAD), lambda i: (i, 0))],
        out_shape=[jax.ShapeDtypeStruct((t, HEAD_PAD), F32)] * 2,
        name="rope_tables",
    )(positions.reshape(t, 1), invf, sign)


def _mixer_in_kernel(*refs, tiles_per_seq, fused):
    ti = pl.program_id(0) % tiles_per_seq
    carry_ref = refs[-1]

    @pl.when(ti == 0)
    def _():
        carry_ref[...] = jnp.zeros_like(carry_ref)

    if fused:
        base_ref, r_ref, gprev_ref = refs[:3]
        (shift_ref, scale_ref, win_ref, wpool_ref, pscale_ref, wsp_ref, bsp_ref, gq_ref, wuq_ref, wuqs_ref,
         gkv_ref, wk_ref, wv_ref, cos_ref, sin_ref, mix_ref, q_ref, k_ref, v_ref, x_out_ref, carry_ref) = refs[3:]
        tm = base_ref.shape[0]
        x = base_ref[...] + gprev_ref[0] * jnp.concatenate(
            [r_ref[pl.ds(j, tm, stride=ROW_TILE), :] for j in range(ROW_TILE)], axis=1)
        x_out_ref[...] = x
    else:
        (x_ref, shift_ref, scale_ref, win_ref, wpool_ref, pscale_ref, wsp_ref, bsp_ref, gq_ref, wuq_ref, wuqs_ref,
         gkv_ref, wk_ref, wv_ref, cos_ref, sin_ref, mix_ref, q_ref, k_ref, v_ref, carry_ref) = refs
        tm = x_ref.shape[0]
        x = x_ref[...]
    h = _rms(x) * (1.0 + scale_ref[0]) + shift_ref[0]
    z = _dot(h.astype(BF16), win_ref[...])

    a = z[:, 0:POOL_WIDTH]
    ext = jnp.concatenate([carry_ref[...], a], axis=0)
    carry_ref[...] = a[tm - MAX_WINDOW:, :]
    p1 = ext + pltpu.roll(ext, 1, 0)
    p2 = p1 + pltpu.roll(p1, 2, 0)
    p3 = p2 + pltpu.roll(p2, 4, 0)
    p4 = p3 + pltpu.roll(p3, 8, 0)
    lane = lax.broadcasted_iota(jnp.int32, (tm, POOL_WIDTH), 1)
    row = lax.broadcasted_iota(jnp.int32, (tm, POOL_WIDTH), 0) + (ti * tm + 1)
    g0, g1, g2 = lane < POOL_CH, lane < 2 * POOL_CH, lane < 3 * POOL_CH
    wsum = jnp.where(g0, p1[MAX_WINDOW:], jnp.where(g1, p2[MAX_WINDOW:],
                     jnp.where(g2, p3[MAX_WINDOW:], p4[MAX_WINDOW:])))
    width = jnp.where(g0, POOL_WINDOWS[0], jnp.where(g1, POOL_WINDOWS[1],
                      jnp.where(g2, POOL_WINDOWS[2], POOL_WINDOWS[3])))
    cnt = jnp.minimum(row, width).astype(F32)
    dlt = wsum / cnt - a
    y_pool = _dot(dlt.astype(BF16), wpool_ref[...]) * pscale_ref[...]
    mix_ref[:, 0:POOL_WIDTH] = y_pool.astype(mix_ref.dtype)

    ug = _gelu_tanh(z[:, POOL_WIDTH:POOL_WIDTH + SG_WIDTH])
    vg = _gelu_tanh(z[:, POOL_WIDTH + SG_WIDTH:POOL_WIDTH + 2 * SG_WIDTH])
    mu = jnp.mean(vg, axis=-1, keepdims=True)
    vc = vg - mu
    var = jnp.mean(vc * vc, axis=-1, keepdims=True)
    vn = (vc * lax.rsqrt(var + EPS)).astype(BF16)
    r_i = lax.broadcasted_iota(jnp.int32, (SG_CHUNK, SG_CHUNK), 0)
    c_i = lax.broadcasted_iota(jnp.int32, (SG_CHUNK, SG_CHUNK), 1)
    wms = [jnp.where(c_i <= r_i, wsp_ref[hh], 0.0).astype(BF16) for hh in range(SG_HEADS)]
    lane_head = lax.broadcasted_iota(jnp.int32, (SG_CHUNK, SG_WIDTH), 1) // SG_DIM
    for cidx in range(tm // SG_CHUNK):
        rows = slice(cidx * SG_CHUNK, (cidx + 1) * SG_CHUNK)
        vchunk = vn[rows]
        mixed = bsp_ref[...]
        for hh in range(SG_HEADS):
            mixed = mixed + jnp.where(lane_head == hh, _dot(wms[hh], vchunk), 0.0)
        mix_ref[rows, POOL_WIDTH:POOL_WIDTH + SG_WIDTH] = (ug[rows] * mixed).astype(mix_ref.dtype)

    o_cq = POOL_WIDTH + 2 * SG_WIDTH
    o_ckv = o_cq + Q_LORA
    o_kpe = o_ckv + KV_LORA
    cos = cos_ref[...]
    sin = sin_ref[...]
    cqn = (_rms(z[:, o_cq:o_ckv]) * gq_ref[...]).astype(BF16)
    q = _dot(cqn, wuq_ref[...])
    qs = _dot(cqn, wuqs_ref[...])
    ckvn = (_rms(z[:, o_ckv:o_kpe]) * gkv_ref[...]).astype(BF16)
    kn = _dot(ckvn, wk_ref[...])
    kpe = z[:, o_kpe:o_kpe + HEAD_PAD] * cos + z[:, o_kpe + HEAD_PAD:o_kpe + 2 * HEAD_PAD] * sin
    for hh in range(MLA_HEADS):
        blk = slice(hh * HEAD_PAD, (hh + 1) * HEAD_PAD)
        q_ref[:, blk] = ((q[:, blk] * cos + qs[:, blk] * sin) * ATTN_SCALE).astype(q_ref.dtype)
        k_ref[:, blk] = (kn[:, blk] + kpe).astype(k_ref.dtype)
    v_ref[...] = _dot(ckvn, wv_ref[...]).astype(v_ref.dtype)


def _mixer_in(x_parts, shift, scale, cos_t, sin_t, p, seq):
    fused = len(x_parts) == 3
    t, d = x_parts[0].shape
    tm = TOKEN_TILE
    tiles_per_seq = seq // tm
    nz = p["w_in"].shape[1]
    hq = MLA_HEADS * HEAD_PAD

    def full(shape):
        return pl.BlockSpec(shape, lambda i: (0,) * len(shape))

    def mod():
        return pl.BlockSpec((1, 1, d), lambda i: (i // tiles_per_seq, 0, 0))

    x_specs = [pl.BlockSpec((tm, d), lambda i: (i, 0))]
    extra_out_specs, extra_out_shapes = [], []
    if fused:
        x_specs += [pl.BlockSpec((tm * ROW_TILE, LANES), lambda i: (i, 0)), mod()]
        extra_out_specs = [pl.BlockSpec((tm, d), lambda i: (i, 0))]
        extra_out_shapes = [jax.ShapeDtypeStruct((t, d), F32)]
    return pl.pallas_call(
        functools.partial(_mixer_in_kernel, tiles_per_seq=tiles_per_seq, fused=fused),
        grid=(t // tm,),
        in_specs=x_specs + [mod(), mod(),
                  full((d, nz)), full((POOL_WIDTH, POOL_WIDTH)), full((1, POOL_WIDTH)),
                  full((SG_HEADS, SG_CHUNK, SG_CHUNK)), full((SG_CHUNK, SG_WIDTH)),
                  full((1, Q_LORA)), full((Q_LORA, hq)), full((Q_LORA, hq)),
                  full((1, KV_LORA)), full((KV_LORA, hq)), full((KV_LORA, MLA_HEADS * V_DIM)),
                  pl.BlockSpec((tm, HEAD_PAD), lambda i: (i, 0)),
                  pl.BlockSpec((tm, HEAD_PAD), lambda i: (i, 0))],
        out_specs=[pl.BlockSpec((tm, POOL_WIDTH + SG_WIDTH), lambda i: (i, 0)),
                   pl.BlockSpec((tm, hq), lambda i: (i, 0)),
                   pl.BlockSpec((tm, hq), lambda i: (i, 0)),
                   pl.BlockSpec((tm, MLA_HEADS * V_DIM), lambda i: (i, 0))] + extra_out_specs,
        out_shape=[jax.ShapeDtypeStruct((t, POOL_WIDTH + SG_WIDTH), BF16),
                   jax.ShapeDtypeStruct((t, hq), BF16),
                   jax.ShapeDtypeStruct((t, hq), BF16),
                   jax.ShapeDtypeStruct((t, MLA_HEADS * V_DIM), BF16)] + extra_out_shapes,
        scratch_shapes=[pltpu.VMEM((MAX_WINDOW, POOL_WIDTH), F32)],
        compiler_params=pltpu.CompilerParams(dimension_semantics=("arbitrary",),
                                             vmem_limit_bytes=VMEM_LIMIT),
        name="mixer_in",
    )(*x_parts, shift, scale, p["w_in"], p["w_pool"], p["pool_scale"], p["w_spatial"], p["b_spatial"],
      p["g_q"], p["w_uq"], p["w_uq_sw"], p["g_kv"], p["w_k"], p["w_v"], cos_t, sin_t)


ATTN_TQ = 256
ATTN_LOOKAHEAD = 2


def _attn_kernel(q_ref, k_ref, v_ref, o_ref):
    seq = q_ref.shape[0]
    tq = ATTN_TQ
    r_i = lax.broadcasted_iota(jnp.int32, (tq, tq), 0)
    c_i = lax.broadcasted_iota(jnp.int32, (tq, tq), 1)
    lane = lax.broadcasted_iota(jnp.int32, (tq, 2 * V_DIM), 1)

    def scores(i, hh):
        lo, hi = i * tq, (i + 1) * tq
        blk = slice(hh * HEAD_PAD, (hh + 1) * HEAD_PAD)
        q = q_ref[lo:hi, blk]
        s_d = jnp.where(c_i <= r_i, _dot_nt(q, k_ref[lo:hi, blk]), -jnp.inf)
        s_o = _dot_nt(q, k_ref[0:lo, blk]) if i > 0 else None
        return s_d, s_o

    def finish(i, s_d, s_o):
        lo, hi = i * tq, (i + 1) * tq
        m = jnp.max(s_d, axis=-1, keepdims=True)
        if s_o is not None:
            m = jnp.maximum(m, jnp.max(s_o, axis=-1, keepdims=True))
            p_o = jnp.exp(s_o - m)
        p_d = jnp.exp(s_d - m)
        l = jnp.sum(p_d, axis=-1, keepdims=True)
        acc = _dot(p_d.astype(BF16), v_ref[lo:hi, :])
        if s_o is not None:
            l = l + jnp.sum(p_o, axis=-1, keepdims=True)
            acc = acc + _dot(p_o.astype(BF16), v_ref[0:lo, :])
        return acc * (1.0 / l)

    chains = [(i, hh) for i in reversed(range(seq // tq)) for hh in range(2)]
    pending = [scores(*chains[c]) for c in range(ATTN_LOOKAHEAD)]
    outs = []
    for c, (i, hh) in enumerate(chains):
        if c + ATTN_LOOKAHEAD < len(chains):
            pending.append(scores(*chains[c + ATTN_LOOKAHEAD]))
        outs.append(finish(i, *pending.pop(0)))
        if hh == 1:
            o_ref[i * tq:(i + 1) * tq, :] = jnp.where(lane < V_DIM, outs[-2], outs[-1]).astype(o_ref.dtype)


def _attention(q, k, v, batch, seq):
    t = q.shape[0]
    return pl.pallas_call(
        _attn_kernel,
        grid=(batch, MLA_HEADS // 2),
        in_specs=[pl.BlockSpec((seq, 2 * HEAD_PAD), lambda b, hp: (b, hp)),
                  pl.BlockSpec((seq, 2 * HEAD_PAD), lambda b, hp: (b, hp)),
                  pl.BlockSpec((seq, 2 * V_DIM), lambda b, hp: (b, hp))],
        out_specs=pl.BlockSpec((seq, 2 * V_DIM), lambda b, hp: (b, hp)),
        out_shape=jax.ShapeDtypeStruct((t, MLA_HEADS * V_DIM), BF16),
        compiler_params=pltpu.CompilerParams(
            dimension_semantics=("arbitrary", "arbitrary"), vmem_limit_bytes=VMEM_LIMIT),
        name="attention",
    )(q, k, v)


def _first_max_index(cur, idx, sentinel):
    m = jnp.max(cur, axis=0, keepdims=True)
    first = jnp.min(jnp.where(cur == m, idx, sentinel), axis=0, keepdims=True)
    return m, first


def _route(logits_t, bias_t):
    n_tok = logits_t.shape[1]
    scores = jax.nn.sigmoid(logits_t)
    sel = scores + bias_t
    neg = -jnp.inf
    sub = lax.broadcasted_iota(jnp.int32, (GROUP_SIZE, n_tok), 0).astype(F32)
    gid = lax.broadcasted_iota(jnp.int32, (N_EXPERT_GROUPS, n_tok), 0).astype(F32)
    gscore = jnp.zeros((N_EXPERT_GROUPS, n_tok), F32)
    for g in range(N_EXPERT_GROUPS):
        s = sel[g * GROUP_SIZE:(g + 1) * GROUP_SIZE]
        m1, i1 = _first_max_index(s, sub, float(GROUP_SIZE))
        m2 = jnp.max(jnp.where(sub == i1, neg, s), axis=0, keepdims=True)
        gscore = jnp.where(gid == float(g), m1 + m2, gscore)
    eid = lax.broadcasted_iota(jnp.int32, (N_EXPERTS, n_tok), 0).astype(F32)
    egroup = jnp.floor(eid * (1.0 / GROUP_SIZE))
    allowed = jnp.zeros((N_EXPERTS, n_tok), F32)
    cur = gscore
    for _ in range(TOPK_GROUPS):
        _, gi = _first_max_index(cur, gid, float(N_EXPERT_GROUPS))
        cur = jnp.where(gid == gi, neg, cur)
        allowed = jnp.where(egroup == gi, 1.0, allowed)
    cur = jnp.where(allowed > 0.0, sel, neg)
    chosen = jnp.zeros((N_EXPERTS, n_tok), F32)
    for _ in range(TOP_K):
        _, ei = _first_max_index(cur, eid, float(N_EXPERTS))
        hit = eid == ei
        cur = jnp.where(hit, neg, cur)
        chosen = jnp.where(hit, 1.0, chosen)
    w = jnp.where(chosen > 0.0, scores, 0.0)
    return w / jnp.sum(w, axis=0, keepdims=True) * ROUTED_SCALE


def _split_bf16(x):
    hi = x.astype(BF16)
    lo = (x - hi.astype(F32)).astype(BF16)
    return hi, lo


ROW_TILE = 8
ROUTE_SPLIT = 2


def _mixer_out_kernel(mix_ref, att_ref, x_ref, gate1_ref, shift_ref, scale_ref, gate2_ref, wo_ref, wr_ref,
                      rb_ref, sg_ref, su_ref, sd_ref, base_ref, h2r_ref, cw_ref):
    tm = x_ref.shape[0]
    half = mix_ref.shape[1]
    y = _dot(mix_ref[...], wo_ref[0:half, :]) + _dot(att_ref[...], wo_ref[half:, :])
    x2 = x_ref[...] + gate1_ref[0] * y
    h2 = _rms(x2) * (1.0 + scale_ref[0]) + shift_ref[0]
    for j in range(ROW_TILE):
        h2r_ref[pl.ds(j, tm, stride=ROW_TILE), :] = h2[:, j * LANES:(j + 1) * LANES]
    hb = h2.astype(BF16)
    act = _silu(_dot(hb, sg_ref[...])) * _dot(hb, su_ref[...])
    base_ref[...] = x2 + gate2_ref[0] * _dot(act.astype(BF16), sd_ref[...])
    h_hi, h_lo = _split_bf16(h2)
    w_hi, w_lo = _split_bf16(wr_ref[...])
    logits_t = _dot_nt(w_hi, h_hi) + (_dot_nt(w_hi, h_lo) + _dot_nt(w_lo, h_hi))
    step = tm // ROUTE_SPLIT
    for g in range(ROUTE_SPLIT):
        cw_ref[:, g * step:(g + 1) * step] = _route(logits_t[:, g * step:(g + 1) * step], rb_ref[...])


def _mixer_out(mix, att, x2d, gate1, shift, scale, gate2, p, seq):
    t, d = x2d.shape
    tm = TOKEN_TILE
    tiles_per_seq = seq // tm
    ff = p["ws_gate"].shape[1]

    def full(shape):
        return pl.BlockSpec(shape, lambda i: (0,) * len(shape))

    def mod():
        return pl.BlockSpec((1, 1, d), lambda i: (i // tiles_per_seq, 0, 0))

    return pl.pallas_call(
        _mixer_out_kernel,
        grid=(t // tm,),
        in_specs=[pl.BlockSpec((tm, mix.shape[1]), lambda i: (i, 0)),
                  pl.BlockSpec((tm, att.shape[1]), lambda i: (i, 0)),
                  pl.BlockSpec((tm, d), lambda i: (i, 0)), mod(), mod(), mod(), mod(),
                  full((d, d)), full((N_EXPERTS, d)), full((N_EXPERTS, 1)),
                  full((d, ff)), full((d, ff)), full((ff, d))],
        out_specs=[pl.BlockSpec((tm, d), lambda i: (i, 0)),
                   pl.BlockSpec((tm * ROW_TILE, LANES), lambda i: (i, 0)),
                   pl.BlockSpec((N_EXPERTS, tm), lambda i: (0, i))],
        out_shape=[jax.ShapeDtypeStruct((t, d), F32),
                   jax.ShapeDtypeStruct((t * ROW_TILE, LANES), F32),
                   jax.ShapeDtypeStruct((N_EXPERTS, t), F32)],
        compiler_params=pltpu.CompilerParams(dimension_semantics=("arbitrary",),
                                             vmem_limit_bytes=VMEM_LIMIT),
        name="mixer_out",
    )(mix, att, x2d, gate1, shift, scale, gate2, p["w_out"], p["w_router_t"], p["router_bias"],
      p["ws_gate"], p["ws_up"], p["ws_down"])


EXPERT_BLOCK = 128
TOK_BITS = 12
RMW_BATCH = 8
MOE_TOKENS = 4096
MOE_GROUP = 5
GROUP_ROWS = MOE_GROUP * EXPERT_BLOCK


def _list_len(ts):
    return -(-ts // GROUP_ROWS) * GROUP_ROWS


def _plan_kernel(cw_ref, tok_ref, w_ref, cnt_ref):
    n_e, ts = cw_ref.shape
    cw = cw_ref[...]
    chosen = cw > 0.0
    cf = jnp.where(chosen, 1.0, 0.0).astype(BF16)
    r_i = lax.broadcasted_iota(jnp.int32, (LANES, LANES), 0)
    c_i = lax.broadcasted_iota(jnp.int32, (LANES, LANES), 1)
    before = jnp.where(r_i < c_i, 1.0, 0.0).astype(BF16)
    ones = jnp.ones((LANES, LANES), BF16)
    carry = jnp.zeros((n_e, LANES), F32)
    ranks = []
    for k in range(ts // LANES):
        ck = cf[:, k * LANES:(k + 1) * LANES]
        ranks.append(_dot(ck, before) + carry)
        carry = carry + _dot(ck, ones)
    rank = jnp.concatenate(ranks, axis=1).astype(jnp.int32)
    lane = lax.broadcasted_iota(jnp.int32, (n_e, ts), 1)
    packed = jnp.where(chosen, ((lane - rank) << TOK_BITS) | lane, -1)
    w = jnp.where(chosen, cw, 0.0)
    for bit in range(ts.bit_length() - 1):
        step = 1 << bit
        src_p = pltpu.roll(packed, ts - step, 1)
        src_w = pltpu.roll(w, ts - step, 1)
        take = (src_p >= 0) & (lane < ts - step) & (((src_p >> (TOK_BITS + bit)) & 1) == 1)
        keep = (packed >= 0) & (((packed >> (TOK_BITS + bit)) & 1) == 0)
        packed = jnp.where(take, src_p, jnp.where(keep, packed, -1))
        w = jnp.where(take, src_w, jnp.where(keep, w, 0.0))
    count = carry[:, 0:1].astype(jnp.int32)
    valid = packed >= 0
    tok = packed & ((1 << TOK_BITS) - 1)
    tok_last = jnp.max(jnp.where(valid, tok, 0).astype(F32), axis=1, keepdims=True).astype(jnp.int32)
    w_last = jnp.sum(jnp.where(lane == count - 1, w, 0.0), axis=1, keepdims=True)
    batch_end = (count + (RMW_BATCH - 1)) & (-RMW_BATCH)
    tok = jnp.where(valid, tok, tok_last)
    w = jnp.where(valid, w, jnp.where(lane < batch_end, w_last, 0.0))
    n_tail = tok_ref.shape[2] - ts
    if n_tail:
        tok = jnp.concatenate([tok, jnp.broadcast_to(tok_last, (n_e, n_tail))], axis=1)
        w = jnp.concatenate([w, jnp.zeros((n_e, n_tail), F32)], axis=1)
    tok_ref[0] = tok * ROW_TILE
    w_ref[0] = w
    cnt_ref[0] = jnp.broadcast_to(count, (n_e, LANES))


def _plan(cw_t, ts):
    n_e, t = cw_t.shape
    n_sup = t // ts
    n_list = _list_len(ts)
    assert ts <= (1 << TOK_BITS)
    return pl.pallas_call(
        _plan_kernel,
        grid=(n_sup,),
        in_specs=[pl.BlockSpec((n_e, ts), lambda s: (0, s))],
        out_specs=[pl.BlockSpec((1, n_e, n_list), lambda s: (s, 0, 0)),
                   pl.BlockSpec((1, n_e, n_list), lambda s: (s, 0, 0)),
                   pl.BlockSpec((1, n_e, LANES), lambda s: (s, 0, 0))],
        out_shape=[jax.ShapeDtypeStruct((n_sup, n_e, n_list), jnp.int32),
                   jax.ShapeDtypeStruct((n_sup, n_e, n_list), F32),
                   jax.ShapeDtypeStruct((n_sup, n_e, LANES), jnp.int32)],
        compiler_params=pltpu.CompilerParams(dimension_semantics=("arbitrary",),
                                             vmem_limit_bytes=VMEM_LIMIT),
        name="plan",
    )(cw_t)


XT_PITCH = GROUP_ROWS + 1


SHORT_ROWS = 528


def _moe_gather(list_ref, first, h_ref, xt_ref, lo=0, hi=GROUP_ROWS):
    for r in range(lo, hi):
        start = pl.multiple_of(list_ref[0, 0, first + r], ROW_TILE)
        xt_ref[pl.ds(r, ROW_TILE, stride=XT_PITCH), :] = h_ref[pl.ds(start, ROW_TILE), :]


def _moe_scatter_add(list_ref, first, yt_ref, acc_ref, lo=0, hi=GROUP_ROWS):
    for grp in range(lo // RMW_BATCH, hi // RMW_BATCH):
        rows = range(grp * RMW_BATCH, (grp + 1) * RMW_BATCH)
        starts = [pl.multiple_of(list_ref[0, 0, first + r], ROW_TILE) for r in rows]
        olds = [acc_ref[pl.ds(st, ROW_TILE), :] for st in starts]
        news = [old + yt_ref[pl.ds(r, ROW_TILE, stride=XT_PITCH), :] for r, old in zip(rows, olds)]
        for st, new in zip(starts, news):
            acc_ref[pl.ds(st, ROW_TILE), :] = new


def _moe_experts(q, xt_ref, w_ref, wg_ref, wu_ref, wd_ref, yt_ref, rows=GROUP_ROWS):
    x = jnp.concatenate(
        [xt_ref[pl.ds(j * XT_PITCH, rows), :] for j in range(ROW_TILE)], axis=1).astype(BF16)
    act = _silu(_dot(x, wg_ref[0, 0].astype(BF16))) * _dot(x, wu_ref[0, 0].astype(BF16))
    y = _dot(act.astype(BF16), wd_ref[0, 0].astype(BF16))
    w_col = jnp.concatenate(
        [jnp.broadcast_to(w_ref[0, q * MOE_GROUP + k], (EXPERT_BLOCK, EXPERT_BLOCK)).T
         for k in range(MOE_GROUP)], axis=0)[0:rows]
    for j in range(ROW_TILE):
        yt_ref[pl.ds(j * XT_PITCH, rows), :] = y[:, j * LANES:(j + 1) * LANES] * w_col


def _groups(count):
    return jnp.maximum((count + (GROUP_ROWS - 1)) // GROUP_ROWS, 1)


def _moe_kernel(cnt_ref, tokp_ref, tok_ref, tokn_ref, w_ref, h_ref, wg_ref, wu_ref, wd_ref, acc_ref,
                xt0_ref, xt1_ref, yt_ref):
    s = pl.program_id(0)
    e = pl.program_id(1)
    last_e = pl.num_programs(1) - 1
    cnt = cnt_ref[s, e]
    cnt_prev = cnt_ref[s, jnp.maximum(e - 1, 0)]
    cnt_next = cnt_ref[s, jnp.minimum(e + 1, last_e)]
    n = _groups(cnt)
    n_prev = _groups(cnt_prev)
    prev_first = (n_prev - 1) * GROUP_ROWS
    short = cnt <= SHORT_ROWS

    @pl.when(e == 0)
    def _():
        acc_ref[...] = jnp.zeros_like(acc_ref)
        yt_ref[...] = jnp.zeros_like(yt_ref)
        _moe_gather(tok_ref, 0, h_ref, xt0_ref)

    def experts(q, xt_ref, rows=GROUP_ROWS):
        _moe_experts(q, xt_ref, w_ref, wg_ref, wu_ref, wd_ref, yt_ref, rows)

    def step(xt_cur, xt_next):
        @pl.when(cnt_prev > SHORT_ROWS)
        def _():
            _moe_scatter_add(tokp_ref, prev_first, yt_ref, acc_ref, SHORT_ROWS, GROUP_ROWS)

        @pl.when(cnt_next > SHORT_ROWS)
        def _():
            _moe_gather(tokn_ref, 0, h_ref, xt_next, SHORT_ROWS, GROUP_ROWS)

        single = (n == 1) & (n_prev == 1)

        @pl.when(single & short)
        def _():
            _moe_gather(tokn_ref, 0, h_ref, xt_next, 0, SHORT_ROWS)
            _moe_scatter_add(tokp_ref, 0, yt_ref, acc_ref, 0, SHORT_ROWS)
            experts(0, xt_cur, SHORT_ROWS)

        @pl.when(single & jnp.logical_not(short))
        def _():
            _moe_gather(tokn_ref, 0, h_ref, xt_next, 0, SHORT_ROWS)
            _moe_scatter_add(tokp_ref, 0, yt_ref, acc_ref, 0, SHORT_ROWS)
            experts(0, xt_cur)

        @pl.when(jnp.logical_not(single))
        def _():
            _moe_gather(tokn_ref, 0, h_ref, xt_next, 0, SHORT_ROWS)
            _moe_scatter_add(tokp_ref, prev_first, yt_ref, acc_ref, 0, SHORT_ROWS)
            _moe_gather(tok_ref, 0, h_ref, xt_cur, SHORT_ROWS, GROUP_ROWS)
            experts(0, xt_cur)

            def more(q, carry):
                _moe_scatter_add(tok_ref, (q - 1) * GROUP_ROWS, yt_ref, acc_ref)
                _moe_gather(tok_ref, q * GROUP_ROWS, h_ref, xt_cur)
                experts(q, xt_cur)
                return carry

            lax.fori_loop(1, n, more, 0)

    @pl.when(e % 2 == 0)
    def _():
        step(xt0_ref, xt1_ref)

    @pl.when(e % 2 == 1)
    def _():
        step(xt1_ref, xt0_ref)

    @pl.when(e == last_e)
    def _():
        _moe_scatter_add(tok_ref, (n - 1) * GROUP_ROWS, yt_ref, acc_ref, 0, SHORT_ROWS)

    @pl.when((e == last_e) & (cnt > SHORT_ROWS))
    def _():
        _moe_scatter_add(tok_ref, (n - 1) * GROUP_ROWS, yt_ref, acc_ref, SHORT_ROWS, GROUP_ROWS)


def _moe(h2rows, tok, wts, counts, wg, wu, wd, layer, ts):
    rows = h2rows.shape[0]
    n_sup = rows // (ts * ROW_TILE)
    d, ff = wg.shape[2], wg.shape[3]
    n_list = tok.shape[2]
    assert d == ROW_TILE * LANES and n_list == _list_len(ts)

    assert N_EXPERTS % 2 == 0

    def list_idx(shift):
        def idx(s, e, cnt):
            return (s * N_EXPERTS + jnp.clip(e + shift, 0, N_EXPERTS - 1), 0, 0)
        return idx

    def sup_idx(s, e, cnt):
        return (s, 0)

    def w_idx(s, e, cnt):
        return (layer, e, 0, 0)

    def list_spec(shift):
        return pl.BlockSpec((1, 1, n_list), list_idx(shift), memory_space=pltpu.SMEM)

    once = pl.Buffered(1)
    tile = pltpu.VMEM((ROW_TILE * XT_PITCH, LANES), F32)
    grid_spec = pltpu.PrefetchScalarGridSpec(
        num_scalar_prefetch=1,
        grid=(n_sup, N_EXPERTS),
        in_specs=[list_spec(-1), list_spec(0), list_spec(1),
                  pl.BlockSpec((1, n_list // EXPERT_BLOCK, 1, EXPERT_BLOCK),
                               lambda s, e, cnt: (s * N_EXPERTS + e, 0, 0, 0)),
                  pl.BlockSpec((ts * ROW_TILE, LANES), sup_idx, pipeline_mode=once),
                  pl.BlockSpec((1, 1, d, ff), w_idx),
                  pl.BlockSpec((1, 1, d, ff), w_idx),
                  pl.BlockSpec((1, 1, ff, d), w_idx)],
        out_specs=pl.BlockSpec((ts * ROW_TILE, LANES), sup_idx, pipeline_mode=once),
        scratch_shapes=[tile, tile, tile],
    )
    lists = tok.reshape(n_sup * N_EXPERTS, 1, n_list)
    return pl.pallas_call(
        _moe_kernel,
        grid_spec=grid_spec,
        out_shape=jax.ShapeDtypeStruct((rows, LANES), F32),
        compiler_params=pltpu.CompilerParams(dimension_semantics=("arbitrary", "arbitrary"),
                                             vmem_limit_bytes=VMEM_LIMIT),
        name="moe",
    )(counts, lists, lists, lists,
      wts.reshape(n_sup * N_EXPERTS, n_list // EXPERT_BLOCK, 1, EXPERT_BLOCK), h2rows, wg, wu, wd)


def _routed_rows_to_tile(r_ref, tm):
    return jnp.concatenate([r_ref[pl.ds(j, tm, stride=ROW_TILE), :] for j in range(ROW_TILE)], axis=1)


def _final_kernel(base_ref, r_ref, gate_ref, fg_ref, o_ref):
    tm = base_ref.shape[0]
    x = base_ref[...] + gate_ref[0] * _routed_rows_to_tile(r_ref, tm)
    o_ref[...] = _rms(x) * fg_ref[...]


def _final(base, routed, gate, final_gain, seq):
    t, d = base.shape
    tm = TOKEN_TILE
    tiles_per_seq = seq // tm
    return pl.pallas_call(
        _final_kernel,
        grid=(t // tm,),
        in_specs=[pl.BlockSpec((tm, d), lambda i: (i, 0)),
                  pl.BlockSpec((tm * ROW_TILE, LANES), lambda i: (i, 0)),
                  pl.BlockSpec((1, 1, d), lambda i: (i // tiles_per_seq, 0, 0)),
                  pl.BlockSpec((1, d), lambda i: (0, 0))],
        out_specs=pl.BlockSpec((tm, d), lambda i: (i, 0)),
        out_shape=jax.ShapeDtypeStruct((t, d), F32),
        compiler_params=pltpu.CompilerParams(dimension_semantics=("arbitrary",),
                                             vmem_limit_bytes=VMEM_LIMIT),
        name="final_norm",
    )(base, routed, gate, final_gain)


def _prep_layer(w_in, w_pool, pool_scale, w_spatial, b_spatial, g_q, w_uq, g_kv, w_ukv, w_out,
                w_router, router_bias, ws_gate, ws_up, ws_down):
    d = w_in.shape[0]
    o_kpe = POOL_WIDTH + 2 * SG_WIDTH + Q_LORA + KV_LORA
    x1 = w_in[:, o_kpe:o_kpe + HALF_ROPE]
    x2 = w_in[:, o_kpe + HALF_ROPE:o_kpe + QK_ROPE]
    zl = jnp.zeros((d, QK_NOPE), F32)
    zr = jnp.zeros((d, HEAD_PAD - QK_NOPE - QK_ROPE), F32)
    w_in_pad = jnp.concatenate([w_in[:, :o_kpe], zl, x1, x2, zr, zl, x2, x1, zr], axis=1)

    uq = w_uq.reshape(Q_LORA, MLA_HEADS, QK_NOPE + QK_ROPE)
    q1 = uq[..., QK_NOPE:QK_NOPE + HALF_ROPE]
    q2 = uq[..., QK_NOPE + HALF_ROPE:]
    zq = jnp.zeros((Q_LORA, MLA_HEADS, HEAD_PAD - QK_NOPE - QK_ROPE), F32)
    w_uq_pad = jnp.concatenate([uq, zq], axis=-1).reshape(Q_LORA, MLA_HEADS * HEAD_PAD)
    w_uq_sw = jnp.concatenate([jnp.zeros_like(uq[..., :QK_NOPE]), q2, q1, zq], axis=-1)
    w_uq_sw = w_uq_sw.reshape(Q_LORA, MLA_HEADS * HEAD_PAD)

    ukv = w_ukv.reshape(KV_LORA, MLA_HEADS, QK_NOPE + V_DIM)
    zk = jnp.zeros((KV_LORA, MLA_HEADS, HEAD_PAD - QK_NOPE), F32)
    w_k = jnp.concatenate([ukv[..., :QK_NOPE], zk], axis=-1).reshape(KV_LORA, MLA_HEADS * HEAD_PAD)
    w_v = ukv[..., QK_NOPE:].reshape(KV_LORA, MLA_HEADS * V_DIM)

    w_pool_bd = jax.scipy.linalg.block_diag(*[w_pool[g] for g in range(len(POOL_WINDOWS))])
    b_sp = jnp.repeat(b_spatial.T, SG_DIM, axis=1)
    return {
        "w_in": w_in_pad.astype(BF16), "w_pool": w_pool_bd.astype(BF16),
        "pool_scale": pool_scale.reshape(1, -1), "w_spatial": w_spatial, "b_spatial": b_sp,
        "g_q": g_q.reshape(1, -1), "w_uq": w_uq_pad.astype(BF16), "w_uq_sw": w_uq_sw.astype(BF16),
        "g_kv": g_kv.reshape(1, -1), "w_k": w_k.astype(BF16), "w_v": w_v.astype(BF16),
        "w_out": w_out.astype(BF16), "w_router_t": w_router.T, "router_bias": router_bias.reshape(-1, 1),
        "ws_gate": ws_gate.astype(BF16), "ws_up": ws_up.astype(BF16), "ws_down": ws_down.astype(BF16),
    }


def kernel(x, c, positions, w_ada, b_ada, w_in, w_pool, pool_scale, w_spatial, b_spatial, g_q, w_uq, g_kv, w_ukv, w_out, w_router, router_bias, w_gate, w_up, w_down, ws_gate, ws_up, ws_down, final_gain):
    batch, seq, d = x.shape
    depth = w_ada.shape[0]
    mod = _ada_mod(c, w_ada, b_ada)
    cos_t, sin_t = _rope_tables(positions)
    x_parts = (x.reshape(batch * seq, d),)
    fg = final_gain.reshape(1, d)
    wg_b, wu_b, wd_b = w_gate, w_up, w_down
    ts = min(MOE_TOKENS, batch * seq)
    for l in range(depth):
        p = _prep_layer(w_in[l], w_pool[l], pool_scale[l], w_spatial[l], b_spatial[l], g_q[l], w_uq[l],
                        g_kv[l], w_ukv[l], w_out[l], w_router[l], router_bias[l], ws_gate[l], ws_up[l],
                        ws_down[l])
        shift1, scale1, gate1, shift2, scale2, gate2 = [
            mod[l, :, k * d:(k + 1) * d].reshape(batch, 1, d) for k in range(6)]
        outs = _mixer_in(x_parts, shift1, scale1, cos_t, sin_t, p, seq)
        mix, q, k, v = outs[:4]
        xt = outs[4] if len(outs) == 5 else x_parts[0]
        att = _attention(q, k, v, batch, seq)
        base, h2rows, cw_t = _mixer_out(mix, att, xt, gate1, shift2, scale2, gate2, p, seq)
        tok, wts, cnt = _plan(cw_t, ts)
        routed = _moe(h2rows, tok, wts, cnt[:, :, 0], wg_b, wu_b, wd_b, l, ts)
        x_parts = (base, routed, gate2)
    return _final(*x_parts, fg, seq).reshape(batch, seq, d)
```
